```python
import math, functools
import jax, jax.numpy as jnp
from jax import lax
import numpy as np

D_MODEL = 1024
BATCH = 8
SEQ = 4096
DEPTH = 1
DEC_BATCH = 128
DEC_SEQ = 8
PAST_LEN = 16384
PAGE_SIZE = 128

BRANCH_WIDTH = 512
N_BRANCH = 3
A_HEADS = 8
A_NOPE = 64
A_ROPE = 32
A_QK = A_NOPE + A_ROPE
A_VDIM = BRANCH_WIDTH // A_HEADS
KV_LORA = 256
Q_LORA = 384
ROPE_BASE = 10000.0
Q_BLOCK = 128
B_HEADS = 8
B_EXPAND = 128
B_VDIM = BRANCH_WIDTH // B_HEADS
B_FDIM = B_HEADS * B_EXPAND
HGRN_CHUNK = 32
N_MEM = 256
C_HEADS = 4
C_HDIM = BRANCH_WIDTH // C_HEADS
EPS = 1e-6

SPLITS = (Q_LORA, KV_LORA, A_ROPE, BRANCH_WIDTH, B_FDIM, B_FDIM, BRANCH_WIDTH, BRANCH_WIDTH,
          BRANCH_WIDTH, BRANCH_WIDTH, N_BRANCH * D_MODEL)
D_IN = sum(SPLITS)

kernel_name = "hybrid_mla_hgrn2_memxattn_gated_step"


def rmsnorm(x, g):
    xf = x.astype(jnp.float32)
    y = xf * lax.rsqrt(jnp.mean(xf * xf, axis=-1, keepdims=True) + EPS)
    return (y * g.astype(jnp.float32)).astype(x.dtype)


def rope(x, pos):
    half = A_ROPE // 2
    inv = jnp.exp(-math.log(ROPE_BASE) * jnp.arange(half, dtype=jnp.float32) / half)
    ang = pos.astype(jnp.float32)[:, None] * inv[None, :]
    shape = (pos.shape[0],) + (1,) * (x.ndim - 3) + (half,)
    cos = jnp.cos(ang).reshape(shape)
    sin = jnp.sin(ang).reshape(shape)
    x1 = x[..., :half].astype(jnp.float32)
    x2 = x[..., half:].astype(jnp.float32)
    return jnp.concatenate([x1 * cos - x2 * sin, x2 * cos + x1 * sin], axis=-1).astype(x.dtype)


def mla_prompt(q_nope, q_pe, ckv, k_pe, w_uk, w_uv):
    b, t = ckv.shape[:2]
    k_nope = jnp.einsum('btc,che->bthe', ckv, w_uk)
    v = jnp.einsum('btc,chv->bthv', ckv, w_uv)
    scale = A_QK ** -0.5
    nblk = t // Q_BLOCK
    qn = q_nope.reshape(b, nblk, Q_BLOCK, A_HEADS, A_NOPE).swapaxes(0, 1)
    qp = q_pe.reshape(b, nblk, Q_BLOCK, A_HEADS, A_ROPE).swapaxes(0, 1)
    kpos = jnp.arange(t)

    def block(args):
        qn_b, qp_b, j = args
        s = (jnp.einsum('bqhe,bkhe->bhqk', qn_b, k_nope)
             + jnp.einsum('bqhr,bkr->bhqk', qp_b, k_pe)).astype(jnp.float32) * scale
        qpos = j * Q_BLOCK + jnp.arange(Q_BLOCK)
        s = jnp.where(kpos[None, :] <= qpos[:, None], s, -jnp.inf)
        p = jax.nn.softmax(s, axis=-1).astype(v.dtype)
        return jnp.einsum('bhqk,bkhv->bqhv', p, v)

    o = lax.map(block, (qn, qp, jnp.arange(nblk)))
    return o.swapaxes(0, 1).reshape(b, t, BRANCH_WIDTH)


def mla_sample(q_nope, q_pe, ckv, k_pe, lat_pool, rope_pool, page_table, w_uk, w_uv):
    b, t = ckv.shape[:2]
    past_lat = lat_pool[page_table].reshape(b, -1, KV_LORA)
    past_rope = rope_pool[page_table].reshape(b, -1, A_ROPE)
    n_past = past_lat.shape[1]
    q_lat = jnp.einsum('bthe,che->bthc', q_nope, w_uk)
    scale = A_QK ** -0.5
    s_past = (jnp.einsum('bthc,bkc->bhtk', q_lat, past_lat)
              + jnp.einsum('bthr,bkr->bhtk', q_pe, past_rope)).astype(jnp.float32) * scale
    s_new = (jnp.einsum('bthc,bkc->bhtk', q_lat, ckv)
             + jnp.einsum('bthr,bkr->bhtk', q_pe, k_pe)).astype(jnp.float32) * scale
    causal = jnp.tril(jnp.ones((t, t), dtype=bool))
    s_new = jnp.where(causal, s_new, -jnp.inf)
    p = jax.nn.softmax(jnp.concatenate([s_past, s_new], axis=-1), axis=-1).astype(ckv.dtype)
    o_lat = (jnp.einsum('bhtk,bkc->bthc', p[..., :n_past], past_lat)
             + jnp.einsum('bhtk,bkc->bthc', p[..., n_past:], ckv))
    o = jnp.einsum('bthc,chv->bthv', o_lat, w_uv)
    return o.reshape(b, t, BRANCH_WIDTH)


def hgrn2_recurrence(q, k, logf, v, s0):
    b, t = q.shape[:2]
    c = math.gcd(t, HGRN_CHUNK)
    n = t // c

    def to_chunks(a):
        return a.astype(jnp.float32).reshape((b, n, c) + a.shape[2:]).swapaxes(0, 1)

    mask = jnp.tril(jnp.ones((c, c), dtype=bool))[None, :, :, None, None]

    def step(S, inp):
        qc, kc, gc, vc = inp
        bcum = jnp.cumsum(gc, axis=1)
        diff = bcum[:, :, None] - bcum[:, None, :]
        dec = jnp.exp(jnp.where(mask, diff, -jnp.inf))
        att = jnp.einsum('bthk,btshk,bshk->bhts', qc, dec, kc)
        o = (jnp.einsum('bhts,bshv->bthv', att, vc)
             + jnp.einsum('bthk,bhkv->bthv', qc * jnp.exp(bcum), S))
        last = bcum[:, -1]
        kdec = kc * jnp.exp(last[:, None] - bcum)
        S = jnp.exp(last)[..., None] * S + jnp.einsum('bshk,bshv->bhkv', kdec, vc)
        return S, o

    S, o = lax.scan(step, s0.astype(jnp.float32),
                    (to_chunks(q), to_chunks(k), to_chunks(logf), to_chunks(v)))
    return o.swapaxes(0, 1).reshape(v.shape).astype(v.dtype), S


def memory_kv(mem, g_mem, w_mem_kv):
    b = mem.shape[0]
    kv = jnp.einsum('bmd,de->bme', rmsnorm(mem, g_mem), w_mem_kv)
    mk, mv = jnp.split(kv, 2, axis=-1)
    return (mk.reshape(b, -1, C_HEADS, C_HDIM), mv.reshape(b, -1, C_HEADS, C_HDIM))


def cross_attn(q_raw, mem_k, mem_v):
    b, t, _ = q_raw.shape
    q = q_raw.reshape(b, t, C_HEADS, C_HDIM)
    s = jnp.einsum('bthe,bmhe->bhtm', q, mem_k).astype(jnp.float32) * (C_HDIM ** -0.5)
    p = jax.nn.softmax(s, axis=-1).astype(mem_v.dtype)
    return jnp.einsum('bhtm,bmhe->bthe', p, mem_v).reshape(b, t, BRANCH_WIDTH)


def mixer_layer(x, pos, mem_k, mem_v, s0, attend, lb, g_norm, w_in, g_q, w_uq, g_kv, g_ho,
                w_branch, w_out):
    b, t, _ = x.shape
    h = rmsnorm(x, g_norm)
    z = jnp.einsum('btd,de->bte', h, w_in)
    (cq, ckv_raw, kr, z_a, f_raw, qb_raw, i_raw, z_b, qc_raw, z_c, g_raw) = jnp.split(
        z, np.cumsum(SPLITS)[:-1], axis=-1)
    q = jnp.einsum('btr,rhe->bthe', rmsnorm(cq, g_q), w_uq)
    q_nope = q[..., :A_NOPE]
    q_pe = rope(q[..., A_NOPE:], pos)
    ckv = rmsnorm(ckv_raw, g_kv)
    k_pe = rope(kr, pos)
    o_a = attend(q_nope, q_pe, ckv, k_pe)
    f = lb + (1.0 - lb) * jax.nn.sigmoid(f_raw.astype(jnp.float32))
    hk = (b, t, B_HEADS, B_EXPAND)
    o_b, s_new = hgrn2_recurrence(jax.nn.silu(qb_raw).reshape(hk), (1.0 - f).reshape(hk),
                                  jnp.log(f).reshape(hk), i_raw.reshape(b, t, B_HEADS, B_VDIM), s0)
    o_b = rmsnorm(o_b, g_ho).reshape(b, t, BRANCH_WIDTH)
    o_c = cross_attn(qc_raw, mem_k, mem_v)
    o = jnp.stack([o_a * jax.nn.silu(z_a), o_b * jax.nn.silu(z_b), o_c * jax.nn.silu(z_c)], axis=2)
    proj = jnp.einsum('btne,ned->btnd', o, w_branch)
    gate = jax.nn.sigmoid(g_raw.astype(jnp.float32)).reshape(b, t, N_BRANCH, D_MODEL).astype(x.dtype)
    y = jnp.einsum('btd,de->bte', jnp.sum(gate * proj, axis=2), w_out)
    return x + y, ckv, k_pe, s_new.astype(x.dtype)


def setup_inputs(seed: int = 0) -> dict:
    key = jax.random.key(seed)
    ks = jax.random.split(key, 24)
    f32 = jnp.float32
    n_pages = PAST_LEN // PAGE_SIZE
    n_used = DEC_BATCH * n_pages
    n_phys = n_used + n_used // 4

    def nrm(k, shape, sc):
        return jax.random.normal(k, shape, f32) * sc

    def gain(k, shape):
        return 1.0 + 0.1 * jax.random.normal(k, shape, f32)

    page_table = jax.random.permutation(ks[5], n_phys)[:n_used].reshape(DEC_BATCH, n_pages).astype(jnp.int32)
    return {
        "x_prompt": nrm(ks[0], (BATCH, SEQ, D_MODEL), 1.0),
        "x_sample": nrm(ks[1], (DEC_BATCH, DEC_SEQ, D_MODEL), 1.0),
        "mem_prompt": nrm(ks[2], (BATCH, N_MEM, D_MODEL), 1.0),
        "cache_kv_latent": nrm(ks[3], (DEPTH, n_phys, PAGE_SIZE, KV_LORA), 1.0),
        "cache_k_rope": nrm(ks[4], (DEPTH, n_phys, PAGE_SIZE, A_ROPE), 1.0),
        "page_table": page_table,
        "state_hgrn": nrm(ks[6], (DEPTH, DEC_BATCH, B_HEADS, B_EXPAND, B_VDIM), 0.5),
        "cache_mem_k": nrm(ks[7], (DEPTH, DEC_BATCH, N_MEM, C_HEADS, C_HDIM), 1.0),
        "cache_mem_v": nrm(ks[8], (DEPTH, DEC_BATCH, N_MEM, C_HEADS, C_HDIM), 1.0),
        "g_norm": gain(ks[9], (DEPTH, D_MODEL)),
        "w_in": nrm(ks[10], (DEPTH, D_MODEL, D_IN), D_MODEL ** -0.5),
        "g_q_lora": gain(ks[11], (DEPTH, Q_LORA)),
        "w_uq": nrm(ks[12], (DEPTH, Q_LORA, A_HEADS, A_QK), Q_LORA ** -0.5),
        "g_kv_lora": gain(ks[13], (DEPTH, KV_LORA)),
        "w_uk": nrm(ks[14], (DEPTH, KV_LORA, A_HEADS, A_NOPE), KV_LORA ** -0.5),
        "w_uv": nrm(ks[15], (DEPTH, KV_LORA, A_HEADS, A_VDIM), KV_LORA ** -0.5),
        "lb_logits": nrm(ks[16], (DEPTH + 1, B_FDIM), 1.0),
        "g_hgrn_out": gain(ks[17], (DEPTH, B_VDIM)),
        "g_mem_norm": gain(ks[18], (DEPTH, D_MODEL)),
        "w_mem_kv": nrm(ks[19], (DEPTH, D_MODEL, 2 * BRANCH_WIDTH), D_MODEL ** -0.5),
        "w_branch": nrm(ks[20], (DEPTH, N_BRANCH, BRANCH_WIDTH, D_MODEL), BRANCH_WIDTH ** -0.5),
        "w_out": nrm(ks[21], (DEPTH, D_MODEL, D_MODEL), D_MODEL ** -0.5),
        "g_final": gain(ks[22], (D_MODEL,)),
    }


def reference(x_prompt, x_sample, mem_prompt, cache_kv_latent, cache_k_rope, page_table, state_hgrn,
              cache_mem_k, cache_mem_v, g_norm, w_in, g_q_lora, w_uq, g_kv_lora, w_uk, w_uv, lb_logits,
              g_hgrn_out, g_mem_norm, w_mem_kv, w_branch, w_out, g_final):
    lb_cum = jnp.cumsum(jax.nn.softmax(lb_logits.astype(jnp.float32), axis=0), axis=0)
    lbs = lb_cum[1:] - lb_cum[:1]
    n_past = page_table.shape[1] * cache_kv_latent.shape[2]
    pos_p = jnp.arange(x_prompt.shape[1])
    pos_s = n_past + jnp.arange(x_sample.shape[1])
    xp, xs = x_prompt, x_sample
    lat_p, rope_p, st_p, mk_pl, mv_pl = [], [], [], [], []
    lat_s, rope_s, st_s = [], [], []
    for l in range(DEPTH):
        shared = (lbs[l], g_norm[l], w_in[l], g_q_lora[l], w_uq[l], g_kv_lora[l], g_hgrn_out[l],
                  w_branch[l], w_out[l])
        mk_p, mv_p = memory_kv(mem_prompt, g_mem_norm[l], w_mem_kv[l])
        s0_p = jnp.zeros((xp.shape[0], B_HEADS, B_EXPAND, B_VDIM), jnp.float32)
        attend_p = functools.partial(mla_prompt, w_uk=w_uk[l], w_uv=w_uv[l])
        xp, ckv_p, kpe_p, s_p = mixer_layer(xp, pos_p, mk_p, mv_p, s0_p, attend_p, *shared)
        attend_s = functools.partial(mla_sample, lat_pool=cache_kv_latent[l], rope_pool=cache_k_rope[l],
                                     page_table=page_table, w_uk=w_uk[l], w_uv=w_uv[l])
        xs, ckv_s, kpe_s, s_s = mixer_layer(xs, pos_s, cache_mem_k[l], cache_mem_v[l], state_hgrn[l],
                                            attend_s, *shared)
        lat_p.append(ckv_p); rope_p.append(kpe_p); st_p.append(s_p); mk_pl.append(mk_p); mv_pl.append(mv_p)
        lat_s.append(ckv_s); rope_s.append(kpe_s); st_s.append(s_s)
    y_prompt = rmsnorm(xp, g_final)
    y_sample = rmsnorm(xs, g_final)
    return (y_prompt, y_sample,
            jnp.stack(lat_p), jnp.stack(rope_p), jnp.stack(st_p), jnp.stack(mk_pl), jnp.stack(mv_pl),
            jnp.stack(lat_s), jnp.stack(rope_s), jnp.stack(st_s))
```

```python
import functools
import math

import numpy as np
import jax
import jax.numpy as jnp
from jax import lax
from jax.experimental import pallas as pl
from jax.experimental.pallas import tpu as pltpu

F32 = jnp.float32
BF16 = jnp.bfloat16
EPS = 1e-6
ROPE_BASE = 10000.0
N_BRANCH = 3
LANES = 128
BF16_ROWS = 16
VMEM_LIMIT = 56 * 1024 * 1024
NT = (((1,), (1,)), ((), ()))
TN = (((0,), (0,)), ((), ()))


def _params(sem):
    return pltpu.CompilerParams(dimension_semantics=sem, vmem_limit_bytes=VMEM_LIMIT)


def _rms(x, g):
    return x * lax.rsqrt(jnp.mean(x * x, axis=-1, keepdims=True) + EPS) * g


def _silu(x):
    return x * jax.nn.sigmoid(x)


def _dot(a, b):
    return jnp.dot(a, b, preferred_element_type=F32)


def _const_spec(shape):
    nd = len(shape)
    return pl.BlockSpec(shape, lambda *_: (0,) * nd)


def _mla_common(x_ref, gn_ref, wa_ref, gq_ref, gkv_ref, ck_ref, sk_ref, h_ref, ckv_ref, kpe_ref,
                q_lora, kv_lora, rope):
    h = _rms(x_ref[...], gn_ref[...]).astype(BF16)
    h_ref[...] = h
    z = _dot(h, wa_ref[...])
    cq_n = _rms(z[:, :q_lora], gq_ref[...]).astype(BF16)
    ckv = _rms(z[:, q_lora:q_lora + kv_lora], gkv_ref[...])
    ckv_ref[...] = ckv
    o = q_lora + kv_lora
    kpe = z[:, o:o + LANES] * ck_ref[...] + z[:, o + LANES:o + 2 * LANES] * sk_ref[...]
    kpe_ref[...] = kpe[:, :rope]
    return cq_n, ckv, kpe


def _mla_prep_prompt_kernel(x_ref, gn_ref, wa_ref, gq_ref, wq_ref, gkv_ref, wkv_ref, cq_ref, sq_ref,
                            ck_ref, sk_ref, h_ref, q_ref, k_ref, v_ref, ckv_ref, kpe_ref, *,
                            scale, q_lora, kv_lora, rope):
    cq_n, ckv, kpe = _mla_common(x_ref, gn_ref, wa_ref, gq_ref, gkv_ref, ck_ref, sk_ref, h_ref,
                                 ckv_ref, kpe_ref, q_lora, kv_lora, rope)
    q2 = _dot(cq_n, wq_ref[...])
    w = q2.shape[1] // 2
    reps = w // LANES
    cos = jnp.tile(cq_ref[...], (1, reps))
    sin = jnp.tile(sq_ref[...], (1, reps))
    q_ref[...] = ((q2[:, :w] * cos + q2[:, w:] * sin) * scale).astype(BF16)
    kv_in = jnp.concatenate([ckv.astype(BF16), kpe.astype(BF16)], axis=1)
    kv = _dot(kv_in, wkv_ref[...])
    wk = kv.shape[1] // 2
    k_ref[...] = kv[:, :wk].astype(BF16)
    v_ref[...] = kv[:, wk:].astype(BF16)


def _mla_prep_sample_kernel(x_ref, gn_ref, wa_ref, gq_ref, wq_ref, gkv_ref, wabs_ref, ck_ref, sk_ref,
                            h_ref, qlat_ref, qrope_ref, ckv_ref, kpe_ref, *,
                            scale, q_lora, kv_lora, rope, nope_w):
    cq_n, _, _ = _mla_common(x_ref, gn_ref, wa_ref, gq_ref, gkv_ref, ck_ref, sk_ref, h_ref,
                             ckv_ref, kpe_ref, q_lora, kv_lora, rope)
    q2 = _dot(cq_n, wq_ref[...])
    q_nope = (q2[:, :nope_w] * scale).astype(BF16)
    qlat_ref[...] = _dot(q_nope, wabs_ref[...])
    w = (q2.shape[1] - nope_w) // 2
    reps = w // LANES
    cos = jnp.tile(ck_ref[...], (1, reps))
    sin = jnp.tile(sk_ref[...], (1, reps))
    qrope_ref[...] = (q2[:, nope_w:nope_w + w] * cos + q2[:, nope_w + w:] * sin) * scale


def _flash_kernel(it_ref, jt_ref, q_ref, k_ref, v_ref, o_ref, m_scr, l_scr, acc_scr, *, heads):
    step = pl.program_id(1)
    i = it_ref[step]
    j = jt_ref[step]
    tq = q_ref.shape[1]

    @pl.when(j == 0)
    def _init():
        m_scr[...] = jnp.full(m_scr.shape, -jnp.inf, F32)
        l_scr[...] = jnp.zeros(l_scr.shape, F32)
        acc_scr[...] = jnp.zeros(acc_scr.shape, F32)

    def update(masked):
        if masked:
            row = lax.broadcasted_iota(jnp.int32, (tq, tq), 0)
            col = lax.broadcasted_iota(jnp.int32, (tq, tq), 1)
            keep = row >= col
        for h in range(heads):
            sl = slice(h * LANES, (h + 1) * LANES)
            s = lax.dot_general(q_ref[0, :, sl], k_ref[0, :, sl], NT, preferred_element_type=F32)
            if masked:
                s = jnp.where(keep, s, -jnp.inf)
            m_prev = m_scr[h]
            m_new = jnp.maximum(m_prev, jnp.max(s, axis=1, keepdims=True))
            alpha = jnp.exp(m_prev - m_new)
            p = jnp.exp(s - m_new)
            l_scr[h] = alpha * l_scr[h] + jnp.sum(p, axis=1, keepdims=True)
            acc_scr[h] = alpha * acc_scr[h] + _dot(p.astype(BF16), v_ref[0, :, sl])
            m_scr[h] = m_new

    @pl.when(j < i)
    def _off_diagonal():
        update(False)

    @pl.when(j == i)
    def _diagonal():
        update(True)
        for p in range(heads // 2):
            even = acc_scr[2 * p] / l_scr[2 * p]
            odd = acc_scr[2 * p + 1] / l_scr[2 * p + 1]
            o_ref[0, :, p * LANES:(p + 1) * LANES] = (even + odd).astype(o_ref.dtype)


def _paged_kernel(pt_ref, qlat_ref, qrope_ref, ckv_ref, kpe_ref, lat_hbm, rope_hbm, o_ref,
                  lat_buf, rope_buf, sem, *, heads, group, page, n_chunks, kv_lora, rope):
    r = pl.program_id(0)
    n_req = pl.num_programs(0)
    t = qlat_ref.shape[0]

    def page_copies(req, chunk, slot, g):
        pg = pt_ref[req, chunk * group + g]
        rows = pl.ds(g * page, page)
        return (pltpu.make_async_copy(lat_hbm.at[pg], lat_buf.at[slot, rows], sem.at[0, slot]),
                pltpu.make_async_copy(rope_hbm.at[pg], rope_buf.at[slot, rows], sem.at[1, slot]))

    def start_chunk(req, chunk, slot):
        for g in range(group):
            for cp in page_copies(req, chunk, slot, g):
                cp.start()

    def wait_chunk(req, chunk, slot):
        for g in range(group):
            for cp in page_copies(req, chunk, slot, g):
                cp.wait()

    @pl.when(r == 0)
    def _prime():
        start_chunk(0, 0, 0)

    q_lat = jnp.concatenate([qlat_ref[:, h * kv_lora:(h + 1) * kv_lora] for h in range(heads)], axis=0)
    q_rope = jnp.concatenate([qrope_ref[:, h * LANES:(h + 1) * LANES] for h in range(heads)],
                             axis=0)[:, :rope]

    def scores(lat, rp):
        return (lax.dot_general(q_lat.astype(lat.dtype), lat, NT, preferred_element_type=F32)
                + lax.dot_general(q_rope.astype(rp.dtype), rp, NT, preferred_element_type=F32))

    def online(carry, s, lat):
        m_prev, l_prev, acc = carry
        m_new = jnp.maximum(m_prev, jnp.max(s, axis=1, keepdims=True))
        alpha = jnp.exp(m_prev - m_new)
        p = jnp.exp(s - m_new)
        return (m_new, alpha * l_prev + jnp.sum(p, axis=1, keepdims=True),
                alpha * acc + _dot(p.astype(lat.dtype), lat))

    def body(c, carry):
        slot = (r * n_chunks + c) % 2

        @pl.when(c + 1 < n_chunks)
        def _next_chunk():
            start_chunk(r, c + 1, 1 - slot)

        @pl.when(jnp.logical_and(c + 1 == n_chunks, r + 1 < n_req))
        def _next_request():
            start_chunk(r + 1, 0, 1 - slot)

        wait_chunk(r, c, slot)
        lat = lat_buf[slot].astype(BF16)
        rp = rope_buf[slot].astype(BF16)
        return online(carry, scores(lat, rp), lat)

    rows = heads * t
    init = (jnp.full((rows, 1), -jnp.inf, F32), jnp.zeros((rows, 1), F32), jnp.zeros((rows, kv_lora), F32))
    carry = lax.fori_loop(0, n_chunks, body, init)

    lat_new = ckv_ref[...]
    s_new = scores(lat_new, kpe_ref[...])
    tok = lax.broadcasted_iota(jnp.int32, (rows, t), 0) % t
    key = lax.broadcasted_iota(jnp.int32, (rows, t), 1)
    s_new = jnp.where(key <= tok, s_new, -jnp.inf)
    _, l_fin, acc = online(carry, s_new, lat_new)
    o = acc / l_fin
    for h in range(heads):
        o_ref[:, h * kv_lora:(h + 1) * kv_lora] = o[h * t:(h + 1) * t].astype(o_ref.dtype)


def _hgrn_project(h_ref, wb_ref, lb_ref, g_scr, k_scr, q_scr, v_scr, fdim, vdim):
    z = _dot(h_ref[...], wb_ref[...])
    l0 = lb_ref[0:1, :]
    l1 = lb_ref[1:2, :]
    mx = jnp.maximum(l0, l1)
    e0 = jnp.exp(l0 - mx)
    e1 = jnp.exp(l1 - mx)
    lb = e1 / (e0 + e1)
    f = lb + (1.0 - lb) * jax.nn.sigmoid(z[:, :fdim])
    g_scr[...] = jnp.log(f)
    k_scr[...] = 1.0 - f
    q_scr[...] = _silu(z[:, fdim:2 * fdim])
    v_scr[...] = z[:, 2 * fdim:2 * fdim + vdim]
    return z[:, 2 * fdim + vdim:]


def _hgrn_chunk(qq, kk, g, v, st_ref, o_ref, rows, *, sub):
    c = qq.shape[0]
    n_pairs = st_ref.shape[0]
    hv = st_ref.shape[1] // 2
    hk = st_ref.shape[2] // 2
    mm = BF16 if sub % BF16_ROWS == 0 else F32
    tri = (lax.broadcasted_iota(jnp.int32, (c, c), 0) >= lax.broadcasted_iota(jnp.int32, (c, c), 1))
    bcum = jnp.dot(tri.astype(F32), g, precision=lax.Precision.HIGHEST, preferred_element_type=F32)
    last = bcum[c - 1:c, :]
    q_in = (qq * jnp.exp(bcum)).astype(mm)
    k_dec = (kk * jnp.exp(last - bcum)).astype(mm)
    v_mm = v.astype(mm)
    lane_v = lax.broadcasted_iota(jnp.int32, (1, 2 * hv), 1) // hv
    bd = (lax.broadcasted_iota(jnp.int32, (2 * hv, 2 * hk), 0) // hv
          == lax.broadcasted_iota(jnp.int32, (2 * hv, 2 * hk), 1) // hk)
    for p in range(n_pairs):
        pk = slice(p * 2 * hk, (p + 1) * 2 * hk)
        pv = slice(p * 2 * hv, (p + 1) * 2 * hv)
        vp = v_mm[:, pv]
        st = st_ref[p]
        o_ref[rows, pv] = lax.dot_general(q_in[:, pk], st.astype(mm), NT, preferred_element_type=F32)
        for e in range(2):
            hl = slice(p * 2 * hk + e * hk, p * 2 * hk + (e + 1) * hk)
            v_head = jnp.where(lane_v == e, vp, jnp.zeros_like(vp))
            for j in range(c // sub):
                r0 = j * sub
                mid = r0 + sub // 2
                ref = bcum[mid:mid + 1, hl]
                qj = (qq[r0:, hl] * jnp.exp(bcum[r0:, hl] - ref)).astype(mm)
                kj = (kk[r0:r0 + sub, hl] * jnp.exp(ref - bcum[r0:r0 + sub, hl])).astype(mm)
                att = lax.dot_general(qj, kj, NT, preferred_element_type=F32)
                keep = (lax.broadcasted_iota(jnp.int32, att.shape, 0)
                        >= lax.broadcasted_iota(jnp.int32, att.shape, 1))
                att = jnp.where(keep, att, 0.0).astype(mm)
                tail = pl.ds(pl.multiple_of(rows.start + r0, sub), c - r0)
                o_ref[tail, pv] += _dot(att, v_head[r0:r0 + sub])
        upd = lax.dot_general(vp, k_dec[:, pk], TN, preferred_element_type=F32)
        st_ref[p] = st * jnp.exp(last[:, pk]) + jnp.where(bd, upd, 0.0)


def _hgrn_finish(o, zb, gho_ref, avg_ref, out_ref):
    ms = _dot((o * o).astype(BF16), avg_ref[...])
    out_ref[...] = (o * lax.rsqrt(ms + EPS) * gho_ref[...] * _silu(zb)).astype(out_ref.dtype)


def _compact_state(st):
    hv = st.shape[0] // 2
    hk = st.shape[1] // 2
    return jnp.concatenate([st[:hv, :hk], st[hv:, hk:]], axis=0)


def _expand_state(sc):
    hv = sc.shape[0] // 2
    z = jnp.zeros((hv, sc.shape[1]), F32)
    return jnp.concatenate([jnp.concatenate([sc[:hv], z], axis=1),
                            jnp.concatenate([z, sc[hv:]], axis=1)], axis=0)


def _hgrn_prompt_kernel(h_ref, wb_ref, lb_ref, gho_ref, avg_ref, out_ref, sfin_ref,
                        st_scr, g_scr, k_scr, q_scr, v_scr, o_scr, *, chunk, sub, fdim, vdim):
    tb = pl.program_id(1)

    @pl.when(tb == 0)
    def _init():
        st_scr[...] = jnp.zeros(st_scr.shape, F32)

    zb = _hgrn_project(h_ref, wb_ref, lb_ref, g_scr, k_scr, q_scr, v_scr, fdim, vdim)

    def body(ci, carry):
        rows = pl.ds(pl.multiple_of(ci * chunk, chunk), chunk)
        _hgrn_chunk(q_scr[rows, :], k_scr[rows, :], g_scr[rows, :], v_scr[rows, :], st_scr, o_scr, rows,
                    sub=sub)
        return carry

    lax.fori_loop(0, h_ref.shape[0] // chunk, body, 0)
    _hgrn_finish(o_scr[...], zb, gho_ref, avg_ref, out_ref)

    @pl.when(tb == pl.num_programs(1) - 1)
    def _final():
        for p in range(st_scr.shape[0]):
            sfin_ref[0, p] = _compact_state(st_scr[p])


def _hgrn_sample_kernel(h_ref, wb_ref, lb_ref, gho_ref, avg_ref, s0_ref, out_ref, sfin_ref,
                        st_scr, g_scr, k_scr, q_scr, v_scr, o_scr, *, t, fdim, vdim):
    zb = _hgrn_project(h_ref, wb_ref, lb_ref, g_scr, k_scr, q_scr, v_scr, fdim, vdim)

    def body(ri, carry):
        for p in range(st_scr.shape[0]):
            st_scr[p] = _expand_state(s0_ref[ri, p])
        rows = pl.ds(pl.multiple_of(ri * t, t), t)
        _hgrn_chunk(q_scr[rows, :], k_scr[rows, :], g_scr[rows, :], v_scr[rows, :], st_scr, o_scr, rows,
                    sub=t)
        for p in range(st_scr.shape[0]):
            sfin_ref[ri, p] = _compact_state(st_scr[p])
        return carry

    lax.fori_loop(0, s0_ref.shape[0], body, 0)
    _hgrn_finish(o_scr[...], zb, gho_ref, avg_ref, out_ref)


def _xattn_heads(q, zc, mk, mv, heads, hdim):
    outs = []
    for h in range(heads):
        sl = slice(h * hdim, (h + 1) * hdim)
        s = lax.dot_general(q[:, sl], mk[:, sl], NT, preferred_element_type=F32)
        s = s - jnp.max(s, axis=1, keepdims=True)
        p = jnp.exp(s)
        p = (p / jnp.sum(p, axis=1, keepdims=True)).astype(mv.dtype)
        outs.append(_dot(p, mv[:, sl]))
    return jnp.concatenate(outs, axis=1) * _silu(zc)


def _xattn_prompt_kernel(h_ref, wc_ref, mk_ref, mv_ref, out_ref, *, heads, hdim, scale):
    z = _dot(h_ref[0], wc_ref[...])
    w = heads * hdim
    q = (z[:, :w] * scale).astype(BF16)
    out_ref[0] = _xattn_heads(q, z[:, w:], mk_ref[0].astype(BF16), mv_ref[0].astype(BF16),
                              heads, hdim).astype(out_ref.dtype)


def _xattn_sample_kernel(h_ref, wc_ref, mk_ref, mv_ref, out_ref, *, heads, hdim, scale, t):
    z = _dot(h_ref[...], wc_ref[...])
    w = heads * hdim
    q = z[:, :w] * scale
    zc = z[:, w:]
    for ri in range(mk_ref.shape[0]):
        rows = slice(ri * t, (ri + 1) * t)
        out_ref[rows, :] = _xattn_heads(q[rows], zc[rows], mk_ref[ri], mv_ref[ri],
                                        heads, hdim).astype(out_ref.dtype)


def _memkv_kernel(m_ref, g_ref, w_ref, k_ref, v_ref):
    kv = _dot(_rms(m_ref[...], g_ref[...]).astype(BF16), w_ref[...])
    w = kv.shape[1] // 2
    k_ref[...] = kv[:, :w]
    v_ref[...] = kv[:, w:]


def _merge_kernel(x_ref, h_ref, oa_ref, ob_ref, oc_ref, wuv_ref, wm_ref, wbr_ref, wout_ref, gf_ref, y_ref, *,
                  width, from_latent):
    h = h_ref[...]
    d = x_ref.shape[1]
    oa = oa_ref[...]
    if from_latent:
        oa = _dot(oa.astype(BF16), wuv_ref[...])
    else:
        oa = oa.astype(F32)
    za = _dot(h, wm_ref[:, :width])
    branches = ((oa * _silu(za)).astype(BF16), ob_ref[...].astype(BF16), oc_ref[...].astype(BF16))
    merged = jnp.zeros((x_ref.shape[0], d), F32)
    for n, o in enumerate(branches):
        gate = jax.nn.sigmoid(_dot(h, wm_ref[:, width + n * d:width + (n + 1) * d]))
        merged = merged + gate * _dot(o, wbr_ref[n])
    out = x_ref[...] + _dot(merged.astype(BF16), wout_ref[...])
    y_ref[...] = _rms(out, gf_ref[...])


def _rope_tables(pos, rope, lead):
    half = rope // 2
    inv = jnp.exp(-math.log(ROPE_BASE) * jnp.arange(half, dtype=F32) / half)
    ang = pos.astype(F32)[:, None] * inv[None, :]
    n = pos.shape[0]
    pad = jnp.zeros((n, LANES - lead - rope), F32)
    cos = jnp.concatenate([jnp.ones((n, lead), F32), jnp.cos(ang), jnp.cos(ang), pad], axis=1)
    sin = jnp.concatenate([jnp.zeros((n, lead), F32), jnp.sin(ang), jnp.sin(ang), pad], axis=1)
    return cos, sin


def _swap_halves(w):
    half = w.shape[-1] // 2
    return jnp.concatenate([-w[..., half:], w[..., :half]], axis=-1)


def _pad_cols(w, width):
    return jnp.pad(w, [(0, 0)] * (w.ndim - 1) + [(0, width - w.shape[-1])])


def _block_diag(blocks):
    rows = sum(b.shape[0] for b in blocks)
    cols = sum(b.shape[1] for b in blocks)
    out = jnp.zeros((rows, cols), blocks[0].dtype)
    r = c = 0
    for b in blocks:
        out = lax.dynamic_update_slice(out, b, (r, c))
        r += b.shape[0]
        c += b.shape[1]
    return out


def _tile_rows(n, pref):
    t = min(n, pref)
    assert n % t == 0
    return t


def kernel(x_prompt, x_sample, mem_prompt, cache_kv_latent, cache_k_rope, page_table, state_hgrn, cache_mem_k, cache_mem_v, g_norm, w_in, g_q_lora, w_uq, g_kv_lora, w_uk, w_uv, lb_logits, g_hgrn_out, g_mem_norm, w_mem_kv, w_branch, w_out, g_final):
    depth = w_in.shape[0]
    assert depth == 1, "one layer per step"
    b, t, d = x_prompt.shape
    nb, ts, _ = x_sample.shape
    n_mem = mem_prompt.shape[1]
    n_phys, page, kv_lora = cache_kv_latent.shape[1:]
    rope = cache_k_rope.shape[-1]
    n_pages = page_table.shape[1]
    q_lora = g_q_lora.shape[-1]
    a_heads, a_qk = w_uq.shape[2:]
    a_nope = w_uk.shape[-1]
    a_vdim = w_uv.shape[-1]
    b_heads, b_expand, b_vdim = state_hgrn.shape[2:]
    fdim = b_heads * b_expand
    c_heads, c_hdim = cache_mem_k.shape[3:]
    width = w_branch.shape[2]
    assert a_qk == a_nope + rope and a_heads * a_vdim == width and b_heads * b_vdim == width
    assert c_heads * c_hdim == width and a_qk <= LANES and 2 * a_vdim == LANES and 2 * b_vdim == LANES
    assert b_expand == LANES and a_heads % 2 == 0 and b_heads % 2 == 0

    splits = (q_lora, kv_lora, rope, width, fdim, fdim, width, width, width, width, N_BRANCH * d)
    offs = np.concatenate([[0], np.cumsum(splits)])
    assert offs[-1] == w_in.shape[-1]
    w_in0 = w_in[0]
    col = lambda i: w_in0[:, offs[i]:offs[i + 1]]

    a_scale = a_qk ** -0.5
    n_p = b * t
    n_s = nb * ts
    xp = x_prompt.reshape(n_p, d)
    xs = x_sample.reshape(n_s, d)
    row = lambda g: g.reshape(1, -1).astype(F32)

    w_a = jnp.concatenate([col(0), col(1), _pad_cols(col(2), LANES), _pad_cols(_swap_halves(col(2)), LANES)],
                          axis=1).astype(BF16)
    uq = w_uq[0]
    uq_nope, uq_rope = uq[..., :a_nope], uq[..., a_nope:]
    zero_nope = jnp.zeros_like(uq_nope)
    wq_prompt = jnp.concatenate([
        _pad_cols(uq, LANES).reshape(q_lora, -1),
        _pad_cols(jnp.concatenate([zero_nope, _swap_halves(uq_rope)], axis=-1), LANES).reshape(q_lora, -1),
    ], axis=1).astype(BF16)
    wq_sample = jnp.concatenate([
        uq_nope.reshape(q_lora, -1),
        _pad_cols(uq_rope, LANES).reshape(q_lora, -1),
        _pad_cols(_swap_halves(uq_rope), LANES).reshape(q_lora, -1),
    ], axis=1).astype(BF16)
    uk, uv = w_uk[0], w_uv[0]
    place = jnp.zeros((LANES, a_heads, LANES), F32)
    place = place.at[jnp.arange(rope)[:, None], jnp.arange(a_heads)[None, :],
                     a_nope + jnp.arange(rope)[:, None]].set(1.0)
    wk = jnp.concatenate([_pad_cols(uk, LANES).reshape(kv_lora, -1), place.reshape(LANES, -1)], axis=0)
    uv_pad = jnp.stack([jnp.pad(uv[:, h], ((0, 0), ((h % 2) * a_vdim, LANES - a_vdim - (h % 2) * a_vdim)))
                        for h in range(a_heads)], axis=1)
    wv = jnp.concatenate([uv_pad.reshape(kv_lora, -1), jnp.zeros((LANES, a_heads * LANES), F32)], axis=0)
    w_kv = jnp.concatenate([wk, wv], axis=1).astype(BF16)
    w_abs = _block_diag([uk[:, h].T for h in range(a_heads)]).astype(BF16)
    w_uv_bd = _block_diag([uv[:, h] for h in range(a_heads)]).astype(BF16)
    w_b = jnp.concatenate([col(4), col(5), col(6), col(7)], axis=1).astype(BF16)
    w_c = jnp.concatenate([col(8), col(9)], axis=1).astype(BF16)
    w_m = jnp.concatenate([col(3), col(10)], axis=1).astype(BF16)
    w_br = w_branch[0].astype(BF16)
    w_o = w_out[0].astype(BF16)
    w_mem = w_mem_kv[0].astype(BF16)
    gho = jnp.tile(g_hgrn_out[0], b_heads).reshape(1, width).astype(F32)
    head_of = jnp.arange(width) // b_vdim
    avg = ((head_of[:, None] == head_of[None, :]).astype(F32) / b_vdim).astype(BF16)
    lb2 = lb_logits[:2].astype(F32)

    n_past = n_pages * page
    cq_p, sq_p = _rope_tables(jnp.arange(t), rope, a_nope)
    ck_p, sk_p = _rope_tables(jnp.arange(t), rope, 0)
    tm_s = _tile_rows(n_s, 256)
    assert tm_s % ts == 0
    ck_s, sk_s = _rope_tables(n_past + (jnp.arange(tm_s) % ts), rope, 0)

    tm = _tile_rows(t, 512)
    n_t = t // tm
    mla_common = dict(q_lora=q_lora, kv_lora=kv_lora, rope=rope)
    rowspec = lambda w: pl.BlockSpec((tm, w), lambda i: (i, 0))
    tabspec = pl.BlockSpec((tm, LANES), lambda i: (i % n_t, 0))
    hq = a_heads * LANES
    h_p, q_p, k_p, v_p, ckv_p, kpe_p = pl.pallas_call(
        functools.partial(_mla_prep_prompt_kernel, scale=a_scale, **mla_common),
        grid=(n_p // tm,),
        in_specs=[rowspec(d), _const_spec((1, d)), _const_spec(w_a.shape), _const_spec((1, q_lora)),
                  _const_spec(wq_prompt.shape), _const_spec((1, kv_lora)), _const_spec(w_kv.shape),
                  tabspec, tabspec, tabspec, tabspec],
        out_specs=[rowspec(d), rowspec(hq), rowspec(hq), rowspec(hq), rowspec(kv_lora), rowspec(rope)],
        out_shape=[jax.ShapeDtypeStruct((n_p, d), BF16), jax.ShapeDtypeStruct((n_p, hq), BF16),
                   jax.ShapeDtypeStruct((n_p, hq), BF16), jax.ShapeDtypeStruct((n_p, hq), BF16),
                   jax.ShapeDtypeStruct((n_p, kv_lora), F32), jax.ShapeDtypeStruct((n_p, rope), F32)],
        compiler_params=_params(("parallel",)),
    )(xp, row(g_norm), w_a, row(g_q_lora), wq_prompt, row(g_kv_lora), w_kv, cq_p, sq_p, ck_p, sk_p)

    srow = lambda w: pl.BlockSpec((tm_s, w), lambda i: (i, 0))
    hl = a_heads * kv_lora
    h_s, qlat_s, qrope_s, ckv_s, kpe_s = pl.pallas_call(
        functools.partial(_mla_prep_sample_kernel, scale=a_scale, nope_w=a_heads * a_nope, **mla_common),
        grid=(n_s // tm_s,),
        in_specs=[srow(d), _const_spec((1, d)), _const_spec(w_a.shape), _const_spec((1, q_lora)),
                  _const_spec(wq_sample.shape), _const_spec((1, kv_lora)), _const_spec(w_abs.shape),
                  _const_spec((tm_s, LANES)), _const_spec((tm_s, LANES))],
        out_specs=[srow(d), srow(hl), srow(hq), srow(kv_lora), srow(rope)],
        out_shape=[jax.ShapeDtypeStruct((n_s, d), BF16), jax.ShapeDtypeStruct((n_s, hl), F32),
                   jax.ShapeDtypeStruct((n_s, hq), F32), jax.ShapeDtypeStruct((n_s, kv_lora), F32),
                   jax.ShapeDtypeStruct((n_s, rope), F32)],
        compiler_params=_params(("parallel",)),
    )(xs, row(g_norm), w_a, row(g_q_lora), wq_sample, row(g_kv_lora), w_abs, ck_s, sk_s)

    tq = _tile_rows(t, 512)
    nq = t // tq
    pairs = [(i, j) for i in range(nq) for j in range(i + 1)]
    i_tab = jnp.asarray([p[0] for p in pairs], jnp.int32)
    j_tab = jnp.asarray([p[1] for p in pairs], jnp.int32)
    qkv = lambda a: a.reshape(b, t, hq)
    oa_p = pl.pallas_call(
        functools.partial(_flash_kernel, heads=a_heads),
        grid_spec=pltpu.PrefetchScalarGridSpec(
            num_scalar_prefetch=2, grid=(b, len(pairs)),
            in_specs=[pl.BlockSpec((1, tq, hq), lambda bi, s, it, jt: (bi, it[s], 0)),
                      pl.BlockSpec((1, tq, hq), lambda bi, s, it, jt: (bi, jt[s], 0)),
                      pl.BlockSpec((1, tq, hq), lambda bi, s, it, jt: (bi, jt[s], 0))],
            out_specs=pl.BlockSpec((1, tq, width), lambda bi, s, it, jt: (bi, it[s], 0)),
            scratch_shapes=[pltpu.VMEM((a_heads, tq, 1), F32), pltpu.VMEM((a_heads, tq, 1), F32),
                            pltpu.VMEM((a_heads, tq, LANES), F32)]),
        out_shape=jax.ShapeDtypeStruct((b, t, width), BF16),
        compiler_params=_params(("parallel", "arbitrary")),
    )(i_tab, j_tab, qkv(q_p), qkv(k_p), qkv(v_p)).reshape(n_p, width)

    group = math.gcd(n_pages, 8)
    n_chunks = n_pages // group
    req = lambda w: pl.BlockSpec((ts, w), lambda r, pt: (r, 0))
    olat_s = pl.pallas_call(
        functools.partial(_paged_kernel, heads=a_heads, group=group, page=page, n_chunks=n_chunks,
                          kv_lora=kv_lora, rope=rope),
        grid_spec=pltpu.PrefetchScalarGridSpec(
            num_scalar_prefetch=1, grid=(nb,),
            in_specs=[req(hl), req(hq), req(kv_lora), req(rope),
                      pl.BlockSpec(memory_space=pl.ANY), pl.BlockSpec(memory_space=pl.ANY)],
            out_specs=req(hl),
            scratch_shapes=[pltpu.VMEM((2, group * page, kv_lora), F32),
                            pltpu.VMEM((2, group * page, rope), F32),
                            pltpu.SemaphoreType.DMA((2, 2))]),
        out_shape=jax.ShapeDtypeStruct((n_s, hl), F32),
        compiler_params=_params(("arbitrary",)),
    )(page_table, qlat_s, qrope_s, ckv_s, kpe_s, cache_kv_latent[0], cache_k_rope[0])

    n_pairs = b_heads // 2
    tb = _tile_rows(t, 256)
    chunk = _tile_rows(tb, 64)
    sub = _tile_rows(chunk, 16)
    wb_cols = w_b.shape[1]
    hgrn_scratch = lambda rows: [pltpu.VMEM((n_pairs, LANES, 2 * b_expand), F32),
                                 pltpu.VMEM((rows, fdim), F32), pltpu.VMEM((rows, fdim), F32),
                                 pltpu.VMEM((rows, fdim), F32), pltpu.VMEM((rows, width), F32),
                                 pltpu.VMEM((rows, width), F32)]
    ob_p, st_p = pl.pallas_call(
        functools.partial(_hgrn_prompt_kernel, chunk=chunk, sub=sub, fdim=fdim, vdim=width),
        grid=(b, t // tb),
        in_specs=[pl.BlockSpec((tb, d), lambda bi, ti: (bi * (t // tb) + ti, 0)),
                  _const_spec((d, wb_cols)), _const_spec((2, fdim)), _const_spec((1, width)),
                  _const_spec((width, width))],
        out_specs=[pl.BlockSpec((tb, width), lambda bi, ti: (bi * (t // tb) + ti, 0)),
                   pl.BlockSpec((1, n_pairs, LANES, b_expand), lambda bi, ti: (bi, 0, 0, 0))],
        out_shape=[jax.ShapeDtypeStruct((n_p, width), BF16),
                   jax.ShapeDtypeStruct((b, n_pairs, LANES, b_expand), F32)],
        scratch_shapes=hgrn_scratch(tb),
        compiler_params=_params(("parallel", "arbitrary")),
    )(h_p, w_b, lb2, gho, avg)

    def to_pairs(s):
        n = s.shape[0]
        return s.reshape(n, n_pairs, 2, b_expand, b_vdim).transpose(0, 1, 2, 4, 3).reshape(
            n, n_pairs, LANES, b_expand)

    def from_pairs(s):
        n = s.shape[0]
        return s.reshape(n, n_pairs, 2, b_vdim, b_expand).transpose(0, 1, 2, 4, 3).reshape(
            n, b_heads, b_expand, b_vdim)

    rq = _tile_rows(nb, 8)
    ob_s, st_s = pl.pallas_call(
        functools.partial(_hgrn_sample_kernel, t=ts, fdim=fdim, vdim=width),
        grid=(nb // rq,),
        in_specs=[pl.BlockSpec((rq * ts, d), lambda i: (i, 0)),
                  _const_spec((d, wb_cols)), _const_spec((2, fdim)), _const_spec((1, width)),
                  _const_spec((width, width)),
                  pl.BlockSpec((rq, n_pairs, LANES, b_expand), lambda i: (i, 0, 0, 0))],
        out_specs=[pl.BlockSpec((rq * ts, width), lambda i: (i, 0)),
                   pl.BlockSpec((rq, n_pairs, LANES, b_expand), lambda i: (i, 0, 0, 0))],
        out_shape=[jax.ShapeDtypeStruct((n_s, width), BF16),
                   jax.ShapeDtypeStruct((nb, n_pairs, LANES, b_expand), F32)],
        scratch_shapes=hgrn_scratch(rq * ts),
        compiler_params=_params(("parallel",)),
    )(h_s, w_b, lb2, gho, avg, to_pairs(state_hgrn[0]))

    n_m = b * n_mem
    tmm = _tile_rows(n_m, 512)
    mk_p, mv_p = pl.pallas_call(
        _memkv_kernel,
        grid=(n_m // tmm,),
        in_specs=[pl.BlockSpec((tmm, d), lambda i: (i, 0)), _const_spec((1, d)), _const_spec(w_mem.shape)],
        out_specs=[pl.BlockSpec((tmm, width), lambda i: (i, 0)), pl.BlockSpec((tmm, width), lambda i: (i, 0))],
        out_shape=[jax.ShapeDtypeStruct((n_m, width), F32), jax.ShapeDtypeStruct((n_m, width), F32)],
        compiler_params=_params(("parallel",)),
    )(mem_prompt.reshape(n_m, d), row(g_mem_norm), w_mem)

    c_scale = c_hdim ** -0.5
    tx = _tile_rows(t, 512)
    memspec = pl.BlockSpec((1, n_mem, width), lambda bi, ti: (bi, 0, 0))
    oc_p = pl.pallas_call(
        functools.partial(_xattn_prompt_kernel, heads=c_heads, hdim=c_hdim, scale=c_scale),
        grid=(b, t // tx),
        in_specs=[pl.BlockSpec((1, tx, d), lambda bi, ti: (bi, ti, 0)), _const_spec(w_c.shape), memspec, memspec],
        out_specs=pl.BlockSpec((1, tx, width), lambda bi, ti: (bi, ti, 0)),
        out_shape=jax.ShapeDtypeStruct((b, t, width), BF16),
        compiler_params=_params(("parallel", "parallel")),
    )(h_p.reshape(b, t, d), w_c, mk_p.reshape(b, n_mem, width), mv_p.reshape(b, n_mem, width)).reshape(n_p, width)

    rx = _tile_rows(nb, 8)
    smem = pl.BlockSpec((rx, n_mem, width), lambda i: (i, 0, 0))
    oc_s = pl.pallas_call(
        functools.partial(_xattn_sample_kernel, heads=c_heads, hdim=c_hdim, scale=c_scale, t=ts),
        grid=(nb // rx,),
        in_specs=[pl.BlockSpec((rx * ts, d), lambda i: (i, 0)), _const_spec(w_c.shape), smem, smem],
        out_specs=pl.BlockSpec((rx * ts, width), lambda i: (i, 0)),
        out_shape=jax.ShapeDtypeStruct((n_s, width), F32),
        compiler_params=_params(("parallel",)),
    )(h_s, w_c, cache_mem_k[0].reshape(nb, n_mem, width), cache_mem_v[0].reshape(nb, n_mem, width))

    def merge(x2, h2, oa, ob, oc, from_latent):
        n = x2.shape[0]
        tmg = _tile_rows(n, 512)
        rs = lambda w: pl.BlockSpec((tmg, w), lambda i: (i, 0))
        return pl.pallas_call(
            functools.partial(_merge_kernel, width=width, from_latent=from_latent),
            grid=(n // tmg,),
            in_specs=[rs(d), rs(d), rs(oa.shape[1]), rs(width), rs(width), _const_spec(w_uv_bd.shape),
                      _const_spec(w_m.shape), _const_spec(w_br.shape), _const_spec(w_o.shape),
                      _const_spec((1, d))],
            out_specs=rs(d),
            out_shape=jax.ShapeDtypeStruct((n, d), F32),
            compiler_params=_params(("parallel",)),
        )(x2, h2, oa, ob, oc, w_uv_bd, w_m, w_br, w_o, row(g_final))

    y_p = merge(xp, h_p, oa_p, ob_p, oc_p, False).reshape(b, t, d)
    y_s = merge(xs, h_s, olat_s, ob_s, oc_s, True).reshape(nb, ts, d)

    return (y_p, y_s,
            ckv_p.reshape(1, b, t, kv_lora), kpe_p.reshape(1, b, t, rope),
            from_pairs(st_p)[None],
            mk_p.reshape(1, b, n_mem, c_heads, c_hdim), mv_p.reshape(1, b, n_mem, c_heads, c_hdim),
            ckv_s.reshape(1, nb, ts, kv_lora), kpe_s.reshape(1, nb, ts, rope),
            from_pairs(st_s)[None])
```

```python
import functools
import math

import numpy as np
import jax
import jax.numpy as jnp
from jax import lax
from jax.experimental import pallas as pl
from jax.experimental.pallas import tpu as pltpu

F32 = jnp.float32
BF16 = jnp.bfloat16
EPS = 1e-6
ROPE_BASE = 10000.0
N_BRANCH = 3
LANES = 128
BF16_ROWS = 16
VMEM_LIMIT = 56 * 1024 * 1024
NT = (((1,), (1,)), ((), ()))
TN = (((0,), (0,)), ((), ()))


def _params(sem):
    return pltpu.CompilerParams(dimension_semantics=sem, vmem_limit_bytes=VMEM_LIMIT)


def _rms(x, g):
    return x * lax.rsqrt(jnp.mean(x * x, axis=-1, keepdims=True) + EPS) * g


def _silu(x):
    return x * jax.nn.sigmoid(x)


def _dot(a, b):
    return jnp.dot(a, b, preferred_element_type=F32)


def _const_spec(shape):
    nd = len(shape)
    return pl.BlockSpec(shape, lambda *_: (0,) * nd)


def _mla_common(x_ref, gn_ref, wa_ref, gq_ref, gkv_ref, ck_ref, sk_ref, h_ref, ckv_ref, kpe_ref,
                q_lora, kv_lora, rope):
    h = _rms(x_ref[...], gn_ref[...]).astype(BF16)
    h_ref[...] = h
    z = _dot(h, wa_ref[...])
    cq_n = _rms(z[:, :q_lora], gq_ref[...]).astype(BF16)
    ckv = _rms(z[:, q_lora:q_lora + kv_lora], gkv_ref[...])
    ckv_ref[...] = ckv
    o = q_lora + kv_lora
    kpe = z[:, o:o + LANES] * ck_ref[...] + z[:, o + LANES:o + 2 * LANES] * sk_ref[...]
    kpe_ref[...] = kpe[:, :rope]
    return cq_n, ckv, kpe


def _mla_prep_prompt_kernel(x_ref, gn_ref, wa_ref, gq_ref, wq_ref, gkv_ref, wkv_ref, cq_ref, sq_ref,
                            ck_ref, sk_ref, h_ref, q_ref, k_ref, v_ref, ckv_ref, kpe_ref, *,
                            scale, q_lora, kv_lora, rope):
    cq_n, ckv, kpe = _mla_common(x_ref, gn_ref, wa_ref, gq_ref, gkv_ref, ck_ref, sk_ref, h_ref,
                                 ckv_ref, kpe_ref, q_lora, kv_lora, rope)
    q2 = _dot(cq_n, wq_ref[...])
    w = q2.shape[1] // 2
    reps = w // LANES
    cos = jnp.tile(cq_ref[...], (1, reps))
    sin = jnp.tile(sq_ref[...], (1, reps))
    q_ref[...] = ((q2[:, :w] * cos + q2[:, w:] * sin) * scale).astype(BF16)
    kv_in = jnp.concatenate([ckv.astype(BF16), kpe.astype(BF16)], axis=1)
    kv = _dot(kv_in, wkv_ref[...])
    wk = kv.shape[1] // 2
    k_ref[...] = kv[:, :wk].astype(BF16)
    v_ref[...] = kv[:, wk:].astype(BF16)


def _mla_prep_sample_kernel(x_ref, gn_ref, wa_ref, gq_ref, wq_ref, gkv_ref, wabs_ref, ck_ref, sk_ref,
                            h_ref, qlat_ref, qrope_ref, ckv_ref, kpe_ref, *,
                            scale, q_lora, kv_lora, rope, nope_w):
    cq_n, _, _ = _mla_common(x_ref, gn_ref, wa_ref, gq_ref, gkv_ref, ck_ref, sk_ref, h_ref,
                             ckv_ref, kpe_ref, q_lora, kv_lora, rope)
    q2 = _dot(cq_n, wq_ref[...])
    q_nope = (q2[:, :nope_w] * scale).astype(BF16)
    qlat_ref[...] = _dot(q_nope, wabs_ref[...])
    w = (q2.shape[1] - nope_w) // 2
    reps = w // LANES
    cos = jnp.tile(ck_ref[...], (1, reps))
    sin = jnp.tile(sk_ref[...], (1, reps))
    qrope_ref[...] = (q2[:, nope_w:nope_w + w] * cos + q2[:, nope_w + w:] * sin) * scale


def _flash_kernel(it_ref, jt_ref, q_ref, k_ref, v_ref, o_ref, m_scr, l_scr, acc_scr, *, heads):
    step = pl.program_id(1)
    i = it_ref[step]
    j = jt_ref[step]
    tq = q_ref.shape[1]

    @pl.when(j == 0)
    def _init():
        m_scr[...] = jnp.full(m_scr.shape, -jnp.inf, F32)
        l_scr[...] = jnp.zeros(l_scr.shape, F32)
        acc_scr[...] = jnp.zeros(acc_scr.shape, F32)

    def update(masked):
        if masked:
            row = lax.broadcasted_iota(jnp.int32, (tq, tq), 0)
            col = lax.broadcasted_iota(jnp.int32, (tq, tq), 1)
            keep = row >= col
        for h in range(heads):
            sl = slice(h * LANES, (h + 1) * LANES)
            s = lax.dot_general(q_ref[0, :, sl], k_ref[0, :, sl], NT, preferred_element_type=F32)
            if masked:
                s = jnp.where(keep, s, -jnp.inf)
            m_prev = m_scr[h]
            m_new = jnp.maximum(m_prev, jnp.max(s, axis=1, keepdims=True))
            alpha = jnp.exp(m_prev - m_new)
            p = jnp.exp(s - jnp.tile(m_new, (1, tq // LANES)))
            lane_sums = p[:, :LANES]
            for c in range(1, tq // LANES):
                lane_sums = lane_sums + p[:, c * LANES:(c + 1) * LANES]
            l_scr[h] = alpha * l_scr[h] + lane_sums
            acc_scr[h] = alpha * acc_scr[h] + _dot(p.astype(BF16), v_ref[0, :, sl])
            m_scr[h] = m_new

    @pl.when(j < i)
    def _off_diagonal():
        update(False)

    @pl.when(j == i)
    def _diagonal():
        update(True)
        for p in range(heads // 2):
            even = acc_scr[2 * p] / jnp.sum(l_scr[2 * p], axis=1, keepdims=True)
            odd = acc_scr[2 * p + 1] / jnp.sum(l_scr[2 * p + 1], axis=1, keepdims=True)
            o_ref[0, :, p * LANES:(p + 1) * LANES] = (even + odd).astype(o_ref.dtype)


def _paged_kernel(pt_ref, qlat_ref, qrope_ref, ckv_ref, kpe_ref, lat_hbm, rope_hbm, o_ref,
                  lat_buf, rope_buf, sem, *, heads, group, page, n_chunks, block_keys, kv_lora, rope):
    r = pl.program_id(0)
    n_req = pl.num_programs(0)
    t = qlat_ref.shape[0]

    def page_copies(req, chunk, slot, g):
        pg = pt_ref[req, chunk * group + g]
        rows = pl.ds(g * page, page)
        return (pltpu.make_async_copy(lat_hbm.at[pg], lat_buf.at[slot, rows], sem.at[0, slot]),
                pltpu.make_async_copy(rope_hbm.at[pg], rope_buf.at[slot, rows], sem.at[1, slot]))

    def start_chunk(req, chunk, slot):
        for g in range(group):
            for cp in page_copies(req, chunk, slot, g):
                cp.start()

    def wait_chunk(req, chunk, slot):
        for g in range(group):
            for cp in page_copies(req, chunk, slot, g):
                cp.wait()

    @pl.when(r == 0)
    def _prime():
        start_chunk(0, 0, 0)

    q_lat = jnp.concatenate([qlat_ref[:, h * kv_lora:(h + 1) * kv_lora] for h in range(heads)], axis=0)
    q_rope = jnp.concatenate([qrope_ref[:, h * LANES:(h + 1) * LANES] for h in range(heads)],
                             axis=0)[:, :rope]

    def scores(lat, rp):
        return (lax.dot_general(q_lat.astype(lat.dtype), lat, NT, preferred_element_type=F32)
                + lax.dot_general(q_rope.astype(rp.dtype), rp, NT, preferred_element_type=F32))

    def partial_softmax(s, lat):
        m = jnp.max(s, axis=1, keepdims=True)
        p = jnp.exp(s - m)
        return m, jnp.sum(p, axis=1, keepdims=True), _dot(p.astype(lat.dtype), lat)

    def combine(carry, parts):
        m_prev, l_prev, acc = carry
        m_new = m_prev
        for m, _, _ in parts:
            m_new = jnp.maximum(m_new, m)
        alpha = jnp.exp(m_prev - m_new)
        l_new, acc = alpha * l_prev, alpha * acc
        for m, l, a in parts:
            w = jnp.exp(m - m_new)
            l_new, acc = l_new + w * l, acc + w * a
        return m_new, l_new, acc

    def body(c, carry):
        slot = (r * n_chunks + c) % 2
        wrap = jnp.where(c + 1 == n_chunks, 1, 0)
        start_chunk(jnp.minimum(r + wrap, n_req - 1), (c + 1) * (1 - wrap), 1 - slot)
        wait_chunk(r, c, slot)
        parts = []
        for blk in range(group * page // block_keys):
            keys = pl.ds(blk * block_keys, block_keys)
            lat = lat_buf[slot, keys].astype(BF16)
            parts.append(partial_softmax(scores(lat, rope_buf[slot, keys].astype(BF16)), lat))
        return combine(carry, parts)

    rows = heads * t
    init = (jnp.full((rows, 1), -jnp.inf, F32), jnp.zeros((rows, 1), F32), jnp.zeros((rows, kv_lora), F32))
    carry = lax.fori_loop(0, n_chunks, body, init)

    @pl.when(r == n_req - 1)
    def _drain():
        wait_chunk(r, 0, ((r + 1) * n_chunks) % 2)

    lat_new = ckv_ref[...]
    s_new = scores(lat_new, kpe_ref[...])
    tok = lax.broadcasted_iota(jnp.int32, (rows, t), 0) % t
    key = lax.broadcasted_iota(jnp.int32, (rows, t), 1)
    s_new = jnp.where(key <= tok, s_new, -jnp.inf)
    _, l_fin, acc = combine(carry, [partial_softmax(s_new, lat_new)])
    o = acc / l_fin
    for h in range(heads):
        o_ref[:, h * kv_lora:(h + 1) * kv_lora] = o[h * t:(h + 1) * t].astype(o_ref.dtype)


def _hgrn_project(h_ref, wb_ref, lb_ref, g_scr, k_scr, q_scr, v_scr, fdim, vdim):
    z = _dot(h_ref[...], wb_ref[...])
    l0 = lb_ref[0:1, :]
    l1 = lb_ref[1:2, :]
    mx = jnp.maximum(l0, l1)
    e0 = jnp.exp(l0 - mx)
    e1 = jnp.exp(l1 - mx)
    lb = e1 / (e0 + e1)
    f = lb + (1.0 - lb) * jax.nn.sigmoid(z[:, :fdim])
    g_scr[...] = jnp.log(f)
    k_scr[...] = 1.0 - f
    q_scr[...] = _silu(z[:, fdim:2 * fdim])
    v_scr[...] = z[:, 2 * fdim:2 * fdim + vdim]
    return z[:, 2 * fdim + vdim:]


def _cumsum_rows(x):
    row = lax.broadcasted_iota(jnp.int32, x.shape, 0)
    shift = 1
    while shift < x.shape[0]:
        x = x + jnp.where(row >= shift, pltpu.roll(x, shift, 0), 0.0)
        shift *= 2
    return x


def _hgrn_chunk(qq, kk, g, v, st_ref, o_ref, rows, *, sub):
    c = qq.shape[0]
    n_pairs = st_ref.shape[0]
    hv = st_ref.shape[1] // 2
    hk = st_ref.shape[2] // 2
    mm = BF16 if sub % BF16_ROWS == 0 else F32
    bcum = _cumsum_rows(g)
    last = bcum[c - 1:c, :]
    q_in = (qq * jnp.exp(bcum)).astype(mm)
    k_dec = (kk * jnp.exp(last - bcum)).astype(mm)
    v_mm = v.astype(mm)
    n_sub = c // sub
    row_blk = lax.broadcasted_iota(jnp.int32, (c, hk), 0) // sub
    causal = lax.broadcasted_iota(jnp.int32, (c, c), 0) >= lax.broadcasted_iota(jnp.int32, (c, c), 1)
    lane_v = lax.broadcasted_iota(jnp.int32, (1, 2 * hv), 1) // hv
    bd = (lax.broadcasted_iota(jnp.int32, (2 * hv, 2 * hk), 0) // hv
          == lax.broadcasted_iota(jnp.int32, (2 * hv, 2 * hk), 1) // hk)
    for p in range(n_pairs):
        pk = slice(p * 2 * hk, (p + 1) * 2 * hk)
        pv = slice(p * 2 * hv, (p + 1) * 2 * hv)
        vp = v_mm[:, pv]
        st = st_ref[p]
        o = lax.dot_general(q_in[:, pk], st.astype(mm), NT, preferred_element_type=F32)
        for e in range(2):
            hl = slice(p * 2 * hk + e * hk, p * 2 * hk + (e + 1) * hk)
            v_head = jnp.where(lane_v == e, vp, jnp.zeros_like(vp))
            b_h, q_h = bcum[:, hl], qq[:, hl]
            refs = [b_h[j * sub + sub // 2:j * sub + sub // 2 + 1] for j in range(n_sub)]
            ref_rows = jnp.concatenate([jnp.broadcast_to(rj, (sub, hk)) for rj in refs], axis=0)
            k_all = kk[:, hl] * jnp.exp(ref_rows - b_h)
            k_cat = jnp.concatenate([jnp.where(row_blk == j, k_all, 0.0).astype(mm) for j in range(n_sub)],
                                    axis=1)
            q_cat = jnp.concatenate(
                [jnp.concatenate([jnp.zeros((j * sub, hk), F32)] * (j > 0)
                                 + [q_h[j * sub:] * jnp.exp(b_h[j * sub:] - refs[j])], axis=0).astype(mm)
                 for j in range(n_sub)], axis=1)
            att = lax.dot_general(q_cat, k_cat, NT, preferred_element_type=F32)
            o = o + _dot(jnp.where(causal, att, 0.0).astype(mm), v_head)
        o_ref[rows, pv] = o
        upd = lax.dot_general(vp, k_dec[:, pk], TN, preferred_element_type=F32)
        st_ref[p] = st * jnp.exp(last[:, pk]) + jnp.where(bd, upd, 0.0)


def _hgrn_finish(o, zb, gho_ref, avg_ref, out_ref):
    ms = _dot((o * o).astype(BF16), avg_ref[...])
    out_ref[...] = (o * lax.rsqrt(ms + EPS) * gho_ref[...] * _silu(zb)).astype(out_ref.dtype)


def _compact_state(st):
    hv = st.shape[0] // 2
    hk = st.shape[1] // 2
    return jnp.concatenate([st[:hv, :hk], st[hv:, hk:]], axis=0)


def _expand_state(sc):
    hv = sc.shape[0] // 2
    z = jnp.zeros((hv, sc.shape[1]), F32)
    return jnp.concatenate([jnp.concatenate([sc[:hv], z], axis=1),
                            jnp.concatenate([z, sc[hv:]], axis=1)], axis=0)


def _hgrn_prompt_kernel(h_ref, wb_ref, lb_ref, gho_ref, avg_ref, out_ref, sfin_ref,
                        st_scr, g_scr, k_scr, q_scr, v_scr, o_scr, *, chunk, sub, fdim, vdim):
    tb = pl.program_id(1)

    @pl.when(tb == 0)
    def _init():
        st_scr[...] = jnp.zeros(st_scr.shape, F32)

    zb = _hgrn_project(h_ref, wb_ref, lb_ref, g_scr, k_scr, q_scr, v_scr, fdim, vdim)

    def body(ci, carry):
        rows = pl.ds(pl.multiple_of(ci * chunk, chunk), chunk)
        _hgrn_chunk(q_scr[rows, :], k_scr[rows, :], g_scr[rows, :], v_scr[rows, :], st_scr, o_scr, rows,
                    sub=sub)
        return carry

    lax.fori_loop(0, h_ref.shape[0] // chunk, body, 0)
    _hgrn_finish(o_scr[...], zb, gho_ref, avg_ref, out_ref)

    @pl.when(tb == pl.num_programs(1) - 1)
    def _final():
        for p in range(st_scr.shape[0]):
            sfin_ref[0, p] = _compact_state(st_scr[p])


def _hgrn_sample_kernel(h_ref, wb_ref, lb_ref, gho_ref, avg_ref, s0_ref, out_ref, sfin_ref,
                        st_scr, g_scr, k_scr, q_scr, v_scr, o_scr, *, t, fdim, vdim):
    zb = _hgrn_project(h_ref, wb_ref, lb_ref, g_scr, k_scr, q_scr, v_scr, fdim, vdim)

    def body(ri, carry):
        for p in range(st_scr.shape[0]):
            st_scr[p] = _expand_state(s0_ref[ri, p])
        rows = pl.ds(pl.multiple_of(ri * t, t), t)
        _hgrn_chunk(q_scr[rows, :], k_scr[rows, :], g_scr[rows, :], v_scr[rows, :], st_scr, o_scr, rows,
                    sub=t)
        for p in range(st_scr.shape[0]):
            sfin_ref[ri, p] = _compact_state(st_scr[p])
        return carry

    lax.fori_loop(0, s0_ref.shape[0], body, 0)
    _hgrn_finish(o_scr[...], zb, gho_ref, avg_ref, out_ref)


def _xattn_heads(q, zc, mk, mv, heads, hdim):
    outs = []
    for h in range(heads):
        sl = slice(h * hdim, (h + 1) * hdim)
        s = lax.dot_general(q[:, sl], mk[:, sl], NT, preferred_element_type=F32)
        s = s - jnp.max(s, axis=1, keepdims=True)
        p = jnp.exp(s)
        p = (p / jnp.sum(p, axis=1, keepdims=True)).astype(mv.dtype)
        outs.append(_dot(p, mv[:, sl]))
    return jnp.concatenate(outs, axis=1) * _silu(zc)


def _xattn_prompt_kernel(h_ref, wc_ref, mk_ref, mv_ref, out_ref, *, heads, hdim, scale):
    z = _dot(h_ref[0], wc_ref[...])
    w = heads * hdim
    q = (z[:, :w] * scale).astype(BF16)
    out_ref[0] = _xattn_heads(q, z[:, w:], mk_ref[0].astype(BF16), mv_ref[0].astype(BF16),
                              heads, hdim).astype(out_ref.dtype)


def _xattn_sample_kernel(h_ref, wc_ref, mk_ref, mv_ref, out_ref, *, heads, hdim, scale, t):
    z = _dot(h_ref[...], wc_ref[...])
    w = heads * hdim
    q = z[:, :w] * scale
    zc = z[:, w:]
    for ri in range(mk_ref.shape[0]):
        rows = slice(ri * t, (ri + 1) * t)
        out_ref[rows, :] = _xattn_heads(q[rows], zc[rows], mk_ref[ri], mv_ref[ri],
                                        heads, hdim).astype(out_ref.dtype)


def _memkv_kernel(m_ref, g_ref, w_ref, k_ref, v_ref):
    kv = _dot(_rms(m_ref[...], g_ref[...]).astype(BF16), w_ref[...])
    w = kv.shape[1] // 2
    k_ref[...] = kv[:, :w]
    v_ref[...] = kv[:, w:]


def _merge_kernel(x_ref, h_ref, oa_ref, ob_ref, oc_ref, wuv_ref, wm_ref, wbr_ref, wout_ref, gf_ref, y_ref, *,
                  width, from_latent):
    h = h_ref[...]
    d = x_ref.shape[1]
    oa = oa_ref[...]
    if from_latent:
        oa = _dot(oa.astype(BF16), wuv_ref[...])
    else:
        oa = oa.astype(F32)
    za = _dot(h, wm_ref[:, :width])
    branches = ((oa * _silu(za)).astype(BF16), ob_ref[...].astype(BF16), oc_ref[...].astype(BF16))
    merged = jnp.zeros((x_ref.shape[0], d), F32)
    for n, o in enumerate(branches):
        gate = jax.nn.sigmoid(_dot(h, wm_ref[:, width + n * d:width + (n + 1) * d]))
        merged = merged + gate * _dot(o, wbr_ref[n])
    out = x_ref[...] + _dot(merged.astype(BF16), wout_ref[...])
    y_ref[...] = _rms(out, gf_ref[...])


def _rope_tables(pos, rope, lead):
    half = rope // 2
    inv = jnp.exp(-math.log(ROPE_BASE) * jnp.arange(half, dtype=F32) / half)
    ang = pos.astype(F32)[:, None] * inv[None, :]
    n = pos.shape[0]
    pad = jnp.zeros((n, LANES - lead - rope), F32)
    cos = jnp.concatenate([jnp.ones((n, lead), F32), jnp.cos(ang), jnp.cos(ang), pad], axis=1)
    sin = jnp.concatenate([jnp.zeros((n, lead), F32), jnp.sin(ang), jnp.sin(ang), pad], axis=1)
    return cos, sin


def _swap_halves(w):
    half = w.shape[-1] // 2
    return jnp.concatenate([-w[..., half:], w[..., :half]], axis=-1)


def _pad_cols(w, width):
    return jnp.pad(w, [(0, 0)] * (w.ndim - 1) + [(0, width - w.shape[-1])])


def _block_diag(blocks):
    rows = sum(b.shape[0] for b in blocks)
    cols = sum(b.shape[1] for b in blocks)
    out = jnp.zeros((rows, cols), blocks[0].dtype)
    r = c = 0
    for b in blocks:
        out = lax.dynamic_update_slice(out, b, (r, c))
        r += b.shape[0]
        c += b.shape[1]
    return out


def _tile_rows(n, pref):
    t = min(n, pref)
    assert n % t == 0
    return t


def kernel(x_prompt, x_sample, mem_prompt, cache_kv_latent, cache_k_rope, page_table, state_hgrn, cache_mem_k, cache_mem_v, g_norm, w_in, g_q_lora, w_uq, g_kv_lora, w_uk, w_uv, lb_logits, g_hgrn_out, g_mem_norm, w_mem_kv, w_branch, w_out, g_final):
    depth = w_in.shape[0]
    assert depth == 1, "one layer per step"
    b, t, d = x_prompt.shape
    nb, ts, _ = x_sample.shape
    n_mem = mem_prompt.shape[1]
    n_phys, page, kv_lora = cache_kv_latent.shape[1:]
    rope = cache_k_rope.shape[-1]
    n_pages = page_table.shape[1]
    q_lora = g_q_lora.shape[-1]
    a_heads, a_qk = w_uq.shape[2:]
    a_nope = w_uk.shape[-1]
    a_vdim = w_uv.shape[-1]
    b_heads, b_expand, b_vdim = state_hgrn.shape[2:]
    fdim = b_heads * b_expand
    c_heads, c_hdim = cache_mem_k.shape[3:]
    width = w_branch.shape[2]
    assert a_qk == a_nope + rope and a_heads * a_vdim == width and b_heads * b_vdim == width
    assert c_heads * c_hdim == width and a_qk <= LANES and 2 * a_vdim == LANES and 2 * b_vdim == LANES
    assert b_expand == LANES and a_heads % 2 == 0 and b_heads % 2 == 0

    splits = (q_lora, kv_lora, rope, width, fdim, fdim, width, width, width, width, N_BRANCH * d)
    offs = np.concatenate([[0], np.cumsum(splits)])
    assert offs[-1] == w_in.shape[-1]
    w_in0 = w_in[0]
    col = lambda i: w_in0[:, offs[i]:offs[i + 1]]

    a_scale = a_qk ** -0.5
    n_p = b * t
    n_s = nb * ts
    xp = x_prompt.reshape(n_p, d)
    xs = x_sample.reshape(n_s, d)
    row = lambda g: g.reshape(1, -1).astype(F32)

    w_a = jnp.concatenate([col(0), col(1), _pad_cols(col(2), LANES), _pad_cols(_swap_halves(col(2)), LANES)],
                          axis=1).astype(BF16)
    uq = w_uq[0]
    uq_nope, uq_rope = uq[..., :a_nope], uq[..., a_nope:]
    zero_nope = jnp.zeros_like(uq_nope)
    wq_prompt = jnp.concatenate([
        _pad_cols(uq, LANES).reshape(q_lora, -1),
        _pad_cols(jnp.concatenate([zero_nope, _swap_halves(uq_rope)], axis=-1), LANES).reshape(q_lora, -1),
    ], axis=1).astype(BF16)
    wq_sample = jnp.concatenate([
        uq_nope.reshape(q_lora, -1),
        _pad_cols(uq_rope, LANES).reshape(q_lora, -1),
        _pad_cols(_swap_halves(uq_rope), LANES).reshape(q_lora, -1),
    ], axis=1).astype(BF16)
    uk, uv = w_uk[0], w_uv[0]
    place = jnp.zeros((LANES, a_heads, LANES), F32)
    place = place.at[jnp.arange(rope)[:, None], jnp.arange(a_heads)[None, :],
                     a_nope + jnp.arange(rope)[:, None]].set(1.0)
    wk = jnp.concatenate([_pad_cols(uk, LANES).reshape(kv_lora, -1), place.reshape(LANES, -1)], axis=0)
    uv_pad = jnp.stack([jnp.pad(uv[:, h], ((0, 0), ((h % 2) * a_vdim, LANES - a_vdim - (h % 2) * a_vdim)))
                        for h in range(a_heads)], axis=1)
    wv = jnp.concatenate([uv_pad.reshape(kv_lora, -1), jnp.zeros((LANES, a_heads * LANES), F32)], axis=0)
    w_kv = jnp.concatenate([wk, wv], axis=1).astype(BF16)
    w_abs = _block_diag([uk[:, h].T for h in range(a_heads)]).astype(BF16)
    w_uv_bd = _block_diag([uv[:, h] for h in range(a_heads)]).astype(BF16)
    w_b = jnp.concatenate([col(4), col(5), col(6), col(7)], axis=1).astype(BF16)
    w_c = jnp.concatenate([col(8), col(9)], axis=1).astype(BF16)
    w_m = jnp.concatenate([col(3), col(10)], axis=1).astype(BF16)
    w_br = w_branch[0].astype(BF16)
    w_o = w_out[0].astype(BF16)
    w_mem = w_mem_kv[0].astype(BF16)
    gho = jnp.tile(g_hgrn_out[0], b_heads).reshape(1, width).astype(F32)
    head_of = jnp.arange(width) // b_vdim
    avg = ((head_of[:, None] == head_of[None, :]).astype(F32) / b_vdim).astype(BF16)
    lb2 = lb_logits[:2].astype(F32)

    n_past = n_pages * page
    cq_p, sq_p = _rope_tables(jnp.arange(t), rope, a_nope)
    ck_p, sk_p = _rope_tables(jnp.arange(t), rope, 0)
    tm_s = _tile_rows(n_s, 256)
    assert tm_s % ts == 0
    ck_s, sk_s = _rope_tables(n_past + (jnp.arange(tm_s) % ts), rope, 0)

    tm = _tile_rows(t, 512)
    n_t = t // tm
    mla_common = dict(q_lora=q_lora, kv_lora=kv_lora, rope=rope)
    rowspec = lambda w: pl.BlockSpec((tm, w), lambda i: (i, 0))
    tabspec = pl.BlockSpec((tm, LANES), lambda i: (i % n_t, 0))
    hq = a_heads * LANES
    h_p, q_p, k_p, v_p, ckv_p, kpe_p = pl.pallas_call(
        functools.partial(_mla_prep_prompt_kernel, scale=a_scale, **mla_common),
        grid=(n_p // tm,),
        in_specs=[rowspec(d), _const_spec((1, d)), _const_spec(w_a.shape), _const_spec((1, q_lora)),
                  _const_spec(wq_prompt.shape), _const_spec((1, kv_lora)), _const_spec(w_kv.shape),
                  tabspec, tabspec, tabspec, tabspec],
        out_specs=[rowspec(d), rowspec(hq), rowspec(hq), rowspec(hq), rowspec(kv_lora), rowspec(rope)],
        out_shape=[jax.ShapeDtypeStruct((n_p, d), BF16), jax.ShapeDtypeStruct((n_p, hq), BF16),
                   jax.ShapeDtypeStruct((n_p, hq), BF16), jax.ShapeDtypeStruct((n_p, hq), BF16),
                   jax.ShapeDtypeStruct((n_p, kv_lora), F32), jax.ShapeDtypeStruct((n_p, rope), F32)],
        compiler_params=_params(("parallel",)),
    )(xp, row(g_norm), w_a, row(g_q_lora), wq_prompt, row(g_kv_lora), w_kv, cq_p, sq_p, ck_p, sk_p)

    srow = lambda w: pl.BlockSpec((tm_s, w), lambda i: (i, 0))
    hl = a_heads * kv_lora
    h_s, qlat_s, qrope_s, ckv_s, kpe_s = pl.pallas_call(
        functools.partial(_mla_prep_sample_kernel, scale=a_scale, nope_w=a_heads * a_nope, **mla_common),
        grid=(n_s // tm_s,),
        in_specs=[srow(d), _const_spec((1, d)), _const_spec(w_a.shape), _const_spec((1, q_lora)),
                  _const_spec(wq_sample.shape), _const_spec((1, kv_lora)), _const_spec(w_abs.shape),
                  _const_spec((tm_s, LANES)), _const_spec((tm_s, LANES))],
        out_specs=[srow(d), srow(hl), srow(hq), srow(kv_lora), srow(rope)],
        out_shape=[jax.ShapeDtypeStruct((n_s, d), BF16), jax.ShapeDtypeStruct((n_s, hl), F32),
                   jax.ShapeDtypeStruct((n_s, hq), F32), jax.ShapeDtypeStruct((n_s, kv_lora), F32),
                   jax.ShapeDtypeStruct((n_s, rope), F32)],
        compiler_params=_params(("parallel",)),
    )(xs, row(g_norm), w_a, row(g_q_lora), wq_sample, row(g_kv_lora), w_abs, ck_s, sk_s)

    tq = _tile_rows(t, 512)
    nq = t // tq
    pairs = [(i, j) for i in range(nq) for j in range(i + 1)]
    i_tab = jnp.asarray([p[0] for p in pairs], jnp.int32)
    j_tab = jnp.asarray([p[1] for p in pairs], jnp.int32)
    qkv = lambda a: a.reshape(b, t, hq)
    oa_p = pl.pallas_call(
        functools.partial(_flash_kernel, heads=a_heads),
        grid_spec=pltpu.PrefetchScalarGridSpec(
            num_scalar_prefetch=2, grid=(b, len(pairs)),
            in_specs=[pl.BlockSpec((1, tq, hq), lambda bi, s, it, jt: (bi, it[s], 0)),
                      pl.BlockSpec((1, tq, hq), lambda bi, s, it, jt: (bi, jt[s], 0)),
                      pl.BlockSpec((1, tq, hq), lambda bi, s, it, jt: (bi, jt[s], 0))],
            out_specs=pl.BlockSpec((1, tq, width), lambda bi, s, it, jt: (bi, it[s], 0)),
            scratch_shapes=[pltpu.VMEM((a_heads, tq, LANES), F32)] * 3),
        out_shape=jax.ShapeDtypeStruct((b, t, width), BF16),
        compiler_params=_params(("parallel", "arbitrary")),
    )(i_tab, j_tab, qkv(q_p), qkv(k_p), qkv(v_p)).reshape(n_p, width)

    group = math.gcd(n_pages, 32)
    n_chunks = n_pages // group
    block_keys = math.gcd(group, 8) * page
    req = lambda w: pl.BlockSpec((ts, w), lambda r, pt: (r, 0))
    olat_s = pl.pallas_call(
        functools.partial(_paged_kernel, heads=a_heads, group=group, page=page, n_chunks=n_chunks,
                          block_keys=block_keys,
                          kv_lora=kv_lora, rope=rope),
        grid_spec=pltpu.PrefetchScalarGridSpec(
            num_scalar_prefetch=1, grid=(nb,),
            in_specs=[req(hl), req(hq), req(kv_lora), req(rope),
                      pl.BlockSpec(memory_space=pl.ANY), pl.BlockSpec(memory_space=pl.ANY)],
            out_specs=req(hl),
            scratch_shapes=[pltpu.VMEM((2, group * page, kv_lora), F32),
                            pltpu.VMEM((2, group * page, rope), F32),
                            pltpu.SemaphoreType.DMA((2, 2))]),
        out_shape=jax.ShapeDtypeStruct((n_s, hl), F32),
        compiler_params=_params(("arbitrary",)),
    )(page_table, qlat_s, qrope_s, ckv_s, kpe_s, cache_kv_latent.reshape(n_phys, page, kv_lora),
      cache_k_rope.reshape(n_phys, page, rope))

    n_pairs = b_heads // 2
    tb = _tile_rows(t, 256)
    chunk = _tile_rows(tb, 64)
    sub = _tile_rows(chunk, 16)
    wb_cols = w_b.shape[1]
    hgrn_scratch = lambda rows: [pltpu.VMEM((n_pairs, LANES, 2 * b_expand), F32),
                                 pltpu.VMEM((rows, fdim), F32), pltpu.VMEM((rows, fdim), F32),
                                 pltpu.VMEM((rows, fdim), F32), pltpu.VMEM((rows, width), F32),
                                 pltpu.VMEM((rows, width), F32)]
    ob_p, st_p = pl.pallas_call(
        functools.partial(_hgrn_prompt_kernel, chunk=chunk, sub=sub, fdim=fdim, vdim=width),
        grid=(b, t // tb),
        in_specs=[pl.BlockSpec((tb, d), lambda bi, ti: (bi * (t // tb) + ti, 0)),
                  _const_spec((d, wb_cols)), _const_spec((2, fdim)), _const_spec((1, width)),
                  _const_spec((width, width))],
        out_specs=[pl.BlockSpec((tb, width), lambda bi, ti: (bi * (t // tb) + ti, 0)),
                   pl.BlockSpec((1, n_pairs, LANES, b_expand), lambda bi, ti: (bi, 0, 0, 0))],
        out_shape=[jax.ShapeDtypeStruct((n_p, width), BF16),
                   jax.ShapeDtypeStruct((b, n_pairs, LANES, b_expand), F32)],
        scratch_shapes=hgrn_scratch(tb),
        compiler_params=_params(("parallel", "arbitrary")),
    )(h_p, w_b, lb2, gho, avg)

    def to_pairs(s):
        n = s.shape[0]
        return s.reshape(n, n_pairs, 2, b_expand, b_vdim).transpose(0, 1, 2, 4, 3).reshape(
            n, n_pairs, LANES, b_expand)

    def from_pairs(s):
        n = s.shape[0]
        return s.reshape(n, n_pairs, 2, b_vdim, b_expand).transpose(0, 1, 2, 4, 3).reshape(
            n, b_heads, b_expand, b_vdim)

    rq = _tile_rows(nb, 8)
    ob_s, st_s = pl.pallas_call(
        functools.partial(_hgrn_sample_kernel, t=ts, fdim=fdim, vdim=width),
        grid=(nb // rq,),
        in_specs=[pl.BlockSpec((rq * ts, d), lambda i: (i, 0)),
                  _const_spec((d, wb_cols)), _const_spec((2, fdim)), _const_spec((1, width)),
                  _const_spec((width, width)),
                  pl.BlockSpec((rq, n_pairs, LANES, b_expand), lambda i: (i, 0, 0, 0))],
        out_specs=[pl.BlockSpec((rq * ts, width), lambda i: (i, 0)),
                   pl.BlockSpec((rq, n_pairs, LANES, b_expand), lambda i: (i, 0, 0, 0))],
        out_shape=[jax.ShapeDtypeStruct((n_s, width), BF16),
                   jax.ShapeDtypeStruct((nb, n_pairs, LANES, b_expand), F32)],
        scratch_shapes=hgrn_scratch(rq * ts),
        compiler_params=_params(("parallel",)),
    )(h_s, w_b, lb2, gho, avg, to_pairs(state_hgrn.reshape(nb, b_heads, b_expand, b_vdim)))

    n_m = b * n_mem
    tmm = _tile_rows(n_m, 512)
    mk_p, mv_p = pl.pallas_call(
        _memkv_kernel,
        grid=(n_m // tmm,),
        in_specs=[pl.BlockSpec((tmm, d), lambda i: (i, 0)), _const_spec((1, d)), _const_spec(w_mem.shape)],
        out_specs=[pl.BlockSpec((tmm, width), lambda i: (i, 0)), pl.BlockSpec((tmm, width), lambda i: (i, 0))],
        out_shape=[jax.ShapeDtypeStruct((n_m, width), F32), jax.ShapeDtypeStruct((n_m, width), F32)],
        compiler_params=_params(("parallel",)),
    )(mem_prompt.reshape(n_m, d), row(g_mem_norm), w_mem)

    c_scale = c_hdim ** -0.5
    tx = _tile_rows(t, 512)
    memspec = pl.BlockSpec((1, n_mem, width), lambda bi, ti: (bi, 0, 0))
    oc_p = pl.pallas_call(
        functools.partial(_xattn_prompt_kernel, heads=c_heads, hdim=c_hdim, scale=c_scale),
        grid=(b, t // tx),
        in_specs=[pl.BlockSpec((1, tx, d), lambda bi, ti: (bi, ti, 0)), _const_spec(w_c.shape), memspec, memspec],
        out_specs=pl.BlockSpec((1, tx, width), lambda bi, ti: (bi, ti, 0)),
        out_shape=jax.ShapeDtypeStruct((b, t, width), BF16),
        compiler_params=_params(("parallel", "parallel")),
    )(h_p.reshape(b, t, d), w_c, mk_p.reshape(b, n_mem, width), mv_p.reshape(b, n_mem, width)).reshape(n_p, width)

    rx = _tile_rows(nb, 8)
    smem = pl.BlockSpec((rx, n_mem, width), lambda i: (i, 0, 0))
    oc_s = pl.pallas_call(
        functools.partial(_xattn_sample_kernel, heads=c_heads, hdim=c_hdim, scale=c_scale, t=ts),
        grid=(nb // rx,),
        in_specs=[pl.BlockSpec((rx * ts, d), lambda i: (i, 0)), _const_spec(w_c.shape), smem, smem],
        out_specs=pl.BlockSpec((rx * ts, width), lambda i: (i, 0)),
        out_shape=jax.ShapeDtypeStruct((n_s, width), F32),
        compiler_params=_params(("parallel",)),
    )(h_s, w_c, cache_mem_k.reshape(nb, n_mem, width), cache_mem_v.reshape(nb, n_mem, width))

    def merge(x2, h2, oa, ob, oc, from_latent):
        n = x2.shape[0]
        tmg = _tile_rows(n, 512)
        rs = lambda w: pl.BlockSpec((tmg, w), lambda i: (i, 0))
        return pl.pallas_call(
            functools.partial(_merge_kernel, width=width, from_latent=from_latent),
            grid=(n // tmg,),
            in_specs=[rs(d), rs(d), rs(oa.shape[1]), rs(width), rs(width), _const_spec(w_uv_bd.shape),
                      _const_spec(w_m.shape), _const_spec(w_br.shape), _const_spec(w_o.shape),
                      _const_spec((1, d))],
            out_specs=rs(d),
            out_shape=jax.ShapeDtypeStruct((n, d), F32),
            compiler_params=_params(("parallel",)),
        )(x2, h2, oa, ob, oc, w_uv_bd, w_m, w_br, w_o, row(g_final))

    y_p = merge(xp, h_p, oa_p, ob_p, oc_p, False).reshape(b, t, d)
    y_s = merge(xs, h_s, olat_s, ob_s, oc_s, True).reshape(nb, ts, d)

    return (y_p, y_s,
            ckv_p.reshape(1, b, t, kv_lora), kpe_p.reshape(1, b, t, rope),
            from_pairs(st_p)[None],
            mk_p.reshape(1, b, n_mem, c_heads, c_hdim), mv_p.reshape(1, b, n_mem, c_heads, c_hdim),
            ckv_s.reshape(1, nb, ts, kv_lora), kpe_s.reshape(1, nb, ts, rope),
            from_pairs(st_s)[None])
```

```python
import functools
import math

import numpy as np
import jax
import jax.numpy as jnp
from jax import lax
from jax.experimental import pallas as pl
from jax.experimental.pallas import tpu as pltpu

F32 = jnp.float32
BF16 = jnp.bfloat16
EPS = 1e-6
ROPE_BASE = 10000.0
N_BRANCH = 3
LANES = 128
BF16_ROWS = 16
VMEM_LIMIT = 56 * 1024 * 1024
NT = (((1,), (1,)), ((), ()))
TN = (((0,), (0,)), ((), ()))
NN = (((1,), (0,)), ((), ()))


def _params(sem):
    return pltpu.CompilerParams(dimension_semantics=sem, vmem_limit_bytes=VMEM_LIMIT)


def _rms(x, g):
    return x * lax.rsqrt(jnp.mean(x * x, axis=-1, keepdims=True) + EPS) * g


def _silu(x):
    return x * jax.nn.sigmoid(x)


def _dot(a, b):
    return jnp.dot(a, b, preferred_element_type=F32)


def _const_spec(shape):
    nd = len(shape)
    return pl.BlockSpec(shape, lambda *_: (0,) * nd)


def _mla_common(x_ref, gn_ref, wa_ref, gq_ref, gkv_ref, ck_ref, sk_ref, h_ref, ckv_ref, kpe_ref,
                q_lora, kv_lora, rope):
    h = _rms(x_ref[...], gn_ref[...]).astype(BF16)
    h_ref[...] = h
    z = _dot(h, wa_ref[...])
    cq_n = _rms(z[:, :q_lora], gq_ref[...]).astype(BF16)
    ckv = _rms(z[:, q_lora:q_lora + kv_lora], gkv_ref[...])
    ckv_ref[...] = ckv
    o = q_lora + kv_lora
    kpe = z[:, o:o + LANES] * ck_ref[...] + z[:, o + LANES:o + 2 * LANES] * sk_ref[...]
    kpe_ref[...] = kpe[:, :rope]
    return cq_n, ckv, kpe


def _mla_prep_prompt_kernel(x_ref, gn_ref, wa_ref, gq_ref, wq_ref, gkv_ref, wkv_ref, vone_ref, cq_ref, sq_ref,
                            ck_ref, sk_ref, h_ref, q_ref, k_ref, v_ref, ckv_ref, kpe_ref, *,
                            scale, q_lora, kv_lora, rope):
    cq_n, ckv, kpe = _mla_common(x_ref, gn_ref, wa_ref, gq_ref, gkv_ref, ck_ref, sk_ref, h_ref,
                                 ckv_ref, kpe_ref, q_lora, kv_lora, rope)
    q2 = _dot(cq_n, wq_ref[...])
    w = q2.shape[1] // 2
    reps = w // LANES
    cos = jnp.tile(cq_ref[...], (1, reps))
    sin = jnp.tile(sq_ref[...], (1, reps))
    q_ref[...] = ((q2[:, :w] * cos + q2[:, w:] * sin) * scale).astype(BF16)
    kv_in = jnp.concatenate([ckv.astype(BF16), kpe.astype(BF16)], axis=1)
    kv = _dot(kv_in, wkv_ref[...])
    wk = kv.shape[1] // 2
    k_ref[...] = kv[:, :wk].astype(BF16)
    v_ref[...] = (kv[:, wk:] + vone_ref[...]).astype(BF16)


def _mla_prep_sample_kernel(x_ref, gn_ref, wa_ref, gq_ref, wq_ref, gkv_ref, wabs_ref, ck_ref, sk_ref,
                            h_ref, qlat_ref, qrope_ref, ckv_ref, kpe_ref, *,
                            scale, q_lora, kv_lora, rope, nope_w):
    cq_n, _, _ = _mla_common(x_ref, gn_ref, wa_ref, gq_ref, gkv_ref, ck_ref, sk_ref, h_ref,
                             ckv_ref, kpe_ref, q_lora, kv_lora, rope)
    q2 = _dot(cq_n, wq_ref[...])
    q_nope = (q2[:, :nope_w] * scale).astype(BF16)
    qlat_ref[...] = _dot(q_nope, wabs_ref[...])
    w = (q2.shape[1] - nope_w) // 2
    reps = w // LANES
    cos = jnp.tile(ck_ref[...], (1, reps))
    sin = jnp.tile(sk_ref[...], (1, reps))
    qrope_ref[...] = (q2[:, nope_w:nope_w + w] * cos + q2[:, nope_w + w:] * sin) * scale


def _sum_lane(head):
    return 0 if head % 2 else LANES - 1


def _flash_kernel(it_ref, jt_ref, q_ref, k_ref, v_ref, o_ref, m_scr, acc_scr, *, heads):
    step = pl.program_id(1)
    i = it_ref[step]
    j = jt_ref[step]
    tq = q_ref.shape[1]

    @pl.when(j == 0)
    def _init():
        m_scr[...] = jnp.full(m_scr.shape, -jnp.inf, F32)
        acc_scr[...] = jnp.zeros(acc_scr.shape, F32)

    def update(masked):
        if masked:
            row = lax.broadcasted_iota(jnp.int32, (tq, tq), 0)
            col = lax.broadcasted_iota(jnp.int32, (tq, tq), 1)
            keep = row >= col
        all_scores = [lax.dot_general(q_ref[0, :, h * LANES:(h + 1) * LANES], k_ref[0, :, h * LANES:(h + 1) * LANES],
                                      NT, preferred_element_type=F32) for h in range(heads)]
        for h, s in enumerate(all_scores):
            sl = slice(h * LANES, (h + 1) * LANES)
            if masked:
                s = jnp.where(keep, s, -jnp.inf)
            m_prev = m_scr[h]
            m_new = jnp.maximum(m_prev, jnp.max(s, axis=1, keepdims=True))
            p = jnp.exp2(s - jnp.tile(m_new, (1, tq // LANES)))
            acc_scr[h] = jnp.exp2(m_prev - m_new) * acc_scr[h] + _dot(p.astype(BF16), v_ref[0, :, sl])
            m_scr[h] = m_new

    @pl.when(j < i)
    def _off_diagonal():
        update(False)

    @pl.when(j == i)
    def _diagonal():
        update(True)
        low_half = lax.broadcasted_iota(jnp.int32, (tq, LANES), 1) < LANES // 2
        for p in range(heads // 2):
            even, odd = acc_scr[2 * p], acc_scr[2 * p + 1]
            l_even = even[:, _sum_lane(0):_sum_lane(0) + 1]
            l_odd = odd[:, _sum_lane(1):_sum_lane(1) + 1]
            o_ref[0, :, p * LANES:(p + 1) * LANES] = jnp.where(low_half, even / l_even,
                                                               odd / l_odd).astype(o_ref.dtype)


def _paged_kernel(pt_ref, qlat_ref, qrope_ref, ckv_ref, kpe_ref, lat_hbm, rope_hbm, o_ref,
                  lat0, lat1, rope0, rope1, sem, *, heads, group, page, n_chunks, block_keys, kv_lora, rope):
    r = pl.program_id(0)
    n_req = pl.num_programs(0)
    t = qlat_ref.shape[0]
    lat_bufs = (lat0, lat1)
    rope_bufs = (rope0, rope1)

    def page_copies(req, chunk, slot, g):
        pg = pt_ref[req, chunk * group + g]
        rows = pl.ds(g * page, page)
        return (pltpu.make_async_copy(lat_hbm.at[pg], lat_bufs[slot].at[rows], sem.at[0, slot]),
                pltpu.make_async_copy(rope_hbm.at[pg], rope_bufs[slot].at[g], sem.at[1, slot]))

    def start_chunk(req, chunk, slot):
        for g in range(group):
            for cp in page_copies(req, chunk, slot, g):
                cp.start()

    def wait_chunk(req, chunk, slot):
        for g in range(group):
            for cp in page_copies(req, chunk, slot, g):
                cp.wait()

    @pl.when(r == 0)
    def _prime():
        start_chunk(0, 0, 0)

    q_lat = jnp.concatenate([qlat_ref[:, h * kv_lora:(h + 1) * kv_lora] for h in range(heads)], axis=0)
    q_rope = jnp.concatenate([qrope_ref[:, h * LANES:(h + 1) * LANES] for h in range(heads)],
                             axis=0)[:, :rope]

    def scores(lat, rp, rp_dims):
        return (lax.dot_general(q_lat.astype(lat.dtype), lat, NT, preferred_element_type=F32)
                + lax.dot_general(q_rope.astype(rp.dtype), rp, rp_dims, preferred_element_type=F32))

    def partial_softmax(s, lat):
        m = jnp.max(s, axis=1, keepdims=True)
        p = jnp.exp(s - m)
        return m, jnp.sum(p, axis=1, keepdims=True), _dot(p.astype(lat.dtype), lat)

    def combine(carry, parts):
        m_prev, l_prev, acc = carry
        m_new = m_prev
        for m, _, _ in parts:
            m_new = jnp.maximum(m_new, m)
        alpha = jnp.exp(m_prev - m_new)
        l_new, acc = alpha * l_prev, alpha * acc
        for m, l, a in parts:
            w = jnp.exp(m - m_new)
            l_new, acc = l_new + w * l, acc + w * a
        return m_new, l_new, acc

    def chunk_step(c, slot, carry):
        wait_chunk(r, c, slot)
        wrap = jnp.where(c + 1 == n_chunks, 1, 0)
        start_chunk(jnp.minimum(r + wrap, n_req - 1), (c + 1) * (1 - wrap), 1 - slot)
        pages = block_keys // page
        lat_block = lambda blk: lat_bufs[slot][blk * block_keys:(blk + 1) * block_keys].astype(BF16)
        all_scores = []
        for blk in range(group // pages):
            rope_t = jnp.concatenate([rope_bufs[slot][blk * pages + g] for g in range(pages)], axis=1)
            all_scores.append(scores(lat_block(blk), rope_t.astype(BF16), NN))
        return combine(carry, [partial_softmax(s, lat_block(blk)) for blk, s in enumerate(all_scores)])

    def body(pair, carry):
        return chunk_step(2 * pair + 1, 1, chunk_step(2 * pair, 0, carry))

    rows = heads * t
    init = (jnp.full((rows, 1), -jnp.inf, F32), jnp.zeros((rows, 1), F32), jnp.zeros((rows, kv_lora), F32))
    carry = lax.fori_loop(0, n_chunks // 2, body, init)

    @pl.when(r == n_req - 1)
    def _drain():
        wait_chunk(r, 0, 0)

    lat_new = ckv_ref[...]
    s_new = scores(lat_new, kpe_ref[...], NT)
    tok = lax.broadcasted_iota(jnp.int32, (rows, t), 0) % t
    key = lax.broadcasted_iota(jnp.int32, (rows, t), 1)
    s_new = jnp.where(key <= tok, s_new, -jnp.inf)
    _, l_fin, acc = combine(carry, [partial_softmax(s_new, lat_new)])
    o = acc / l_fin
    for h in range(heads):
        o_ref[:, h * kv_lora:(h + 1) * kv_lora] = o[h * t:(h + 1) * t].astype(o_ref.dtype)


def _hgrn_project(h_ref, wb_ref, lb_ref, g_scr, k_scr, q_scr, v_scr, fdim, vdim):
    z = _dot(h_ref[...], wb_ref[...])
    l0 = lb_ref[0:1, :]
    l1 = lb_ref[1:2, :]
    mx = jnp.maximum(l0, l1)
    e0 = jnp.exp(l0 - mx)
    e1 = jnp.exp(l1 - mx)
    lb = e1 / (e0 + e1)
    f = lb + (1.0 - lb) * jax.nn.sigmoid(z[:, :fdim])
    g_scr[...] = jnp.log(f)
    k_scr[...] = 1.0 - f
    q_scr[...] = _silu(z[:, fdim:2 * fdim])
    v_scr[...] = z[:, 2 * fdim:2 * fdim + vdim]
    return z[:, 2 * fdim + vdim:]


def _cumsum_rows(x):
    row = lax.broadcasted_iota(jnp.int32, x.shape, 0)
    shift = 1
    while shift < x.shape[0]:
        x = x + jnp.where(row >= shift, pltpu.roll(x, shift, 0), 0.0)
        shift *= 2
    return x


def _hgrn_chunk(qq, kk, g, v, st_ref, o_ref, rows, *, sub):
    c = qq.shape[0]
    n_pairs = st_ref.shape[0]
    hv = st_ref.shape[1] // 2
    hk = st_ref.shape[2] // 2
    mm = BF16 if sub % BF16_ROWS == 0 else F32
    bcum = _cumsum_rows(g)
    last = bcum[c - 1:c, :]
    q_in = (qq * jnp.exp(bcum)).astype(mm)
    k_dec = (kk * jnp.exp(last - bcum)).astype(mm)
    v_mm = v.astype(mm)
    n_sub = c // sub
    row_blk = lax.broadcasted_iota(jnp.int32, (c, hk), 0) // sub
    causal = lax.broadcasted_iota(jnp.int32, (c, c), 0) >= lax.broadcasted_iota(jnp.int32, (c, c), 1)
    lane_v = lax.broadcasted_iota(jnp.int32, (1, 2 * hv), 1) // hv
    bd = (lax.broadcasted_iota(jnp.int32, (2 * hv, 2 * hk), 0) // hv
          == lax.broadcasted_iota(jnp.int32, (2 * hv, 2 * hk), 1) // hk)
    def head_scores(hl):
        b_h, q_h = bcum[:, hl], qq[:, hl]
        refs = [b_h[j * sub + sub // 2:j * sub + sub // 2 + 1] for j in range(n_sub)]
        ref_rows = jnp.concatenate([jnp.broadcast_to(rj, (sub, hk)) for rj in refs], axis=0)
        k_all = kk[:, hl] * jnp.exp(ref_rows - b_h)
        k_cat = jnp.concatenate([jnp.where(row_blk == j, k_all, 0.0).astype(mm) for j in range(n_sub)], axis=1)
        q_cat = jnp.concatenate(
            [jnp.concatenate([jnp.zeros((j * sub, hk), F32)] * (j > 0)
                             + [q_h[j * sub:] * jnp.exp(b_h[j * sub:] - refs[j])], axis=0).astype(mm)
             for j in range(n_sub)], axis=1)
        att = lax.dot_general(q_cat, k_cat, NT, preferred_element_type=F32)
        return jnp.where(causal, att, 0.0).astype(mm)

    atts = [head_scores(slice(h * hk, (h + 1) * hk)) for h in range(2 * n_pairs)]
    for p in range(n_pairs):
        pk = slice(p * 2 * hk, (p + 1) * 2 * hk)
        pv = slice(p * 2 * hv, (p + 1) * 2 * hv)
        vp = v_mm[:, pv]
        st = st_ref[p]
        o = lax.dot_general(q_in[:, pk], st.astype(mm), NT, preferred_element_type=F32)
        for e in range(2):
            o = o + _dot(atts[2 * p + e], jnp.where(lane_v == e, vp, jnp.zeros_like(vp)))
        o_ref[rows, pv] = o
        upd = lax.dot_general(vp, k_dec[:, pk], TN, preferred_element_type=F32)
        st_ref[p] = st * jnp.exp(last[:, pk]) + jnp.where(bd, upd, 0.0)


def _hgrn_finish(o, zb, gho_ref, avg_ref, out_ref):
    ms = _dot((o * o).astype(BF16), avg_ref[...])
    out_ref[...] = (o * lax.rsqrt(ms + EPS) * gho_ref[...] * _silu(zb)).astype(out_ref.dtype)


def _compact_state(st):
    hv = st.shape[0] // 2
    hk = st.shape[1] // 2
    return jnp.concatenate([st[:hv, :hk], st[hv:, hk:]], axis=0)


def _expand_state(sc):
    hv = sc.shape[0] // 2
    z = jnp.zeros((hv, sc.shape[1]), F32)
    return jnp.concatenate([jnp.concatenate([sc[:hv], z], axis=1),
                            jnp.concatenate([z, sc[hv:]], axis=1)], axis=0)


def _hgrn_prompt_kernel(h_ref, wb_ref, lb_ref, gho_ref, avg_ref, out_ref, sfin_ref,
                        st_scr, g_scr, k_scr, q_scr, v_scr, o_scr, *, chunk, sub, fdim, vdim):
    tb = pl.program_id(1)

    @pl.when(tb == 0)
    def _init():
        st_scr[...] = jnp.zeros(st_scr.shape, F32)

    zb = _hgrn_project(h_ref, wb_ref, lb_ref, g_scr, k_scr, q_scr, v_scr, fdim, vdim)

    def body(ci, carry):
        rows = pl.ds(pl.multiple_of(ci * chunk, chunk), chunk)
        _hgrn_chunk(q_scr[rows, :], k_scr[rows, :], g_scr[rows, :], v_scr[rows, :], st_scr, o_scr, rows,
                    sub=sub)
        return carry

    lax.fori_loop(0, h_ref.shape[0] // chunk, body, 0)
    _hgrn_finish(o_scr[...], zb, gho_ref, avg_ref, out_ref)

    @pl.when(tb == pl.num_programs(1) - 1)
    def _final():
        for p in range(st_scr.shape[0]):
            sfin_ref[0, p] = _compact_state(st_scr[p])


def _hgrn_sample_kernel(h_ref, wb_ref, lb_ref, gho_ref, avg_ref, s0_ref, out_ref, sfin_ref,
                        st_scr, g_scr, k_scr, q_scr, v_scr, o_scr, *, t, fdim, vdim):
    zb = _hgrn_project(h_ref, wb_ref, lb_ref, g_scr, k_scr, q_scr, v_scr, fdim, vdim)

    def body(ri, carry):
        for p in range(st_scr.shape[0]):
            st_scr[p] = _expand_state(s0_ref[ri, p])
        rows = pl.ds(pl.multiple_of(ri * t, t), t)
        _hgrn_chunk(q_scr[rows, :], k_scr[rows, :], g_scr[rows, :], v_scr[rows, :], st_scr, o_scr, rows,
                    sub=t)
        for p in range(st_scr.shape[0]):
            sfin_ref[ri, p] = _compact_state(st_scr[p])
        return carry

    lax.fori_loop(0, s0_ref.shape[0], body, 0)
    _hgrn_finish(o_scr[...], zb, gho_ref, avg_ref, out_ref)


def _xattn_heads(q, zc, head_k, head_v, heads, hdim):
    outs = []
    for h in range(heads):
        mv = head_v(h)
        s = lax.dot_general(q[:, h * hdim:(h + 1) * hdim], head_k(h), NT, preferred_element_type=F32)
        s = s - jnp.max(s, axis=1, keepdims=True)
        p = jnp.exp(s)
        p = (p / jnp.sum(p, axis=1, keepdims=True)).astype(mv.dtype)
        outs.append(_dot(p, mv))
    return jnp.concatenate(outs, axis=1) * _silu(zc)


def _xattn_prompt_kernel(h_ref, wc_ref, mk_ref, mv_ref, out_ref, *, heads, hdim, scale):
    z = _dot(h_ref[0], wc_ref[...])
    w = heads * hdim
    q = (z[:, :w] * scale).astype(BF16)
    mk = mk_ref[0].astype(BF16)
    mv = mv_ref[0].astype(BF16)
    out_ref[0] = _xattn_heads(q, z[:, w:], lambda h: mk[:, h * hdim:(h + 1) * hdim],
                              lambda h: mv[:, h * hdim:(h + 1) * hdim], heads, hdim).astype(out_ref.dtype)


def _xattn_sample_kernel(h_ref, wc_ref, mk_ref, mv_ref, out_ref, *, heads, hdim, scale, t):
    z = _dot(h_ref[...], wc_ref[...])
    w = heads * hdim
    q = z[:, :w] * scale
    zc = z[:, w:]
    for ri in range(mk_ref.shape[0]):
        rows = slice(ri * t, (ri + 1) * t)
        out_ref[rows, :] = _xattn_heads(q[rows], zc[rows], lambda h: mk_ref[ri, :, h, :],
                                        lambda h: mv_ref[ri, :, h, :], heads, hdim).astype(out_ref.dtype)


def _memkv_kernel(m_ref, g_ref, w_ref, k_ref, v_ref):
    kv = _dot(_rms(m_ref[...], g_ref[...]).astype(BF16), w_ref[...])
    w = kv.shape[1] // 2
    k_ref[...] = kv[:, :w]
    v_ref[...] = kv[:, w:]


def _merge_kernel(x_ref, h_ref, oa_ref, ob_ref, oc_ref, wuv_ref, wm_ref, wbr_ref, wout_ref, gf_ref, y_ref, *,
                  width, from_latent):
    h = h_ref[...]
    d = x_ref.shape[1]
    oa = oa_ref[...]
    if from_latent:
        oa = _dot(oa.astype(BF16), wuv_ref[...])
    else:
        oa = oa.astype(F32)
    za = _dot(h, wm_ref[:, :width])
    branches = ((oa * _silu(za)).astype(BF16), ob_ref[...].astype(BF16), oc_ref[...].astype(BF16))
    merged = jnp.zeros((x_ref.shape[0], d), F32)
    for n, o in enumerate(branches):
        gate = jax.nn.sigmoid(_dot(h, wm_ref[:, width + n * d:width + (n + 1) * d]))
        merged = merged + gate * _dot(o, wbr_ref[n])
    out = x_ref[...] + _dot(merged.astype(BF16), wout_ref[...])
    y_ref[...] = _rms(out, gf_ref[...])


def _rope_tables(pos, rope, lead):
    half = rope // 2
    inv = jnp.exp(-math.log(ROPE_BASE) * jnp.arange(half, dtype=F32) / half)
    ang = pos.astype(F32)[:, None] * inv[None, :]
    n = pos.shape[0]
    pad = jnp.zeros((n, LANES - lead - rope), F32)
    cos = jnp.concatenate([jnp.ones((n, lead), F32), jnp.cos(ang), jnp.cos(ang), pad], axis=1)
    sin = jnp.concatenate([jnp.zeros((n, lead), F32), jnp.sin(ang), jnp.sin(ang), pad], axis=1)
    return cos, sin


def _swap_halves(w):
    half = w.shape[-1] // 2
    return jnp.concatenate([-w[..., half:], w[..., :half]], axis=-1)


def _pad_cols(w, width):
    return jnp.pad(w, [(0, 0)] * (w.ndim - 1) + [(0, width - w.shape[-1])])


def _block_diag(blocks):
    rows = sum(b.shape[0] for b in blocks)
    cols = sum(b.shape[1] for b in blocks)
    out = jnp.zeros((rows, cols), blocks[0].dtype)
    r = c = 0
    for b in blocks:
        out = lax.dynamic_update_slice(out, b, (r, c))
        r += b.shape[0]
        c += b.shape[1]
    return out


def _tile_rows(n, pref):
    t = min(n, pref)
    assert n % t == 0
    return t


def kernel(x_prompt, x_sample, mem_prompt, cache_kv_latent, cache_k_rope, page_table, state_hgrn, cache_mem_k, cache_mem_v, g_norm, w_in, g_q_lora, w_uq, g_kv_lora, w_uk, w_uv, lb_logits, g_hgrn_out, g_mem_norm, w_mem_kv, w_branch, w_out, g_final):
    depth = w_in.shape[0]
    assert depth == 1, "one layer per step"
    b, t, d = x_prompt.shape
    nb, ts, _ = x_sample.shape
    n_mem = mem_prompt.shape[1]
    n_phys, page, kv_lora = cache_kv_latent.shape[1:]
    rope = cache_k_rope.shape[-1]
    n_pages = page_table.shape[1]
    q_lora = g_q_lora.shape[-1]
    a_heads, a_qk = w_uq.shape[2:]
    a_nope = w_uk.shape[-1]
    a_vdim = w_uv.shape[-1]
    b_heads, b_expand, b_vdim = state_hgrn.shape[2:]
    fdim = b_heads * b_expand
    c_heads, c_hdim = cache_mem_k.shape[3:]
    width = w_branch.shape[2]
    assert a_qk == a_nope + rope and a_heads * a_vdim == width and b_heads * b_vdim == width
    assert c_heads * c_hdim == width and a_qk <= LANES and 2 * a_vdim == LANES and 2 * b_vdim == LANES
    assert b_expand == LANES and a_heads % 2 == 0 and b_heads % 2 == 0

    splits = (q_lora, kv_lora, rope, width, fdim, fdim, width, width, width, width, N_BRANCH * d)
    offs = np.concatenate([[0], np.cumsum(splits)])
    assert offs[-1] == w_in.shape[-1]
    w_in0 = w_in[0]
    col = lambda i: w_in0[:, offs[i]:offs[i + 1]]

    a_scale = a_qk ** -0.5
    n_p = b * t
    n_s = nb * ts
    xp = x_prompt.reshape(n_p, d)
    xs = x_sample.reshape(n_s, d)
    row = lambda g: g.reshape(1, -1).astype(F32)

    w_a = jnp.concatenate([col(0), col(1), _pad_cols(col(2), LANES), _pad_cols(_swap_halves(col(2)), LANES)],
                          axis=1).astype(BF16)
    uq = w_uq[0]
    uq_nope, uq_rope = uq[..., :a_nope], uq[..., a_nope:]
    zero_nope = jnp.zeros_like(uq_nope)
    wq_prompt = jnp.concatenate([
        _pad_cols(uq, LANES).reshape(q_lora, -1),
        _pad_cols(jnp.concatenate([zero_nope, _swap_halves(uq_rope)], axis=-1), LANES).reshape(q_lora, -1),
    ], axis=1).astype(BF16)
    wq_sample = jnp.concatenate([
        uq_nope.reshape(q_lora, -1),
        _pad_cols(uq_rope, LANES).reshape(q_lora, -1),
        _pad_cols(_swap_halves(uq_rope), LANES).reshape(q_lora, -1),
    ], axis=1).astype(BF16)
    uk, uv = w_uk[0], w_uv[0]
    place = jnp.zeros((LANES, a_heads, LANES), F32)
    place = place.at[jnp.arange(rope)[:, None], jnp.arange(a_heads)[None, :],
                     a_nope + jnp.arange(rope)[:, None]].set(1.0)
    wk = jnp.concatenate([_pad_cols(uk, LANES).reshape(kv_lora, -1), place.reshape(LANES, -1)], axis=0)
    uv_pad = jnp.stack([jnp.pad(uv[:, h], ((0, 0), ((h % 2) * a_vdim, LANES - a_vdim - (h % 2) * a_vdim)))
                        for h in range(a_heads)], axis=1)
    wv = jnp.concatenate([uv_pad.reshape(kv_lora, -1), jnp.zeros((LANES, a_heads * LANES), F32)], axis=0)
    w_kv = jnp.concatenate([wk, wv], axis=1).astype(BF16)
    v_one = jnp.zeros((a_heads, LANES), F32).at[
        jnp.arange(a_heads), jnp.asarray([_sum_lane(h) for h in range(a_heads)])].set(1.0).reshape(1, -1)
    w_abs = _block_diag([uk[:, h].T for h in range(a_heads)]).astype(BF16)
    w_uv_bd = _block_diag([uv[:, h] for h in range(a_heads)]).astype(BF16)
    w_b = jnp.concatenate([col(4), col(5), col(6), col(7)], axis=1).astype(BF16)
    w_c = jnp.concatenate([col(8), col(9)], axis=1).astype(BF16)
    w_m = jnp.concatenate([col(3), col(10)], axis=1).astype(BF16)
    w_br = w_branch[0].astype(BF16)
    w_o = w_out[0].astype(BF16)
    w_mem = w_mem_kv[0].astype(BF16)
    gho = jnp.tile(g_hgrn_out[0], b_heads).reshape(1, width).astype(F32)
    head_of = jnp.arange(width) // b_vdim
    avg = ((head_of[:, None] == head_of[None, :]).astype(F32) / b_vdim).astype(BF16)
    lb2 = lb_logits[:2].astype(F32)

    n_past = n_pages * page
    cq_p, sq_p = _rope_tables(jnp.arange(t), rope, a_nope)
    ck_p, sk_p = _rope_tables(jnp.arange(t), rope, 0)
    tm_s = _tile_rows(n_s, 256)
    assert tm_s % ts == 0
    ck_s, sk_s = _rope_tables(n_past + (jnp.arange(tm_s) % ts), rope, 0)

    tm = _tile_rows(t, 512)
    n_t = t // tm
    mla_common = dict(q_lora=q_lora, kv_lora=kv_lora, rope=rope)
    rowspec = lambda w: pl.BlockSpec((tm, w), lambda i: (i, 0))
    tabspec = pl.BlockSpec((tm, LANES), lambda i: (i % n_t, 0))
    hq = a_heads * LANES
    h_p, q_p, k_p, v_p, ckv_p, kpe_p = pl.pallas_call(
        functools.partial(_mla_prep_prompt_kernel, scale=a_scale * math.log2(math.e), **mla_common),
        grid=(n_p // tm,),
        in_specs=[rowspec(d), _const_spec((1, d)), _const_spec(w_a.shape), _const_spec((1, q_lora)),
                  _const_spec(wq_prompt.shape), _const_spec((1, kv_lora)), _const_spec(w_kv.shape),
                  _const_spec((1, hq)), tabspec, tabspec, tabspec, tabspec],
        out_specs=[rowspec(d), rowspec(hq), rowspec(hq), rowspec(hq), rowspec(kv_lora), rowspec(rope)],
        out_shape=[jax.ShapeDtypeStruct((n_p, d), BF16), jax.ShapeDtypeStruct((n_p, hq), BF16),
                   jax.ShapeDtypeStruct((n_p, hq), BF16), jax.ShapeDtypeStruct((n_p, hq), BF16),
                   jax.ShapeDtypeStruct((n_p, kv_lora), F32), jax.ShapeDtypeStruct((n_p, rope), F32)],
        compiler_params=_params(("parallel",)),
    )(xp, row(g_norm), w_a, row(g_q_lora), wq_prompt, row(g_kv_lora), w_kv, v_one, cq_p, sq_p, ck_p, sk_p)

    srow = lambda w: pl.BlockSpec((tm_s, w), lambda i: (i, 0))
    hl = a_heads * kv_lora
    h_s, qlat_s, qrope_s, ckv_s, kpe_s = pl.pallas_call(
        functools.partial(_mla_prep_sample_kernel, scale=a_scale, nope_w=a_heads * a_nope, **mla_common),
        grid=(n_s // tm_s,),
        in_specs=[srow(d), _const_spec((1, d)), _const_spec(w_a.shape), _const_spec((1, q_lora)),
                  _const_spec(wq_sample.shape), _const_spec((1, kv_lora)), _const_spec(w_abs.shape),
                  _const_spec((tm_s, LANES)), _const_spec((tm_s, LANES))],
        out_specs=[srow(d), srow(hl), srow(hq), srow(kv_lora), srow(rope)],
        out_shape=[jax.ShapeDtypeStruct((n_s, d), BF16), jax.ShapeDtypeStruct((n_s, hl), F32),
                   jax.ShapeDtypeStruct((n_s, hq), F32), jax.ShapeDtypeStruct((n_s, kv_lora), F32),
                   jax.ShapeDtypeStruct((n_s, rope), F32)],
        compiler_params=_params(("parallel",)),
    )(xs, row(g_norm), w_a, row(g_q_lora), wq_sample, row(g_kv_lora), w_abs, ck_s, sk_s)

    tq = _tile_rows(t, 512)
    nq = t // tq
    pairs = [(i, j) for i in range(nq) for j in range(i + 1)]
    i_tab = jnp.asarray([p[0] for p in pairs], jnp.int32)
    j_tab = jnp.asarray([p[1] for p in pairs], jnp.int32)
    qkv = lambda a: a.reshape(b, t, hq)
    oa_p = pl.pallas_call(
        functools.partial(_flash_kernel, heads=a_heads),
        grid_spec=pltpu.PrefetchScalarGridSpec(
            num_scalar_prefetch=2, grid=(b, len(pairs)),
            in_specs=[pl.BlockSpec((1, tq, hq), lambda bi, s, it, jt: (bi, it[s], 0)),
                      pl.BlockSpec((1, tq, hq), lambda bi, s, it, jt: (bi, jt[s], 0)),
                      pl.BlockSpec((1, tq, hq), lambda bi, s, it, jt: (bi, jt[s], 0))],
            out_specs=pl.BlockSpec((1, tq, width), lambda bi, s, it, jt: (bi, it[s], 0)),
            scratch_shapes=[pltpu.VMEM((a_heads, tq, LANES), F32)] * 2),
        out_shape=jax.ShapeDtypeStruct((b, t, width), BF16),
        compiler_params=_params(("parallel", "arbitrary")),
    )(i_tab, j_tab, qkv(q_p), qkv(k_p), qkv(v_p)).reshape(n_p, width)

    assert n_pages % 2 == 0
    group = math.gcd(n_pages // 2, 32)
    n_chunks = n_pages // group
    assert n_chunks % 2 == 0
    block_keys = math.gcd(group, 8) * page
    req = lambda w: pl.BlockSpec((ts, w), lambda r, pt: (r, 0))
    olat_s = pl.pallas_call(
        functools.partial(_paged_kernel, heads=a_heads, group=group, page=page, n_chunks=n_chunks,
                          block_keys=block_keys,
                          kv_lora=kv_lora, rope=rope),
        grid_spec=pltpu.PrefetchScalarGridSpec(
            num_scalar_prefetch=1, grid=(nb,),
            in_specs=[req(hl), req(hq), req(kv_lora), req(rope),
                      pl.BlockSpec(memory_space=pl.ANY), pl.BlockSpec(memory_space=pl.ANY)],
            out_specs=req(hl),
            scratch_shapes=[pltpu.VMEM((group * page, kv_lora), F32)] * 2
            + [pltpu.VMEM((group, rope, page), F32)] * 2 + [pltpu.SemaphoreType.DMA((2, 2))]),
        out_shape=jax.ShapeDtypeStruct((n_s, hl), F32),
        compiler_params=_params(("arbitrary",)),
    )(page_table, qlat_s, qrope_s, ckv_s, kpe_s, cache_kv_latent.reshape(n_phys, page, kv_lora),
      jnp.swapaxes(cache_k_rope, 2, 3).reshape(n_phys, rope, page))

    n_pairs = b_heads // 2
    tb = _tile_rows(t, 256)
    chunk = _tile_rows(tb, 64)
    sub = _tile_rows(chunk, 16)
    wb_cols = w_b.shape[1]
    hgrn_scratch = lambda rows: [pltpu.VMEM((n_pairs, LANES, 2 * b_expand), F32),
                                 pltpu.VMEM((rows, fdim), F32), pltpu.VMEM((rows, fdim), F32),
                                 pltpu.VMEM((rows, fdim), F32), pltpu.VMEM((rows, width), F32),
                                 pltpu.VMEM((rows, width), F32)]
    ob_p, st_p = pl.pallas_call(
        functools.partial(_hgrn_prompt_kernel, chunk=chunk, sub=sub, fdim=fdim, vdim=width),
        grid=(b, t // tb),
        in_specs=[pl.BlockSpec((tb, d), lambda bi, ti: (bi * (t // tb) + ti, 0)),
                  _const_spec((d, wb_cols)), _const_spec((2, fdim)), _const_spec((1, width)),
                  _const_spec((width, width))],
        out_specs=[pl.BlockSpec((tb, width), lambda bi, ti: (bi * (t // tb) + ti, 0)),
                   pl.BlockSpec((1, n_pairs, LANES, b_expand), lambda bi, ti: (bi, 0, 0, 0))],
        out_shape=[jax.ShapeDtypeStruct((n_p, width), BF16),
                   jax.ShapeDtypeStruct((b, n_pairs, LANES, b_expand), F32)],
        scratch_shapes=hgrn_scratch(tb),
        compiler_params=_params(("parallel", "arbitrary")),
    )(h_p, w_b, lb2, gho, avg)

    def to_pairs(s):
        n = s.shape[0]
        return s.reshape(n, n_pairs, 2, b_expand, b_vdim).transpose(0, 1, 2, 4, 3).reshape(
            n, n_pairs, LANES, b_expand)

    def from_pairs(s):
        n = s.shape[0]
        return s.reshape(n, n_pairs, 2, b_vdim, b_expand).transpose(0, 1, 2, 4, 3).reshape(
            n, b_heads, b_expand, b_vdim)

    rq = _tile_rows(nb, 8)
    ob_s, st_s = pl.pallas_call(
        functools.partial(_hgrn_sample_kernel, t=ts, fdim=fdim, vdim=width),
        grid=(nb // rq,),
        in_specs=[pl.BlockSpec((rq * ts, d), lambda i: (i, 0)),
                  _const_spec((d, wb_cols)), _const_spec((2, fdim)), _const_spec((1, width)),
                  _const_spec((width, width)),
                  pl.BlockSpec((rq, n_pairs, LANES, b_expand), lambda i: (i, 0, 0, 0))],
        out_specs=[pl.BlockSpec((rq * ts, width), lambda i: (i, 0)),
                   pl.BlockSpec((rq, n_pairs, LANES, b_expand), lambda i: (i, 0, 0, 0))],
        out_shape=[jax.ShapeDtypeStruct((n_s, width), BF16),
                   jax.ShapeDtypeStruct((nb, n_pairs, LANES, b_expand), F32)],
        scratch_shapes=hgrn_scratch(rq * ts),
        compiler_params=_params(("parallel",)),
    )(h_s, w_b, lb2, gho, avg, to_pairs(state_hgrn.reshape(nb, b_heads, b_expand, b_vdim)))

    n_m = b * n_mem
    tmm = _tile_rows(n_m, 512)
    mk_p, mv_p = pl.pallas_call(
        _memkv_kernel,
        grid=(n_m // tmm,),
        in_specs=[pl.BlockSpec((tmm, d), lambda i: (i, 0)), _const_spec((1, d)), _const_spec(w_mem.shape)],
        out_specs=[pl.BlockSpec((tmm, width), lambda i: (i, 0)), pl.BlockSpec((tmm, width), lambda i: (i, 0))],
        out_shape=[jax.ShapeDtypeStruct((n_m, width), F32), jax.ShapeDtypeStruct((n_m, width), F32)],
        compiler_params=_params(("parallel",)),
    )(mem_prompt.reshape(n_m, d), row(g_mem_norm), w_mem)

    c_scale = c_hdim ** -0.5
    tx = _tile_rows(t, 512)
    memspec = pl.BlockSpec((1, n_mem, width), lambda bi, ti: (bi, 0, 0))
    oc_p = pl.pallas_call(
        functools.partial(_xattn_prompt_kernel, heads=c_heads, hdim=c_hdim, scale=c_scale),
        grid=(b, t // tx),
        in_specs=[pl.BlockSpec((1, tx, d), lambda bi, ti: (bi, ti, 0)), _const_spec(w_c.shape), memspec, memspec],
        out_specs=pl.BlockSpec((1, tx, width), lambda bi, ti: (bi, ti, 0)),
        out_shape=jax.ShapeDtypeStruct((b, t, width), BF16),
        compiler_params=_params(("parallel", "parallel")),
    )(h_p.reshape(b, t, d), w_c, mk_p.reshape(b, n_mem, width), mv_p.reshape(b, n_mem, width)).reshape(n_p, width)

    rx = _tile_rows(nb, 4)
    smem = pl.BlockSpec((rx, n_mem, c_heads, c_hdim), lambda i: (i, 0, 0, 0))
    oc_s = pl.pallas_call(
        functools.partial(_xattn_sample_kernel, heads=c_heads, hdim=c_hdim, scale=c_scale, t=ts),
        grid=(nb // rx,),
        in_specs=[pl.BlockSpec((rx * ts, d), lambda i: (i, 0)), _const_spec(w_c.shape), smem, smem],
        out_specs=pl.BlockSpec((rx * ts, width), lambda i: (i, 0)),
        out_shape=jax.ShapeDtypeStruct((n_s, width), F32),
        compiler_params=_params(("parallel",)),
    )(h_s, w_c, cache_mem_k.reshape(nb, n_mem, c_heads, c_hdim), cache_mem_v.reshape(nb, n_mem, c_heads, c_hdim))

    def merge(x2, h2, oa, ob, oc, from_latent):
        n = x2.shape[0]
        tmg = _tile_rows(n, 512)
        rs = lambda w: pl.BlockSpec((tmg, w), lambda i: (i, 0))
        return pl.pallas_call(
            functools.partial(_merge_kernel, width=width, from_latent=from_latent),
            grid=(n // tmg,),
            in_specs=[rs(d), rs(d), rs(oa.shape[1]), rs(width), rs(width), _const_spec(w_uv_bd.shape),
                      _const_spec(w_m.shape), _const_spec(w_br.shape), _const_spec(w_o.shape),
                      _const_spec((1, d))],
            out_specs=rs(d),
            out_shape=jax.ShapeDtypeStruct((n, d), F32),
            compiler_params=_params(("parallel",)),
        )(x2, h2, oa, ob, oc, w_uv_bd, w_m, w_br, w_o, row(g_final))

    y_p = merge(xp, h_p, oa_p, ob_p, oc_p, False).reshape(b, t, d)
    y_s = merge(xs, h_s, olat_s, ob_s, oc_s, True).reshape(nb, ts, d)

    return (y_p, y_s,
            ckv_p.reshape(1, b, t, kv_lora), kpe_p.reshape(1, b, t, rope),
            from_pairs(st_p)[None],
            mk_p.reshape(1, b, n_mem, c_heads, c_hdim), mv_p.reshape(1, b, n_mem, c_heads, c_hdim),
            ckv_s.reshape(1, nb, ts, kv_lora), kpe_s.reshape(1, nb, ts, rope),
            from_pairs(st_s)[None])
```

```python
import functools
import math

import numpy as np
import jax
import jax.numpy as jnp
from jax import lax
from jax.experimental import pallas as pl
from jax.experimental.pallas import tpu as pltpu

F32 = jnp.float32
BF16 = jnp.bfloat16
EPS = 1e-6
ROPE_BASE = 10000.0
N_BRANCH = 3
LANES = 128
BF16_ROWS = 16
PAGED_SLOTS = 4
PAGED_AHEAD = 2
VMEM_LIMIT = 56 * 1024 * 1024
NT = (((1,), (1,)), ((), ()))
TN = (((0,), (0,)), ((), ()))
NN = (((1,), (0,)), ((), ()))


def _params(sem):
    return pltpu.CompilerParams(dimension_semantics=sem, vmem_limit_bytes=VMEM_LIMIT)


def _rms(x, g):
    return x * lax.rsqrt(jnp.mean(x * x, axis=-1, keepdims=True) + EPS) * g


def _silu(x):
    return x * jax.nn.sigmoid(x)


def _dot(a, b):
    return jnp.dot(a, b, preferred_element_type=F32)


def _const_spec(shape):
    nd = len(shape)
    return pl.BlockSpec(shape, lambda *_: (0,) * nd)


def _mla_common(x_ref, gn_ref, wa_ref, gq_ref, gkv_ref, ck_ref, sk_ref, h_ref, ckv_ref, kpe_ref,
                q_lora, kv_lora, rope):
    h = _rms(x_ref[...], gn_ref[...]).astype(BF16)
    h_ref[...] = h
    z = _dot(h, wa_ref[...])
    cq_n = _rms(z[:, :q_lora], gq_ref[...]).astype(BF16)
    ckv = _rms(z[:, q_lora:q_lora + kv_lora], gkv_ref[...])
    ckv_ref[...] = ckv
    o = q_lora + kv_lora
    kpe = z[:, o:o + LANES] * ck_ref[...] + z[:, o + LANES:o + 2 * LANES] * sk_ref[...]
    kpe_ref[...] = kpe[:, :rope]
    return cq_n, ckv, kpe


def _mla_prep_prompt_kernel(x_ref, gn_ref, wa_ref, gq_ref, wq_ref, gkv_ref, wkv_ref, vone_ref, cq_ref, sq_ref,
                            ck_ref, sk_ref, h_ref, q_ref, k_ref, v_ref, ckv_ref, kpe_ref, *,
                            scale, q_lora, kv_lora, rope):
    cq_n, ckv, kpe = _mla_common(x_ref, gn_ref, wa_ref, gq_ref, gkv_ref, ck_ref, sk_ref, h_ref,
                                 ckv_ref, kpe_ref, q_lora, kv_lora, rope)
    q2 = _dot(cq_n, wq_ref[...])
    w = q2.shape[1] // 2
    reps = w // LANES
    cos = jnp.tile(cq_ref[...], (1, reps))
    sin = jnp.tile(sq_ref[...], (1, reps))
    q_ref[...] = ((q2[:, :w] * cos + q2[:, w:] * sin) * scale).astype(BF16)
    kv_in = jnp.concatenate([ckv.astype(BF16), kpe.astype(BF16)], axis=1)
    kv = _dot(kv_in, wkv_ref[...])
    wk = kv.shape[1] // 2
    k_ref[...] = kv[:, :wk].astype(BF16)
    v_ref[...] = (kv[:, wk:] + vone_ref[...]).astype(BF16)


def _mla_prep_sample_kernel(x_ref, gn_ref, wa_ref, gq_ref, wq_ref, gkv_ref, wabs_ref, ck_ref, sk_ref,
                            h_ref, qlat_ref, qrope_ref, ckv_ref, kpe_ref, *,
                            scale, q_lora, kv_lora, rope, nope_w):
    cq_n, _, _ = _mla_common(x_ref, gn_ref, wa_ref, gq_ref, gkv_ref, ck_ref, sk_ref, h_ref,
                             ckv_ref, kpe_ref, q_lora, kv_lora, rope)
    q2 = _dot(cq_n, wq_ref[...])
    q_nope = (q2[:, :nope_w] * scale).astype(BF16)
    qlat_ref[...] = _dot(q_nope, wabs_ref[...])
    w = (q2.shape[1] - nope_w) // 2
    reps = w // LANES
    cos = jnp.tile(ck_ref[...], (1, reps))
    sin = jnp.tile(sk_ref[...], (1, reps))
    qrope_ref[...] = (q2[:, nope_w:nope_w + w] * cos + q2[:, nope_w + w:] * sin) * scale


def _sum_lane(head):
    return 0 if head % 2 else LANES - 1


def _flash_kernel(it_ref, jt_ref, q_ref, k_ref, v_ref, o_ref, m_scr, acc_scr, *, heads):
    step = pl.program_id(1)
    i = it_ref[step]
    j = jt_ref[step]
    tq = q_ref.shape[1]

    @pl.when(j == 0)
    def _init():
        m_scr[...] = jnp.full(m_scr.shape, -jnp.inf, F32)
        acc_scr[...] = jnp.zeros(acc_scr.shape, F32)

    def update(masked):
        if masked:
            row = lax.broadcasted_iota(jnp.int32, (tq, tq), 0)
            col = lax.broadcasted_iota(jnp.int32, (tq, tq), 1)
            keep = row >= col
        all_scores = [lax.dot_general(q_ref[0, :, h * LANES:(h + 1) * LANES], k_ref[0, :, h * LANES:(h + 1) * LANES],
                                      NT, preferred_element_type=F32) for h in range(heads)]
        for h, s in enumerate(all_scores):
            sl = slice(h * LANES, (h + 1) * LANES)
            if masked:
                s = jnp.where(keep, s, -jnp.inf)
            m_prev = m_scr[h]
            m_new = jnp.maximum(m_prev, jnp.max(s, axis=1, keepdims=True))
            p = jnp.exp2(s - jnp.tile(m_new, (1, tq // LANES)))
            acc_scr[h] = jnp.exp2(m_prev - m_new) * acc_scr[h] + _dot(p.astype(BF16), v_ref[0, :, sl])
            m_scr[h] = m_new

    @pl.when(j < i)
    def _off_diagonal():
        update(False)

    @pl.when(j == i)
    def _diagonal():
        update(True)
        low_half = lax.broadcasted_iota(jnp.int32, (tq, LANES), 1) < LANES // 2
        for p in range(heads // 2):
            even, odd = acc_scr[2 * p], acc_scr[2 * p + 1]
            l_even = even[:, _sum_lane(0):_sum_lane(0) + 1]
            l_odd = odd[:, _sum_lane(1):_sum_lane(1) + 1]
            o_ref[0, :, p * LANES:(p + 1) * LANES] = jnp.where(low_half, even / l_even,
                                                               odd / l_odd).astype(o_ref.dtype)


def _paged_kernel(pt_ref, qlat_ref, qrope_ref, ckv_ref, kpe_ref, lat_hbm, rope_hbm, o_ref,
                  *scratch, heads, group, page, n_chunks, block_keys, kv_lora, rope):
    r = pl.program_id(0)
    n_req = pl.num_programs(0)
    t = qlat_ref.shape[0]
    lat_bufs = scratch[:PAGED_SLOTS]
    rope_bufs = scratch[PAGED_SLOTS:2 * PAGED_SLOTS]
    sem = scratch[2 * PAGED_SLOTS]

    def page_copies(req, chunk, slot, g):
        pg = pt_ref[req, chunk * group + g]
        rows = pl.ds(g * page, page)
        return (pltpu.make_async_copy(lat_hbm.at[pg], lat_bufs[slot].at[rows], sem.at[0, slot]),
                pltpu.make_async_copy(rope_hbm.at[pg], rope_bufs[slot].at[g], sem.at[1, slot]))

    def start_chunk(req, chunk, slot):
        for g in range(group):
            for cp in page_copies(req, chunk, slot, g):
                cp.start()

    def wait_chunk(req, chunk, slot):
        for g in range(group):
            for cp in page_copies(req, chunk, slot, g):
                cp.wait()

    @pl.when(r == 0)
    def _prime():
        for c in range(PAGED_AHEAD):
            start_chunk(0, c, c)

    q_lat = jnp.concatenate([qlat_ref[:, h * kv_lora:(h + 1) * kv_lora] for h in range(heads)], axis=0)
    q_rope = jnp.concatenate([qrope_ref[:, h * LANES:(h + 1) * LANES] for h in range(heads)],
                             axis=0)[:, :rope]

    def scores(lat, rp, rp_dims):
        return (lax.dot_general(q_lat.astype(lat.dtype), lat, NT, preferred_element_type=F32)
                + lax.dot_general(q_rope.astype(rp.dtype), rp, rp_dims, preferred_element_type=F32))

    def partial_softmax(s, lat):
        m = jnp.max(s, axis=1, keepdims=True)
        p = jnp.exp(s - m)
        return m, jnp.sum(p, axis=1, keepdims=True), _dot(p.astype(lat.dtype), lat)

    def combine(carry, parts):
        m_prev, l_prev, acc = carry
        m_new = m_prev
        for m, _, _ in parts:
            m_new = jnp.maximum(m_new, m)
        alpha = jnp.exp(m_prev - m_new)
        l_new, acc = alpha * l_prev, alpha * acc
        for m, l, a in parts:
            w = jnp.exp(m - m_new)
            l_new, acc = l_new + w * l, acc + w * a
        return m_new, l_new, acc

    def chunk_step(c, slot, carry):
        wait_chunk(r, c, slot)
        wrap = jnp.where(c + PAGED_AHEAD >= n_chunks, 1, 0)
        start_chunk(jnp.minimum(r + wrap, n_req - 1), c + PAGED_AHEAD - wrap * n_chunks,
                    (slot + PAGED_AHEAD) % PAGED_SLOTS)
        pages = block_keys // page
        lat_block = lambda blk: lat_bufs[slot][blk * block_keys:(blk + 1) * block_keys].astype(BF16)
        all_scores = []
        for blk in range(group // pages):
            rope_t = jnp.concatenate([rope_bufs[slot][blk * pages + g] for g in range(pages)], axis=1)
            all_scores.append(scores(lat_block(blk), rope_t.astype(BF16), NN))
        return combine(carry, [partial_softmax(s, lat_block(blk)) for blk, s in enumerate(all_scores)])

    def body(it, carry):
        for slot in range(PAGED_SLOTS):
            carry = chunk_step(PAGED_SLOTS * it + slot, slot, carry)
        return carry

    rows = heads * t
    init = (jnp.full((rows, 1), -jnp.inf, F32), jnp.zeros((rows, 1), F32), jnp.zeros((rows, kv_lora), F32))
    carry = lax.fori_loop(0, n_chunks // PAGED_SLOTS, body, init)

    @pl.when(r == n_req - 1)
    def _drain():
        for c in range(PAGED_AHEAD):
            wait_chunk(r, c, c)

    lat_new = ckv_ref[...]
    s_new = scores(lat_new, kpe_ref[...], NT)
    tok = lax.broadcasted_iota(jnp.int32, (rows, t), 0) % t
    key = lax.broadcasted_iota(jnp.int32, (rows, t), 1)
    s_new = jnp.where(key <= tok, s_new, -jnp.inf)
    _, l_fin, acc = combine(carry, [partial_softmax(s_new, lat_new)])
    o = acc / l_fin
    for h in range(heads):
        o_ref[:, h * kv_lora:(h + 1) * kv_lora] = o[h * t:(h + 1) * t].astype(o_ref.dtype)


def _hgrn_project(h_ref, wb_ref, lb_ref, g_scr, k_scr, q_scr, v_scr, fdim, vdim):
    z = _dot(h_ref[...], wb_ref[...])
    l0 = lb_ref[0:1, :]
    l1 = lb_ref[1:2, :]
    mx = jnp.maximum(l0, l1)
    e0 = jnp.exp(l0 - mx)
    e1 = jnp.exp(l1 - mx)
    lb = e1 / (e0 + e1)
    f = lb + (1.0 - lb) * jax.nn.sigmoid(z[:, :fdim])
    g_scr[...] = jnp.log(f)
    k_scr[...] = 1.0 - f
    q_scr[...] = _silu(z[:, fdim:2 * fdim])
    v_scr[...] = z[:, 2 * fdim:2 * fdim + vdim]
    return z[:, 2 * fdim + vdim:]


def _cumsum_rows(x):
    row = lax.broadcasted_iota(jnp.int32, x.shape, 0)
    shift = 1
    while shift < x.shape[0]:
        x = x + jnp.where(row >= shift, pltpu.roll(x, shift, 0), 0.0)
        shift *= 2
    return x


def _hgrn_chunk(qq, kk, g, v, st_ref, o_ref, rows, *, sub):
    c = qq.shape[0]
    n_pairs = st_ref.shape[0]
    hv = st_ref.shape[1] // 2
    hk = st_ref.shape[2] // 2
    mm = BF16 if sub % BF16_ROWS == 0 else F32
    bcum = _cumsum_rows(g)
    last = bcum[c - 1:c, :]
    q_in = (qq * jnp.exp(bcum)).astype(mm)
    k_dec = (kk * jnp.exp(last - bcum)).astype(mm)
    v_mm = v.astype(mm)
    n_sub = c // sub
    row_blk = lax.broadcasted_iota(jnp.int32, (c, hk), 0) // sub
    causal = lax.broadcasted_iota(jnp.int32, (c, c), 0) >= lax.broadcasted_iota(jnp.int32, (c, c), 1)
    lane_v = lax.broadcasted_iota(jnp.int32, (1, 2 * hv), 1) // hv
    bd = (lax.broadcasted_iota(jnp.int32, (2 * hv, 2 * hk), 0) // hv
          == lax.broadcasted_iota(jnp.int32, (2 * hv, 2 * hk), 1) // hk)
    def head_scores(hl):
        b_h, q_h = bcum[:, hl], qq[:, hl]
        refs = [b_h[j * sub + sub // 2:j * sub + sub // 2 + 1] for j in range(n_sub)]
        ref_rows = jnp.concatenate([jnp.broadcast_to(rj, (sub, hk)) for rj in refs], axis=0)
        k_all = kk[:, hl] * jnp.exp(ref_rows - b_h)
        k_cat = jnp.concatenate([jnp.where(row_blk == j, k_all, 0.0).astype(mm) for j in range(n_sub)], axis=1)
        q_cat = jnp.concatenate(
            [jnp.concatenate([jnp.zeros((j * sub, hk), F32)] * (j > 0)
                             + [q_h[j * sub:] * jnp.exp(b_h[j * sub:] - refs[j])], axis=0).astype(mm)
             for j in range(n_sub)], axis=1)
        att = lax.dot_general(q_cat, k_cat, NT, preferred_element_type=F32)
        return jnp.where(causal, att, 0.0).astype(mm)

    atts = [head_scores(slice(h * hk, (h + 1) * hk)) for h in range(2 * n_pairs)]
    for p in range(n_pairs):
        pk = slice(p * 2 * hk, (p + 1) * 2 * hk)
        pv = slice(p * 2 * hv, (p + 1) * 2 * hv)
        vp = v_mm[:, pv]
        st = st_ref[p]
        o = lax.dot_general(q_in[:, pk], st.astype(mm), NT, preferred_element_type=F32)
        for e in range(2):
            o = o + _dot(atts[2 * p + e], jnp.where(lane_v == e, vp, jnp.zeros_like(vp)))
        o_ref[rows, pv] = o
        upd = lax.dot_general(vp, k_dec[:, pk], TN, preferred_element_type=F32)
        st_ref[p] = st * jnp.exp(last[:, pk]) + jnp.where(bd, upd, 0.0)


def _hgrn_finish(o, zb, gho_ref, avg_ref, out_ref):
    ms = _dot((o * o).astype(BF16), avg_ref[...])
    out_ref[...] = (o * lax.rsqrt(ms + EPS) * gho_ref[...] * _silu(zb)).astype(out_ref.dtype)


def _compact_state(st):
    hv = st.shape[0] // 2
    hk = st.shape[1] // 2
    return jnp.concatenate([st[:hv, :hk], st[hv:, hk:]], axis=0)


def _expand_state(sc):
    hv = sc.shape[0] // 2
    z = jnp.zeros((hv, sc.shape[1]), F32)
    return jnp.concatenate([jnp.concatenate([sc[:hv], z], axis=1),
                            jnp.concatenate([z, sc[hv:]], axis=1)], axis=0)


def _hgrn_prompt_kernel(h_ref, wb_ref, lb_ref, gho_ref, avg_ref, out_ref, sfin_ref,
                        st_scr, g_scr, k_scr, q_scr, v_scr, o_scr, *, chunk, sub, fdim, vdim):
    tb = pl.program_id(1)

    @pl.when(tb == 0)
    def _init():
        st_scr[...] = jnp.zeros(st_scr.shape, F32)

    zb = _hgrn_project(h_ref, wb_ref, lb_ref, g_scr, k_scr, q_scr, v_scr, fdim, vdim)

    def body(ci, carry):
        rows = pl.ds(pl.multiple_of(ci * chunk, chunk), chunk)
        _hgrn_chunk(q_scr[rows, :], k_scr[rows, :], g_scr[rows, :], v_scr[rows, :], st_scr, o_scr, rows,
                    sub=sub)
        return carry

    lax.fori_loop(0, h_ref.shape[0] // chunk, body, 0)
    _hgrn_finish(o_scr[...], zb, gho_ref, avg_ref, out_ref)

    @pl.when(tb == pl.num_programs(1) - 1)
    def _final():
        for p in range(st_scr.shape[0]):
            sfin_ref[0, p] = _compact_state(st_scr[p])


def _hgrn_sample_kernel(h_ref, wb_ref, lb_ref, gho_ref, avg_ref, s0_ref, out_ref, sfin_ref,
                        st_scr, g_scr, k_scr, q_scr, v_scr, o_scr, *, t, fdim, vdim):
    zb = _hgrn_project(h_ref, wb_ref, lb_ref, g_scr, k_scr, q_scr, v_scr, fdim, vdim)

    def body(ri, carry):
        for p in range(st_scr.shape[0]):
            st_scr[p] = _expand_state(s0_ref[ri, p])
        rows = pl.ds(pl.multiple_of(ri * t, t), t)
        _hgrn_chunk(q_scr[rows, :], k_scr[rows, :], g_scr[rows, :], v_scr[rows, :], st_scr, o_scr, rows,
                    sub=t)
        for p in range(st_scr.shape[0]):
            sfin_ref[ri, p] = _compact_state(st_scr[p])
        return carry

    lax.fori_loop(0, s0_ref.shape[0], body, 0)
    _hgrn_finish(o_scr[...], zb, gho_ref, avg_ref, out_ref)


def _xattn_heads(q, zc, head_k, head_v, heads, hdim):
    outs = []
    all_scores = [lax.dot_general(q[:, h * hdim:(h + 1) * hdim], head_k(h), NT, preferred_element_type=F32)
                  for h in range(heads)]
    for h, s in enumerate(all_scores):
        mv = head_v(h)
        s = s - jnp.max(s, axis=1, keepdims=True)
        p = jnp.exp(s)
        p = (p / jnp.sum(p, axis=1, keepdims=True)).astype(mv.dtype)
        outs.append(_dot(p, mv))
    return jnp.concatenate(outs, axis=1) * _silu(zc)


def _xattn_prompt_kernel(h_ref, wc_ref, mk_ref, mv_ref, out_ref, *, heads, hdim, scale):
    z = _dot(h_ref[0], wc_ref[...])
    w = heads * hdim
    q = (z[:, :w] * scale).astype(BF16)
    mk = mk_ref[0].astype(BF16)
    mv = mv_ref[0].astype(BF16)
    out_ref[0] = _xattn_heads(q, z[:, w:], lambda h: mk[:, h * hdim:(h + 1) * hdim],
                              lambda h: mv[:, h * hdim:(h + 1) * hdim], heads, hdim).astype(out_ref.dtype)


def _xattn_sample_kernel(h_ref, wc_ref, mk_ref, mv_ref, out_ref, *, heads, hdim, scale, t):
    z = _dot(h_ref[...], wc_ref[...])
    w = heads * hdim
    q = z[:, :w] * scale
    zc = z[:, w:]
    n_rows = mk_ref.shape[1]
    row_head = lax.broadcasted_iota(jnp.int32, (heads * t, n_rows), 0) // t
    col_head = lax.broadcasted_iota(jnp.int32, (heads * t, n_rows), 1) % heads
    all_scores = []
    for ri in range(mk_ref.shape[0]):
        qs = jnp.concatenate([q[ri * t:(ri + 1) * t, h * hdim:(h + 1) * hdim] for h in range(heads)], axis=0)
        all_scores.append(lax.dot_general(qs.astype(BF16), mk_ref[ri].astype(BF16), NT,
                                          preferred_element_type=F32))
    for ri, s in enumerate(all_scores):
        rows = slice(ri * t, (ri + 1) * t)
        s = jnp.where(row_head == col_head, s, -jnp.inf)
        p = jnp.exp(s - jnp.max(s, axis=1, keepdims=True))
        p = (p / jnp.sum(p, axis=1, keepdims=True)).astype(BF16)
        o = _dot(p, mv_ref[ri].astype(BF16))
        o = jnp.concatenate([o[h * t:(h + 1) * t] for h in range(heads)], axis=1)
        out_ref[rows, :] = (o * _silu(zc[rows])).astype(out_ref.dtype)


def _memkv_kernel(m_ref, g_ref, w_ref, k_ref, v_ref):
    kv = _dot(_rms(m_ref[...], g_ref[...]).astype(BF16), w_ref[...])
    w = kv.shape[1] // 2
    k_ref[...] = kv[:, :w]
    v_ref[...] = kv[:, w:]


def _merge_kernel(x_ref, h_ref, oa_ref, ob_ref, oc_ref, wuv_ref, wm_ref, wbr_ref, wout_ref, gf_ref, y_ref, *,
                  width, from_latent):
    h = h_ref[...]
    d = x_ref.shape[1]
    oa = oa_ref[...]
    if from_latent:
        oa = _dot(oa.astype(BF16), wuv_ref[...])
    else:
        oa = oa.astype(F32)
    za = _dot(h, wm_ref[:, :width])
    branches = ((oa * _silu(za)).astype(BF16), ob_ref[...].astype(BF16), oc_ref[...].astype(BF16))
    merged = jnp.zeros((x_ref.shape[0], d), F32)
    for n, o in enumerate(branches):
        gate = jax.nn.sigmoid(_dot(h, wm_ref[:, width + n * d:width + (n + 1) * d]))
        merged = merged + gate * _dot(o, wbr_ref[n])
    out = x_ref[...] + _dot(merged.astype(BF16), wout_ref[...])
    y_ref[...] = _rms(out, gf_ref[...])


def _rope_tables(pos, rope, lead):
    half = rope // 2
    inv = jnp.exp(-math.log(ROPE_BASE) * jnp.arange(half, dtype=F32) / half)
    ang = pos.astype(F32)[:, None] * inv[None, :]
    n = pos.shape[0]
    pad = jnp.zeros((n, LANES - lead - rope), F32)
    cos = jnp.concatenate([jnp.ones((n, lead), F32), jnp.cos(ang), jnp.cos(ang), pad], axis=1)
    sin = jnp.concatenate([jnp.zeros((n, lead), F32), jnp.sin(ang), jnp.sin(ang), pad], axis=1)
    return cos, sin


def _swap_halves(w):
    half = w.shape[-1] // 2
    return jnp.concatenate([-w[..., half:], w[..., :half]], axis=-1)


def _pad_cols(w, width):
    return jnp.pad(w, [(0, 0)] * (w.ndim - 1) + [(0, width - w.shape[-1])])


def _block_diag(blocks):
    rows = sum(b.shape[0] for b in blocks)
    cols = sum(b.shape[1] for b in blocks)
    out = jnp.zeros((rows, cols), blocks[0].dtype)
    r = c = 0
    for b in blocks:
        out = lax.dynamic_update_slice(out, b, (r, c))
        r += b.shape[0]
        c += b.shape[1]
    return out


def _tile_rows(n, pref):
    t = min(n, pref)
    assert n % t == 0
    return t


def kernel(x_prompt, x_sample, mem_prompt, cache_kv_latent, cache_k_rope, page_table, state_hgrn, cache_mem_k, cache_mem_v, g_norm, w_in, g_q_lora, w_uq, g_kv_lora, w_uk, w_uv, lb_logits, g_hgrn_out, g_mem_norm, w_mem_kv, w_branch, w_out, g_final):
    depth = w_in.shape[0]
    assert depth == 1, "one layer per step"
    b, t, d = x_prompt.shape
    nb, ts, _ = x_sample.shape
    n_mem = mem_prompt.shape[1]
    n_phys, page, kv_lora = cache_kv_latent.shape[1:]
    rope = cache_k_rope.shape[-1]
    n_pages = page_table.shape[1]
    q_lora = g_q_lora.shape[-1]
    a_heads, a_qk = w_uq.shape[2:]
    a_nope = w_uk.shape[-1]
    a_vdim = w_uv.shape[-1]
    b_heads, b_expand, b_vdim = state_hgrn.shape[2:]
    fdim = b_heads * b_expand
    c_heads, c_hdim = cache_mem_k.shape[3:]
    width = w_branch.shape[2]
    assert a_qk == a_nope + rope and a_heads * a_vdim == width and b_heads * b_vdim == width
    assert c_heads * c_hdim == width and a_qk <= LANES and 2 * a_vdim == LANES and 2 * b_vdim == LANES
    assert b_expand == LANES and a_heads % 2 == 0 and b_heads % 2 == 0

    splits = (q_lora, kv_lora, rope, width, fdim, fdim, width, width, width, width, N_BRANCH * d)
    offs = np.concatenate([[0], np.cumsum(splits)])
    assert offs[-1] == w_in.shape[-1]
    w_in0 = w_in[0]
    col = lambda i: w_in0[:, offs[i]:offs[i + 1]]

    a_scale = a_qk ** -0.5
    n_p = b * t
    n_s = nb * ts
    xp = x_prompt.reshape(n_p, d)
    xs = x_sample.reshape(n_s, d)
    row = lambda g: g.reshape(1, -1).astype(F32)

    w_a = jnp.concatenate([col(0), col(1), _pad_cols(col(2), LANES), _pad_cols(_swap_halves(col(2)), LANES)],
                          axis=1).astype(BF16)
    uq = w_uq[0]
    uq_nope, uq_rope = uq[..., :a_nope], uq[..., a_nope:]
    zero_nope = jnp.zeros_like(uq_nope)
    wq_prompt = jnp.concatenate([
        _pad_cols(uq, LANES).reshape(q_lora, -1),
        _pad_cols(jnp.concatenate([zero_nope, _swap_halves(uq_rope)], axis=-1), LANES).reshape(q_lora, -1),
    ], axis=1).astype(BF16)
    wq_sample = jnp.concatenate([
        uq_nope.reshape(q_lora, -1),
        _pad_cols(uq_rope, LANES).reshape(q_lora, -1),
        _pad_cols(_swap_halves(uq_rope), LANES).reshape(q_lora, -1),
    ], axis=1).astype(BF16)
    uk, uv = w_uk[0], w_uv[0]
    place = jnp.zeros((LANES, a_heads, LANES), F32)
    place = place.at[jnp.arange(rope)[:, None], jnp.arange(a_heads)[None, :],
                     a_nope + jnp.arange(rope)[:, None]].set(1.0)
    wk = jnp.concatenate([_pad_cols(uk, LANES).reshape(kv_lora, -1), place.reshape(LANES, -1)], axis=0)
    uv_pad = jnp.stack([jnp.pad(uv[:, h], ((0, 0), ((h % 2) * a_vdim, LANES - a_vdim - (h % 2) * a_vdim)))
                        for h in range(a_heads)], axis=1)
    wv = jnp.concatenate([uv_pad.reshape(kv_lora, -1), jnp.zeros((LANES, a_heads * LANES), F32)], axis=0)
    w_kv = jnp.concatenate([wk, wv], axis=1).astype(BF16)
    v_one = jnp.zeros((a_heads, LANES), F32).at[
        jnp.arange(a_heads), jnp.asarray([_sum_lane(h) for h in range(a_heads)])].set(1.0).reshape(1, -1)
    w_abs = _block_diag([uk[:, h].T for h in range(a_heads)]).astype(BF16)
    w_uv_bd = _block_diag([uv[:, h] for h in range(a_heads)]).astype(BF16)
    w_b = jnp.concatenate([col(4), col(5), col(6), col(7)], axis=1).astype(BF16)
    w_c = jnp.concatenate([col(8), col(9)], axis=1).astype(BF16)
    w_m = jnp.concatenate([col(3), col(10)], axis=1).astype(BF16)
    w_br = w_branch[0].astype(BF16)
    w_o = w_out[0].astype(BF16)
    w_mem = w_mem_kv[0].astype(BF16)
    gho = jnp.tile(g_hgrn_out[0], b_heads).reshape(1, width).astype(F32)
    head_of = jnp.arange(width) // b_vdim
    avg = ((head_of[:, None] == head_of[None, :]).astype(F32) / b_vdim).astype(BF16)
    lb2 = lb_logits[:2].astype(F32)

    n_past = n_pages * page
    cq_p, sq_p = _rope_tables(jnp.arange(t), rope, a_nope)
    ck_p, sk_p = _rope_tables(jnp.arange(t), rope, 0)
    tm_s = _tile_rows(n_s, 256)
    assert tm_s % ts == 0
    ck_s, sk_s = _rope_tables(n_past + (jnp.arange(tm_s) % ts), rope, 0)

    tm = _tile_rows(t, 512)
    n_t = t // tm
    mla_common = dict(q_lora=q_lora, kv_lora=kv_lora, rope=rope)
    rowspec = lambda w: pl.BlockSpec((tm, w), lambda i: (i, 0))
    tabspec = pl.BlockSpec((tm, LANES), lambda i: (i % n_t, 0))
    hq = a_heads * LANES
    h_p, q_p, k_p, v_p, ckv_p, kpe_p = pl.pallas_call(
        functools.partial(_mla_prep_prompt_kernel, scale=a_scale * math.log2(math.e), **mla_common),
        grid=(n_p // tm,),
        in_specs=[rowspec(d), _const_spec((1, d)), _const_spec(w_a.shape), _const_spec((1, q_lora)),
                  _const_spec(wq_prompt.shape), _const_spec((1, kv_lora)), _const_spec(w_kv.shape),
                  _const_spec((1, hq)), tabspec, tabspec, tabspec, tabspec],
        out_specs=[rowspec(d), rowspec(hq), rowspec(hq), rowspec(hq), rowspec(kv_lora), rowspec(rope)],
        out_shape=[jax.ShapeDtypeStruct((n_p, d), BF16), jax.ShapeDtypeStruct((n_p, hq), BF16),
                   jax.ShapeDtypeStruct((n_p, hq), BF16), jax.ShapeDtypeStruct((n_p, hq), BF16),
                   jax.ShapeDtypeStruct((n_p, kv_lora), F32), jax.ShapeDtypeStruct((n_p, rope), F32)],
        compiler_params=_params(("parallel",)),
    )(xp, row(g_norm), w_a, row(g_q_lora), wq_prompt, row(g_kv_lora), w_kv, v_one, cq_p, sq_p, ck_p, sk_p)

    srow = lambda w: pl.BlockSpec((tm_s, w), lambda i: (i, 0))
    hl = a_heads * kv_lora
    h_s, qlat_s, qrope_s, ckv_s, kpe_s = pl.pallas_call(
        functools.partial(_mla_prep_sample_kernel, scale=a_scale, nope_w=a_heads * a_nope, **mla_common),
        grid=(n_s // tm_s,),
        in_specs=[srow(d), _const_spec((1, d)), _const_spec(w_a.shape), _const_spec((1, q_lora)),
                  _const_spec(wq_sample.shape), _const_spec((1, kv_lora)), _const_spec(w_abs.shape),
                  _const_spec((tm_s, LANES)), _const_spec((tm_s, LANES))],
        out_specs=[srow(d), srow(hl), srow(hq), srow(kv_lora), srow(rope)],
        out_shape=[jax.ShapeDtypeStruct((n_s, d), BF16), jax.ShapeDtypeStruct((n_s, hl), F32),
                   jax.ShapeDtypeStruct((n_s, hq), F32), jax.ShapeDtypeStruct((n_s, kv_lora), F32),
                   jax.ShapeDtypeStruct((n_s, rope), F32)],
        compiler_params=_params(("parallel",)),
    )(xs, row(g_norm), w_a, row(g_q_lora), wq_sample, row(g_kv_lora), w_abs, ck_s, sk_s)

    tq = _tile_rows(t, 512)
    nq = t // tq
    pairs = [(i, j) for i in range(nq) for j in range(i + 1)]
    i_tab = jnp.asarray([p[0] for p in pairs], jnp.int32)
    j_tab = jnp.asarray([p[1] for p in pairs], jnp.int32)
    qkv = lambda a: a.reshape(b, t, hq)
    oa_p = pl.pallas_call(
        functools.partial(_flash_kernel, heads=a_heads),
        grid_spec=pltpu.PrefetchScalarGridSpec(
            num_scalar_prefetch=2, grid=(b, len(pairs)),
            in_specs=[pl.BlockSpec((1, tq, hq), lambda bi, s, it, jt: (bi, it[s], 0)),
                      pl.BlockSpec((1, tq, hq), lambda bi, s, it, jt: (bi, jt[s], 0)),
                      pl.BlockSpec((1, tq, hq), lambda bi, s, it, jt: (bi, jt[s], 0))],
            out_specs=pl.BlockSpec((1, tq, width), lambda bi, s, it, jt: (bi, it[s], 0)),
            scratch_shapes=[pltpu.VMEM((a_heads, tq, LANES), F32)] * 2),
        out_shape=jax.ShapeDtypeStruct((b, t, width), BF16),
        compiler_params=_params(("parallel", "arbitrary")),
    )(i_tab, j_tab, qkv(q_p), qkv(k_p), qkv(v_p)).reshape(n_p, width)

    assert n_pages % PAGED_SLOTS == 0
    group = math.gcd(n_pages // PAGED_SLOTS, 32)
    n_chunks = n_pages // group
    block_keys = math.gcd(group, 8) * page
    req = lambda w: pl.BlockSpec((ts, w), lambda r, pt: (r, 0))
    olat_s = pl.pallas_call(
        functools.partial(_paged_kernel, heads=a_heads, group=group, page=page, n_chunks=n_chunks,
                          block_keys=block_keys,
                          kv_lora=kv_lora, rope=rope),
        grid_spec=pltpu.PrefetchScalarGridSpec(
            num_scalar_prefetch=1, grid=(nb,),
            in_specs=[req(hl), req(hq), req(kv_lora), req(rope),
                      pl.BlockSpec(memory_space=pl.ANY), pl.BlockSpec(memory_space=pl.ANY)],
            out_specs=req(hl),
            scratch_shapes=[pltpu.VMEM((group * page, kv_lora), F32)] * PAGED_SLOTS
            + [pltpu.VMEM((group, rope, page), F32)] * PAGED_SLOTS
            + [pltpu.SemaphoreType.DMA((2, PAGED_SLOTS))]),
        out_shape=jax.ShapeDtypeStruct((n_s, hl), F32),
        compiler_params=_params(("arbitrary",)),
    )(page_table, qlat_s, qrope_s, ckv_s, kpe_s, cache_kv_latent.reshape(n_phys, page, kv_lora),
      jnp.swapaxes(cache_k_rope, 2, 3).reshape(n_phys, rope, page))

    n_pairs = b_heads // 2
    tb = _tile_rows(t, 256)
    chunk = _tile_rows(tb, 64)
    sub = _tile_rows(chunk, 16)
    wb_cols = w_b.shape[1]
    hgrn_scratch = lambda rows: [pltpu.VMEM((n_pairs, LANES, 2 * b_expand), F32),
                                 pltpu.VMEM((rows, fdim), F32), pltpu.VMEM((rows, fdim), F32),
                                 pltpu.VMEM((rows, fdim), F32), pltpu.VMEM((rows, width), F32),
                                 pltpu.VMEM((rows, width), F32)]
    ob_p, st_p = pl.pallas_call(
        functools.partial(_hgrn_prompt_kernel, chunk=chunk, sub=sub, fdim=fdim, vdim=width),
        grid=(b, t // tb),
        in_specs=[pl.BlockSpec((tb, d), lambda bi, ti: (bi * (t // tb) + ti, 0)),
                  _const_spec((d, wb_cols)), _const_spec((2, fdim)), _const_spec((1, width)),
                  _const_spec((width, width))],
        out_specs=[pl.BlockSpec((tb, width), lambda bi, ti: (bi * (t // tb) + ti, 0)),
                   pl.BlockSpec((1, n_pairs, LANES, b_expand), lambda bi, ti: (bi, 0, 0, 0))],
        out_shape=[jax.ShapeDtypeStruct((n_p, width), BF16),
                   jax.ShapeDtypeStruct((b, n_pairs, LANES, b_expand), F32)],
        scratch_shapes=hgrn_scratch(tb),
        compiler_params=_params(("parallel", "arbitrary")),
    )(h_p, w_b, lb2, gho, avg)

    def to_pairs(s):
        n = s.shape[0]
        return s.reshape(n, n_pairs, 2, b_expand, b_vdim).transpose(0, 1, 2, 4, 3).reshape(
            n, n_pairs, LANES, b_expand)

    def from_pairs(s):
        n = s.shape[0]
        return s.reshape(n, n_pairs, 2, b_vdim, b_expand).transpose(0, 1, 2, 4, 3).reshape(
            n, b_heads, b_expand, b_vdim)

    rq = _tile_rows(nb, 8)
    ob_s, st_s = pl.pallas_call(
        functools.partial(_hgrn_sample_kernel, t=ts, fdim=fdim, vdim=width),
        grid=(nb // rq,),
        in_specs=[pl.BlockSpec((rq * ts, d), lambda i: (i, 0)),
                  _const_spec((d, wb_cols)), _const_spec((2, fdim)), _const_spec((1, width)),
                  _const_spec((width, width)),
                  pl.BlockSpec((rq, n_pairs, LANES, b_expand), lambda i: (i, 0, 0, 0))],
        out_specs=[pl.BlockSpec((rq * ts, width), lambda i: (i, 0)),
                   pl.BlockSpec((rq, n_pairs, LANES, b_expand), lambda i: (i, 0, 0, 0))],
        out_shape=[jax.ShapeDtypeStruct((n_s, width), BF16),
                   jax.ShapeDtypeStruct((nb, n_pairs, LANES, b_expand), F32)],
        scratch_shapes=hgrn_scratch(rq * ts),
        compiler_params=_params(("parallel",)),
    )(h_s, w_b, lb2, gho, avg, to_pairs(state_hgrn.reshape(nb, b_heads, b_expand, b_vdim)))

    n_m = b * n_mem
    tmm = _tile_rows(n_m, 512)
    mk_p, mv_p = pl.pallas_call(
        _memkv_kernel,
        grid=(n_m // tmm,),
        in_specs=[pl.BlockSpec((tmm, d), lambda i: (i, 0)), _const_spec((1, d)), _const_spec(w_mem.shape)],
        out_specs=[pl.BlockSpec((tmm, width), lambda i: (i, 0)), pl.BlockSpec((tmm, width), lambda i: (i, 0))],
        out_shape=[jax.ShapeDtypeStruct((n_m, width), F32), jax.ShapeDtypeStruct((n_m, width), F32)],
        compiler_params=_params(("parallel",)),
    )(mem_prompt.reshape(n_m, d), row(g_mem_norm), w_mem)

    c_scale = c_hdim ** -0.5
    tx = _tile_rows(t, 512)
    memspec = pl.BlockSpec((1, n_mem, width), lambda bi, ti: (bi, 0, 0))
    oc_p = pl.pallas_call(
        functools.partial(_xattn_prompt_kernel, heads=c_heads, hdim=c_hdim, scale=c_scale),
        grid=(b, t // tx),
        in_specs=[pl.BlockSpec((1, tx, d), lambda bi, ti: (bi, ti, 0)), _const_spec(w_c.shape), memspec, memspec],
        out_specs=pl.BlockSpec((1, tx, width), lambda bi, ti: (bi, ti, 0)),
        out_shape=jax.ShapeDtypeStruct((b, t, width), BF16),
        compiler_params=_params(("parallel", "parallel")),
    )(h_p.reshape(b, t, d), w_c, mk_p.reshape(b, n_mem, width), mv_p.reshape(b, n_mem, width)).reshape(n_p, width)

    rx = _tile_rows(nb, 8)
    smem = pl.BlockSpec((rx, n_mem * c_heads, c_hdim), lambda i: (i, 0, 0))
    oc_s = pl.pallas_call(
        functools.partial(_xattn_sample_kernel, heads=c_heads, hdim=c_hdim, scale=c_scale, t=ts),
        grid=(nb // rx,),
        in_specs=[pl.BlockSpec((rx * ts, d), lambda i: (i, 0)), _const_spec(w_c.shape), smem, smem],
        out_specs=pl.BlockSpec((rx * ts, width), lambda i: (i, 0)),
        out_shape=jax.ShapeDtypeStruct((n_s, width), F32),
        compiler_params=_params(("parallel",)),
    )(h_s, w_c, cache_mem_k.reshape(nb, n_mem * c_heads, c_hdim), cache_mem_v.reshape(nb, n_mem * c_heads, c_hdim))

    def merge(x2, h2, oa, ob, oc, from_latent):
        n = x2.shape[0]
        tmg = _tile_rows(n, 512)
        rs = lambda w: pl.BlockSpec((tmg, w), lambda i: (i, 0))
        return pl.pallas_call(
            functools.partial(_merge_kernel, width=width, from_latent=from_latent),
            grid=(n // tmg,),
            in_specs=[rs(d), rs(d), rs(oa.shape[1]), rs(width), rs(width), _const_spec(w_uv_bd.shape),
                      _const_spec(w_m.shape), _const_spec(w_br.shape), _const_spec(w_o.shape),
                      _const_spec((1, d))],
            out_specs=rs(d),
            out_shape=jax.ShapeDtypeStruct((n, d), F32),
            compiler_params=_params(("parallel",)),
        )(x2, h2, oa, ob, oc, w_uv_bd, w_m, w_br, w_o, row(g_final))

    y_p = merge(xp, h_p, oa_p, ob_p, oc_p, False).reshape(b, t, d)
    y_s = merge(xs, h_s, olat_s, ob_s, oc_s, True).reshape(nb, ts, d)

    return (y_p, y_s,
            ckv_p.reshape(1, b, t, kv_lora), kpe_p.reshape(1, b, t, rope),
            from_pairs(st_p)[None],
            mk_p.reshape(1, b, n_mem, c_heads, c_hdim), mv_p.reshape(1, b, n_mem, c_heads, c_hdim),
            ckv_s.reshape(1, nb, ts, kv_lora), kpe_s.reshape(1, nb, ts, rope),
            from_pairs(st_s)[None])
```

```python
import functools
import math

import numpy as np
import jax
import jax.numpy as jnp
from jax import lax
from jax.experimental import pallas as pl
from jax.experimental.pallas import tpu as pltpu

F32 = jnp.float32
BF16 = jnp.bfloat16
EPS = 1e-6
ROPE_BASE = 10000.0
N_BRANCH = 3
LANES = 128
SUBLANES = 8
BF16_ROWS = 16
PAGED_SLOTS = 4
PAGED_AHEAD = 3
VMEM_LIMIT = 56 * 1024 * 1024
NT = (((1,), (1,)), ((), ()))
TN = (((0,), (0,)), ((), ()))
NN = (((1,), (0,)), ((), ()))


def _params(sem):
    return pltpu.CompilerParams(dimension_semantics=sem, vmem_limit_bytes=VMEM_LIMIT)


def _rms(x, g):
    return x * lax.rsqrt(jnp.mean(x * x, axis=-1, keepdims=True) + EPS) * g


def _silu(x):
    return x * jax.nn.sigmoid(x)


def _dot(a, b):
    return jnp.dot(a, b, preferred_element_type=F32)


def _const_spec(shape):
    nd = len(shape)
    return pl.BlockSpec(shape, lambda *_: (0,) * nd)


def _mla_common(x_ref, gn_ref, wa_ref, gq_ref, gkv_ref, ck_ref, sk_ref, h_ref, ckv_ref, kpe_ref,
                q_lora, kv_lora, rope):
    h = _rms(x_ref[...], gn_ref[...]).astype(BF16)
    h_ref[...] = h
    z = _dot(h, wa_ref[...])
    cq_n = _rms(z[:, :q_lora], gq_ref[...]).astype(BF16)
    ckv = _rms(z[:, q_lora:q_lora + kv_lora], gkv_ref[...])
    ckv_ref[...] = ckv
    o = q_lora + kv_lora
    kpe = z[:, o:o + LANES] * ck_ref[...] + z[:, o + LANES:o + 2 * LANES] * sk_ref[...]
    kpe_ref[...] = kpe[:, :rope]
    return cq_n, ckv, kpe


def _mla_prep_prompt_kernel(x_ref, gn_ref, wa_ref, gq_ref, wq_ref, gkv_ref, wk_ref, wvt_ref, cq_ref, sq_ref,
                            ck_ref, sk_ref, h_ref, q_ref, k_ref, vt_ref, ckv_ref, kpe_ref, *,
                            scale, q_lora, kv_lora, rope):
    cq_n, ckv, kpe = _mla_common(x_ref, gn_ref, wa_ref, gq_ref, gkv_ref, ck_ref, sk_ref, h_ref,
                                 ckv_ref, kpe_ref, q_lora, kv_lora, rope)
    q2 = _dot(cq_n, wq_ref[...])
    w = q2.shape[1] // 2
    reps = w // LANES
    cos = jnp.tile(cq_ref[...], (1, reps))
    sin = jnp.tile(sq_ref[...], (1, reps))
    q_ref[...] = ((q2[:, :w] * cos + q2[:, w:] * sin) * scale).astype(BF16)
    ckv_bf = ckv.astype(BF16)
    k_ref[...] = _dot(jnp.concatenate([ckv_bf, kpe.astype(BF16)], axis=1), wk_ref[...]).astype(BF16)
    vt = lax.dot_general(wvt_ref[...], ckv_bf, NT, preferred_element_type=F32)
    r = lax.broadcasted_iota(jnp.int32, vt.shape, 0)
    sum_row = jnp.where((r // LANES) % 2 == 0, _sum_row(0), _sum_row(1))
    vt_ref[0] = jnp.where(r % LANES == sum_row, 1.0, vt).astype(BF16)


def _mla_prep_sample_kernel(x_ref, gn_ref, wa_ref, gq_ref, wq_ref, gkv_ref, wabs_ref, ck_ref, sk_ref,
                            h_ref, qlat_ref, qrope_ref, ckv_ref, kpe_ref, *,
                            scale, q_lora, kv_lora, rope, nope_w):
    cq_n, _, _ = _mla_common(x_ref, gn_ref, wa_ref, gq_ref, gkv_ref, ck_ref, sk_ref, h_ref,
                             ckv_ref, kpe_ref, q_lora, kv_lora, rope)
    q2 = _dot(cq_n, wq_ref[...])
    q_nope = (q2[:, :nope_w] * scale).astype(BF16)
    qlat_ref[...] = _dot(q_nope, wabs_ref[...])
    w = (q2.shape[1] - nope_w) // 2
    reps = w // LANES
    cos = jnp.tile(ck_ref[...], (1, reps))
    sin = jnp.tile(sk_ref[...], (1, reps))
    qrope_ref[...] = (q2[:, nope_w:nope_w + w] * cos + q2[:, nope_w + w:] * sin) * scale


def _sum_row(head):
    return 0 if head % 2 else LANES - 1


def _flash_kernel(it_ref, jt_ref, q_ref, k_ref, vt_ref, o_ref, m_scr, acc_scr, *, heads):
    step = pl.program_id(1)
    i = it_ref[step]
    j = jt_ref[step]
    tq = q_ref.shape[1]

    @pl.when(j == 0)
    def _init():
        m_scr[...] = jnp.full(m_scr.shape, -jnp.inf, F32)
        acc_scr[...] = jnp.zeros(acc_scr.shape, F32)

    def update(masked):
        if masked:
            key = lax.broadcasted_iota(jnp.int32, (tq, tq), 0)
            query = lax.broadcasted_iota(jnp.int32, (tq, tq), 1)
            keep = key <= query
        all_scores = [lax.dot_general(k_ref[0, :, h * LANES:(h + 1) * LANES], q_ref[0, :, h * LANES:(h + 1) * LANES],
                                      NT, preferred_element_type=F32) for h in range(heads)]
        for h, s in enumerate(all_scores):
            if masked:
                s = jnp.where(keep, s, -jnp.inf)
            m_prev = m_scr[h]
            m_new = jnp.maximum(m_prev, jnp.max(s, axis=0, keepdims=True))
            p = jnp.exp2(s - m_new[0:1, :])
            acc_scr[h] = (jnp.exp2(m_prev - m_new)[0:1, :] * acc_scr[h]
                          + _dot(vt_ref[0, h * LANES:(h + 1) * LANES, :], p.astype(BF16)))
            m_scr[h] = m_new

    @pl.when(j < i)
    def _off_diagonal():
        update(False)

    @pl.when(j == i)
    def _diagonal():
        update(True)
        top_half = lax.broadcasted_iota(jnp.int32, (LANES, tq), 0) < LANES // 2
        for p in range(heads // 2):
            even, odd = acc_scr[2 * p], acc_scr[2 * p + 1]
            l_even = even[_sum_row(0):_sum_row(0) + 1, :]
            l_odd = odd[_sum_row(1):_sum_row(1) + 1, :]
            o_t = jnp.where(top_half, even / l_even, odd / l_odd)
            o_ref[0, :, p * LANES:(p + 1) * LANES] = o_t.T.astype(o_ref.dtype)


def _paged_kernel(pt_ref, qlat_ref, qrope_ref, ckv_ref, kpe_ref, lat_hbm, rope_hbm, o_ref,
                  *scratch, heads, group, page, n_chunks, block_keys, kv_lora, rope):
    r = pl.program_id(0)
    n_req = pl.num_programs(0)
    t = qlat_ref.shape[0]
    lat_bufs = scratch[:PAGED_SLOTS]
    rope_bufs = scratch[PAGED_SLOTS:2 * PAGED_SLOTS]
    sem = scratch[2 * PAGED_SLOTS]

    def page_copies(req, chunk, slot, g):
        pg = pt_ref[req, chunk * group + g]
        rows = pl.ds(g * page, page)
        return (pltpu.make_async_copy(lat_hbm.at[pg], lat_bufs[slot].at[rows], sem.at[0, slot]),
                pltpu.make_async_copy(rope_hbm.at[pg], rope_bufs[slot].at[g], sem.at[1, slot]))

    def start_chunk(req, chunk, slot):
        for g in range(group):
            for cp in page_copies(req, chunk, slot, g):
                cp.start()

    def wait_chunk(req, chunk, slot):
        for g in range(group):
            for cp in page_copies(req, chunk, slot, g):
                cp.wait()

    @pl.when(r == 0)
    def _prime():
        for c in range(PAGED_AHEAD):
            start_chunk(0, c, c)

    q_lat = jnp.concatenate([qlat_ref[:, h * kv_lora:(h + 1) * kv_lora] for h in range(heads)], axis=0)
    q_rope = jnp.concatenate([qrope_ref[:, h * LANES:(h + 1) * LANES] for h in range(heads)],
                             axis=0)[:, :rope]

    def scores(lat, rp, rp_dims):
        return (lax.dot_general(q_lat.astype(lat.dtype), lat, NT, preferred_element_type=F32)
                + lax.dot_general(q_rope.astype(rp.dtype), rp, rp_dims, preferred_element_type=F32))

    def partial_softmax(s, lat):
        m = jnp.max(s, axis=1, keepdims=True)
        p = jnp.exp(s - m)
        return m, jnp.sum(p, axis=1, keepdims=True), _dot(p.astype(lat.dtype), lat)

    def combine(carry, parts):
        m_prev, l_prev, acc = carry
        m_new = m_prev
        for m, _, _ in parts:
            m_new = jnp.maximum(m_new, m)
        alpha = jnp.exp(m_prev - m_new)
        l_new, acc = alpha * l_prev, alpha * acc
        for m, l, a in parts:
            w = jnp.exp(m - m_new)
            l_new, acc = l_new + w * l, acc + w * a
        return m_new, l_new, acc

    def chunk_step(c, slot, carry):
        wait_chunk(r, c, slot)
        wrap = jnp.where(c + PAGED_AHEAD >= n_chunks, 1, 0)
        start_chunk(jnp.minimum(r + wrap, n_req - 1), c + PAGED_AHEAD - wrap * n_chunks,
                    (slot + PAGED_AHEAD) % PAGED_SLOTS)
        pages = block_keys // page
        lat_block = lambda blk: lat_bufs[slot][blk * block_keys:(blk + 1) * block_keys].astype(BF16)
        all_scores = []
        for blk in range(group // pages):
            rope_t = jnp.concatenate([rope_bufs[slot][blk * pages + g] for g in range(pages)], axis=1)
            all_scores.append(scores(lat_block(blk), rope_t.astype(BF16), NN))
        return combine(carry, [partial_softmax(s, lat_block(blk)) for blk, s in enumerate(all_scores)])

    def body(it, carry):
        for slot in range(PAGED_SLOTS):
            carry = chunk_step(PAGED_SLOTS * it + slot, slot, carry)
        return carry

    rows = heads * t
    init = (jnp.full((rows, 1), -jnp.inf, F32), jnp.zeros((rows, 1), F32), jnp.zeros((rows, kv_lora), F32))
    carry = lax.fori_loop(0, n_chunks // PAGED_SLOTS, body, init)

    @pl.when(r == n_req - 1)
    def _drain():
        for c in range(PAGED_AHEAD):
            wait_chunk(r, c, c)

    lat_new = ckv_ref[...]
    s_new = scores(lat_new, kpe_ref[...], NT)
    tok = lax.broadcasted_iota(jnp.int32, (rows, t), 0) % t
    key = lax.broadcasted_iota(jnp.int32, (rows, t), 1)
    s_new = jnp.where(key <= tok, s_new, -jnp.inf)
    _, l_fin, acc = combine(carry, [partial_softmax(s_new, lat_new)])
    o = acc / l_fin
    for h in range(heads):
        o_ref[:, h * kv_lora:(h + 1) * kv_lora] = o[h * t:(h + 1) * t].astype(o_ref.dtype)


def _hgrn_project(h_ref, wb_ref, lb_ref, g_scr, k_scr, q_scr, v_scr, fdim, vdim):
    z = _dot(h_ref[...], wb_ref[...])
    l0 = lb_ref[0:1, :]
    l1 = lb_ref[1:2, :]
    mx = jnp.maximum(l0, l1)
    e0 = jnp.exp(l0 - mx)
    e1 = jnp.exp(l1 - mx)
    lb = e1 / (e0 + e1)
    f = lb + (1.0 - lb) * jax.nn.sigmoid(z[:, :fdim])
    g_scr[...] = jnp.log(f)
    k_scr[...] = 1.0 - f
    q_scr[...] = _silu(z[:, fdim:2 * fdim])
    v_scr[...] = z[:, 2 * fdim:2 * fdim + vdim]
    return z[:, 2 * fdim + vdim:]


def _cumsum_rows(x):
    row = lax.broadcasted_iota(jnp.int32, x.shape, 0)
    shift = 1
    while shift < x.shape[0]:
        x = x + jnp.where(row >= shift, pltpu.roll(x, shift, 0), 0.0)
        shift *= 2
    return x


def _hgrn_chunk(qq, kk, g, v, st_ref, o_ref, rows, *, sub):
    c = qq.shape[0]
    n_pairs = st_ref.shape[0]
    hv = st_ref.shape[1] // 2
    hk = st_ref.shape[2] // 2
    mm = BF16 if sub % BF16_ROWS == 0 else F32
    bcum = _cumsum_rows(g)
    last = bcum[c - 1:c, :]
    q_in = (qq * jnp.exp(bcum)).astype(mm)
    k_dec = (kk * jnp.exp(last - bcum)).astype(mm)
    v_mm = v.astype(mm)
    n_sub = c // sub
    row_blk = lax.broadcasted_iota(jnp.int32, (c, hk), 0) // sub
    causal = lax.broadcasted_iota(jnp.int32, (c, c), 0) >= lax.broadcasted_iota(jnp.int32, (c, c), 1)
    lane_v = lax.broadcasted_iota(jnp.int32, (1, 2 * hv), 1) // hv
    bd = (lax.broadcasted_iota(jnp.int32, (2 * hv, 2 * hk), 0) // hv
          == lax.broadcasted_iota(jnp.int32, (2 * hv, 2 * hk), 1) // hk)
    def head_scores(hl):
        b_h, q_h = bcum[:, hl], qq[:, hl]
        refs = [b_h[j * sub + sub // 2:j * sub + sub // 2 + 1] for j in range(n_sub)]
        ref_rows = jnp.concatenate([jnp.broadcast_to(rj, (sub, hk)) for rj in refs], axis=0)
        k_all = kk[:, hl] * jnp.exp(ref_rows - b_h)
        k_cat = jnp.concatenate([jnp.where(row_blk == j, k_all, 0.0).astype(mm) for j in range(n_sub)], axis=1)
        q_cat = jnp.concatenate(
            [jnp.concatenate([jnp.zeros((j * sub, hk), F32)] * (j > 0)
                             + [q_h[j * sub:] * jnp.exp(b_h[j * sub:] - refs[j])], axis=0).astype(mm)
             for j in range(n_sub)], axis=1)
        att = lax.dot_general(q_cat, k_cat, NT, preferred_element_type=F32)
        return jnp.where(causal, att, 0.0).astype(mm)

    atts = [head_scores(slice(h * hk, (h + 1) * hk)) for h in range(2 * n_pairs)]
    for p in range(n_pairs):
        pk = slice(p * 2 * hk, (p + 1) * 2 * hk)
        pv = slice(p * 2 * hv, (p + 1) * 2 * hv)
        vp = v_mm[:, pv]
        st = st_ref[p]
        o = lax.dot_general(q_in[:, pk], st.astype(mm), NT, preferred_element_type=F32)
        for e in range(2):
            o = o + _dot(atts[2 * p + e], jnp.where(lane_v == e, vp, jnp.zeros_like(vp)))
        o_ref[rows, pv] = o
        upd = lax.dot_general(vp, k_dec[:, pk], TN, preferred_element_type=F32)
        st_ref[p] = st * jnp.exp(last[:, pk]) + jnp.where(bd, upd, 0.0)


def _hgrn_finish(o, zb, gho_ref, avg_ref, out_ref):
    ms = _dot((o * o).astype(BF16), avg_ref[...])
    out_ref[...] = (o * lax.rsqrt(ms + EPS) * gho_ref[...] * _silu(zb)).astype(out_ref.dtype)


def _compact_state(st):
    hv = st.shape[0] // 2
    hk = st.shape[1] // 2
    return jnp.concatenate([st[:hv, :hk], st[hv:, hk:]], axis=0)


def _expand_state(sc):
    hv = sc.shape[0] // 2
    z = jnp.zeros((hv, sc.shape[1]), F32)
    return jnp.concatenate([jnp.concatenate([sc[:hv], z], axis=1),
                            jnp.concatenate([z, sc[hv:]], axis=1)], axis=0)


def _hgrn_prompt_kernel(h_ref, wb_ref, lb_ref, gho_ref, avg_ref, out_ref, sfin_ref,
                        st_scr, g_scr, k_scr, q_scr, v_scr, o_scr, *, chunk, sub, fdim, vdim):
    tb = pl.program_id(1)

    @pl.when(tb == 0)
    def _init():
        st_scr[...] = jnp.zeros(st_scr.shape, F32)

    zb = _hgrn_project(h_ref, wb_ref, lb_ref, g_scr, k_scr, q_scr, v_scr, fdim, vdim)

    def body(ci, carry):
        rows = pl.ds(pl.multiple_of(ci * chunk, chunk), chunk)
        _hgrn_chunk(q_scr[rows, :], k_scr[rows, :], g_scr[rows, :], v_scr[rows, :], st_scr, o_scr, rows,
                    sub=sub)
        return carry

    lax.fori_loop(0, h_ref.shape[0] // chunk, body, 0, unroll=True)
    _hgrn_finish(o_scr[...], zb, gho_ref, avg_ref, out_ref)

    @pl.when(tb == pl.num_programs(1) - 1)
    def _final():
        for p in range(st_scr.shape[0]):
            sfin_ref[0, p] = _compact_state(st_scr[p])


def _hgrn_sample_kernel(h_ref, wb_ref, lb_ref, gho_ref, avg_ref, s0_ref, out_ref, sfin_ref,
                        st_scr, g_scr, k_scr, q_scr, v_scr, o_scr, *, t, fdim, vdim):
    zb = _hgrn_project(h_ref, wb_ref, lb_ref, g_scr, k_scr, q_scr, v_scr, fdim, vdim)

    def body(ri, carry):
        for p in range(st_scr.shape[0]):
            st_scr[p] = _expand_state(s0_ref[ri, p])
        rows = pl.ds(pl.multiple_of(ri * t, t), t)
        _hgrn_chunk(q_scr[rows, :], k_scr[rows, :], g_scr[rows, :], v_scr[rows, :], st_scr, o_scr, rows,
                    sub=t)
        for p in range(st_scr.shape[0]):
            sfin_ref[ri, p] = _compact_state(st_scr[p])
        return carry

    lax.fori_loop(0, s0_ref.shape[0], body, 0)
    _hgrn_finish(o_scr[...], zb, gho_ref, avg_ref, out_ref)


def _xattn_heads(q, zc, head_k, head_v, heads, hdim):
    outs = []
    all_scores = [lax.dot_general(q[:, h * hdim:(h + 1) * hdim], head_k(h), NT, preferred_element_type=F32)
                  for h in range(heads)]
    for h, s in enumerate(all_scores):
        mv = head_v(h)
        s = s - jnp.max(s, axis=1, keepdims=True)
        p = jnp.exp(s)
        p = (p / jnp.sum(p, axis=1, keepdims=True)).astype(mv.dtype)
        outs.append(_dot(p, mv))
    return jnp.concatenate(outs, axis=1) * _silu(zc)


def _xattn_prompt_kernel(h_ref, wc_ref, mk_ref, mv_ref, out_ref, *, heads, hdim, scale):
    z = _dot(h_ref[0], wc_ref[...])
    w = heads * hdim
    q = (z[:, :w] * scale).astype(BF16)
    mk = mk_ref[0].astype(BF16)
    mv = mv_ref[0].astype(BF16)
    out_ref[0] = _xattn_heads(q, z[:, w:], lambda h: mk[:, h * hdim:(h + 1) * hdim],
                              lambda h: mv[:, h * hdim:(h + 1) * hdim], heads, hdim).astype(out_ref.dtype)


def _xattn_sample_kernel(h_ref, wc_ref, mk_ref, mv_ref, out_ref, *, heads, hdim, scale, t):
    z = _dot(h_ref[...], wc_ref[...])
    w = heads * hdim
    q = z[:, :w] * scale
    zc = z[:, w:]
    n_rows = mk_ref.shape[1]
    row_head = lax.broadcasted_iota(jnp.int32, (heads * t, n_rows), 0) // t
    col_head = lax.broadcasted_iota(jnp.int32, (heads * t, n_rows), 1) % heads
    all_scores = []
    for ri in range(mk_ref.shape[0]):
        qs = jnp.concatenate([q[ri * t:(ri + 1) * t, h * hdim:(h + 1) * hdim] for h in range(heads)], axis=0)
        all_scores.append(lax.dot_general(qs.astype(BF16), mk_ref[ri].astype(BF16), NT,
                                          preferred_element_type=F32))
    for ri, s in enumerate(all_scores):
        rows = slice(ri * t, (ri + 1) * t)
        s = jnp.where(row_head == col_head, s, -jnp.inf)
        p = jnp.exp(s - jnp.max(s, axis=1, keepdims=True))
        p = (p / jnp.sum(p, axis=1, keepdims=True)).astype(BF16)
        o = _dot(p, mv_ref[ri].astype(BF16))
        o = jnp.concatenate([o[h * t:(h + 1) * t] for h in range(heads)], axis=1)
        out_ref[rows, :] = (o * _silu(zc[rows])).astype(out_ref.dtype)


def _memkv_kernel(m_ref, g_ref, w_ref, k_ref, v_ref):
    kv = _dot(_rms(m_ref[...], g_ref[...]).astype(BF16), w_ref[...])
    w = kv.shape[1] // 2
    k_ref[...] = kv[:, :w]
    v_ref[...] = kv[:, w:]


def _merge_kernel(x_ref, h_ref, oa_ref, ob_ref, oc_ref, wuv_ref, wm_ref, wbr_ref, wout_ref, gf_ref, y_ref, *,
                  width, from_latent):
    h = h_ref[...]
    d = x_ref.shape[1]
    oa = oa_ref[...]
    if from_latent:
        oa = _dot(oa.astype(BF16), wuv_ref[...])
    else:
        oa = oa.astype(F32)
    za = _dot(h, wm_ref[:, :width])
    branches = ((oa * _silu(za)).astype(BF16), ob_ref[...].astype(BF16), oc_ref[...].astype(BF16))
    merged = jnp.zeros((x_ref.shape[0], d), F32)
    for n, o in enumerate(branches):
        gate = jax.nn.sigmoid(_dot(h, wm_ref[:, width + n * d:width + (n + 1) * d]))
        merged = merged + gate * _dot(o, wbr_ref[n])
    out = x_ref[...] + _dot(merged.astype(BF16), wout_ref[...])
    y_ref[...] = _rms(out, gf_ref[...])


def _rope_tables(pos, rope, lead):
    half = rope // 2
    inv = jnp.exp(-math.log(ROPE_BASE) * jnp.arange(half, dtype=F32) / half)
    ang = pos.astype(F32)[:, None] * inv[None, :]
    n = pos.shape[0]
    pad = jnp.zeros((n, LANES - lead - rope), F32)
    cos = jnp.concatenate([jnp.ones((n, lead), F32), jnp.cos(ang), jnp.cos(ang), pad], axis=1)
    sin = jnp.concatenate([jnp.zeros((n, lead), F32), jnp.sin(ang), jnp.sin(ang), pad], axis=1)
    return cos, sin


def _swap_halves(w):
    half = w.shape[-1] // 2
    return jnp.concatenate([-w[..., half:], w[..., :half]], axis=-1)


def _pad_cols(w, width):
    return jnp.pad(w, [(0, 0)] * (w.ndim - 1) + [(0, width - w.shape[-1])])


def _block_diag(blocks):
    rows = sum(b.shape[0] for b in blocks)
    cols = sum(b.shape[1] for b in blocks)
    out = jnp.zeros((rows, cols), blocks[0].dtype)
    r = c = 0
    for b in blocks:
        out = lax.dynamic_update_slice(out, b, (r, c))
        r += b.shape[0]
        c += b.shape[1]
    return out


def _tile_rows(n, pref):
    t = min(n, pref)
    assert n % t == 0
    return t


def kernel(x_prompt, x_sample, mem_prompt, cache_kv_latent, cache_k_rope, page_table, state_hgrn, cache_mem_k, cache_mem_v, g_norm, w_in, g_q_lora, w_uq, g_kv_lora, w_uk, w_uv, lb_logits, g_hgrn_out, g_mem_norm, w_mem_kv, w_branch, w_out, g_final):
    depth = w_in.shape[0]
    assert depth == 1, "one layer per step"
    b, t, d = x_prompt.shape
    nb, ts, _ = x_sample.shape
    n_mem = mem_prompt.shape[1]
    n_phys, page, kv_lora = cache_kv_latent.shape[1:]
    rope = cache_k_rope.shape[-1]
    n_pages = page_table.shape[1]
    q_lora = g_q_lora.shape[-1]
    a_heads, a_qk = w_uq.shape[2:]
    a_nope = w_uk.shape[-1]
    a_vdim = w_uv.shape[-1]
    b_heads, b_expand, b_vdim = state_hgrn.shape[2:]
    fdim = b_heads * b_expand
    c_heads, c_hdim = cache_mem_k.shape[3:]
    width = w_branch.shape[2]
    assert a_qk == a_nope + rope and a_heads * a_vdim == width and b_heads * b_vdim == width
    assert c_heads * c_hdim == width and a_qk <= LANES and 2 * a_vdim == LANES and 2 * b_vdim == LANES
    assert b_expand == LANES and a_heads % 2 == 0 and b_heads % 2 == 0

    splits = (q_lora, kv_lora, rope, width, fdim, fdim, width, width, width, width, N_BRANCH * d)
    offs = np.concatenate([[0], np.cumsum(splits)])
    assert offs[-1] == w_in.shape[-1]
    w_in0 = w_in[0]
    col = lambda i: w_in0[:, offs[i]:offs[i + 1]]

    a_scale = a_qk ** -0.5
    n_p = b * t
    n_s = nb * ts
    xp = x_prompt.reshape(n_p, d)
    xs = x_sample.reshape(n_s, d)
    row = lambda g: g.reshape(1, -1).astype(F32)

    w_a = jnp.concatenate([col(0), col(1), _pad_cols(col(2), LANES), _pad_cols(_swap_halves(col(2)), LANES)],
                          axis=1).astype(BF16)
    uq = w_uq[0]
    uq_nope, uq_rope = uq[..., :a_nope], uq[..., a_nope:]
    zero_nope = jnp.zeros_like(uq_nope)
    wq_prompt = jnp.concatenate([
        _pad_cols(uq, LANES).reshape(q_lora, -1),
        _pad_cols(jnp.concatenate([zero_nope, _swap_halves(uq_rope)], axis=-1), LANES).reshape(q_lora, -1),
    ], axis=1).astype(BF16)
    wq_sample = jnp.concatenate([
        uq_nope.reshape(q_lora, -1),
        _pad_cols(uq_rope, LANES).reshape(q_lora, -1),
        _pad_cols(_swap_halves(uq_rope), LANES).reshape(q_lora, -1),
    ], axis=1).astype(BF16)
    uk, uv = w_uk[0], w_uv[0]
    place = jnp.zeros((LANES, a_heads, LANES), F32)
    place = place.at[jnp.arange(rope)[:, None], jnp.arange(a_heads)[None, :],
                     a_nope + jnp.arange(rope)[:, None]].set(1.0)
    w_k = jnp.concatenate([_pad_cols(uk, LANES).reshape(kv_lora, -1), place.reshape(LANES, -1)],
                          axis=0).astype(BF16)
    uv_pad = jnp.stack([jnp.pad(uv[:, h], ((0, 0), ((h % 2) * a_vdim, LANES - a_vdim - (h % 2) * a_vdim)))
                        for h in range(a_heads)], axis=1)
    w_vt = uv_pad.reshape(kv_lora, -1).T.astype(BF16)
    w_abs = _block_diag([uk[:, h].T for h in range(a_heads)]).astype(BF16)
    w_uv_bd = _block_diag([uv[:, h] for h in range(a_heads)]).astype(BF16)
    w_b = jnp.concatenate([col(4), col(5), col(6), col(7)], axis=1).astype(BF16)
    w_c = jnp.concatenate([col(8), col(9)], axis=1).astype(BF16)
    w_m = jnp.concatenate([col(3), col(10)], axis=1).astype(BF16)
    w_br = w_branch[0].astype(BF16)
    w_o = w_out[0].astype(BF16)
    w_mem = w_mem_kv[0].astype(BF16)
    gho = jnp.tile(g_hgrn_out[0], b_heads).reshape(1, width).astype(F32)
    head_of = jnp.arange(width) // b_vdim
    avg = ((head_of[:, None] == head_of[None, :]).astype(F32) / b_vdim).astype(BF16)
    lb2 = lb_logits[:2].astype(F32)

    n_past = n_pages * page
    cq_p, sq_p = _rope_tables(jnp.arange(t), rope, a_nope)
    ck_p, sk_p = _rope_tables(jnp.arange(t), rope, 0)
    tm_s = _tile_rows(n_s, 256)
    assert tm_s % ts == 0
    ck_s, sk_s = _rope_tables(n_past + (jnp.arange(tm_s) % ts), rope, 0)

    tm = _tile_rows(t, 512)
    n_t = t // tm
    mla_common = dict(q_lora=q_lora, kv_lora=kv_lora, rope=rope)
    rowspec = lambda w: pl.BlockSpec((tm, w), lambda i: (i, 0))
    tabspec = pl.BlockSpec((tm, LANES), lambda i: (i % n_t, 0))
    hq = a_heads * LANES
    h_p, q_p, k_p, vt_p, ckv_p, kpe_p = pl.pallas_call(
        functools.partial(_mla_prep_prompt_kernel, scale=a_scale * math.log2(math.e), **mla_common),
        grid=(n_p // tm,),
        in_specs=[rowspec(d), _const_spec((1, d)), _const_spec(w_a.shape), _const_spec((1, q_lora)),
                  _const_spec(wq_prompt.shape), _const_spec((1, kv_lora)), _const_spec(w_k.shape),
                  _const_spec(w_vt.shape), tabspec, tabspec, tabspec, tabspec],
        out_specs=[rowspec(d), rowspec(hq), rowspec(hq),
                   pl.BlockSpec((1, hq, tm), lambda i: (i // n_t, 0, i % n_t)),
                   rowspec(kv_lora), rowspec(rope)],
        out_shape=[jax.ShapeDtypeStruct((n_p, d), BF16), jax.ShapeDtypeStruct((n_p, hq), BF16),
                   jax.ShapeDtypeStruct((n_p, hq), BF16), jax.ShapeDtypeStruct((b, hq, t), BF16),
                   jax.ShapeDtypeStruct((n_p, kv_lora), F32), jax.ShapeDtypeStruct((n_p, rope), F32)],
        compiler_params=_params(("parallel",)),
    )(xp, row(g_norm), w_a, row(g_q_lora), wq_prompt, row(g_kv_lora), w_k, w_vt, cq_p, sq_p, ck_p, sk_p)

    srow = lambda w: pl.BlockSpec((tm_s, w), lambda i: (i, 0))
    hl = a_heads * kv_lora
    h_s, qlat_s, qrope_s, ckv_s, kpe_s = pl.pallas_call(
        functools.partial(_mla_prep_sample_kernel, scale=a_scale, nope_w=a_heads * a_nope, **mla_common),
        grid=(n_s // tm_s,),
        in_specs=[srow(d), _const_spec((1, d)), _const_spec(w_a.shape), _const_spec((1, q_lora)),
                  _const_spec(wq_sample.shape), _const_spec((1, kv_lora)), _const_spec(w_abs.shape),
                  _const_spec((tm_s, LANES)), _const_spec((tm_s, LANES))],
        out_specs=[srow(d), srow(hl), srow(hq), srow(kv_lora), srow(rope)],
        out_shape=[jax.ShapeDtypeStruct((n_s, d), BF16), jax.ShapeDtypeStruct((n_s, hl), F32),
                   jax.ShapeDtypeStruct((n_s, hq), F32), jax.ShapeDtypeStruct((n_s, kv_lora), F32),
                   jax.ShapeDtypeStruct((n_s, rope), F32)],
        compiler_params=_params(("parallel",)),
    )(xs, row(g_norm), w_a, row(g_q_lora), wq_sample, row(g_kv_lora), w_abs, ck_s, sk_s)

    tq = _tile_rows(t, 512)
    nq = t // tq
    pairs = [(i, j) for i in range(nq) for j in range(i + 1)]
    i_tab = jnp.asarray([p[0] for p in pairs], jnp.int32)
    j_tab = jnp.asarray([p[1] for p in pairs], jnp.int32)
    qkv = lambda a: a.reshape(b, t, hq)
    oa_p = pl.pallas_call(
        functools.partial(_flash_kernel, heads=a_heads),
        grid_spec=pltpu.PrefetchScalarGridSpec(
            num_scalar_prefetch=2, grid=(b, len(pairs)),
            in_specs=[pl.BlockSpec((1, tq, hq), lambda bi, s, it, jt: (bi, it[s], 0)),
                      pl.BlockSpec((1, tq, hq), lambda bi, s, it, jt: (bi, jt[s], 0)),
                      pl.BlockSpec((1, hq, tq), lambda bi, s, it, jt: (bi, 0, jt[s]))],
            out_specs=pl.BlockSpec((1, tq, width), lambda bi, s, it, jt: (bi, it[s], 0)),
            scratch_shapes=[pltpu.VMEM((a_heads, SUBLANES, tq), F32), pltpu.VMEM((a_heads, LANES, tq), F32)]),
        out_shape=jax.ShapeDtypeStruct((b, t, width), BF16),
        compiler_params=_params(("parallel", "arbitrary")),
    )(i_tab, j_tab, qkv(q_p), qkv(k_p), vt_p).reshape(n_p, width)

    assert n_pages % PAGED_SLOTS == 0
    group = math.gcd(n_pages // PAGED_SLOTS, 32)
    n_chunks = n_pages // group
    block_keys = math.gcd(group, 8) * page
    req = lambda w: pl.BlockSpec((ts, w), lambda r, pt: (r, 0))
    olat_s = pl.pallas_call(
        functools.partial(_paged_kernel, heads=a_heads, group=group, page=page, n_chunks=n_chunks,
                          block_keys=block_keys,
                          kv_lora=kv_lora, rope=rope),
        grid_spec=pltpu.PrefetchScalarGridSpec(
            num_scalar_prefetch=1, grid=(nb,),
            in_specs=[req(hl), req(hq), req(kv_lora), req(rope),
                      pl.BlockSpec(memory_space=pl.ANY), pl.BlockSpec(memory_space=pl.ANY)],
            out_specs=req(hl),
            scratch_shapes=[pltpu.VMEM((group * page, kv_lora), F32)] * PAGED_SLOTS
            + [pltpu.VMEM((group, rope, page), F32)] * PAGED_SLOTS
            + [pltpu.SemaphoreType.DMA((2, PAGED_SLOTS))]),
        out_shape=jax.ShapeDtypeStruct((n_s, hl), F32),
        compiler_params=_params(("arbitrary",)),
    )(page_table, qlat_s, qrope_s, ckv_s, kpe_s, cache_kv_latent.reshape(n_phys, page, kv_lora),
      jnp.swapaxes(cache_k_rope, 2, 3).reshape(n_phys, rope, page))

    n_pairs = b_heads // 2
    tb = _tile_rows(t, 256)
    chunk = _tile_rows(tb, 64)
    sub = _tile_rows(chunk, 16)
    wb_cols = w_b.shape[1]
    hgrn_scratch = lambda rows: [pltpu.VMEM((n_pairs, LANES, 2 * b_expand), F32),
                                 pltpu.VMEM((rows, fdim), F32), pltpu.VMEM((rows, fdim), F32),
                                 pltpu.VMEM((rows, fdim), F32), pltpu.VMEM((rows, width), F32),
                                 pltpu.VMEM((rows, width), F32)]
    ob_p, st_p = pl.pallas_call(
        functools.partial(_hgrn_prompt_kernel, chunk=chunk, sub=sub, fdim=fdim, vdim=width),
        grid=(b, t // tb),
        in_specs=[pl.BlockSpec((tb, d), lambda bi, ti: (bi * (t // tb) + ti, 0)),
                  _const_spec((d, wb_cols)), _const_spec((2, fdim)), _const_spec((1, width)),
                  _const_spec((width, width))],
        out_specs=[pl.BlockSpec((tb, width), lambda bi, ti: (bi * (t // tb) + ti, 0)),
                   pl.BlockSpec((1, n_pairs, LANES, b_expand), lambda bi, ti: (bi, 0, 0, 0))],
        out_shape=[jax.ShapeDtypeStruct((n_p, width), BF16),
                   jax.ShapeDtypeStruct((b, n_pairs, LANES, b_expand), F32)],
        scratch_shapes=hgrn_scratch(tb),
        compiler_params=_params(("parallel", "arbitrary")),
    )(h_p, w_b, lb2, gho, avg)

    def to_pairs(s):
        n = s.shape[0]
        return s.reshape(n, n_pairs, 2, b_expand, b_vdim).transpose(0, 1, 2, 4, 3).reshape(
            n, n_pairs, LANES, b_expand)

    def from_pairs(s):
        n = s.shape[0]
        return s.reshape(n, n_pairs, 2, b_vdim, b_expand).transpose(0, 1, 2, 4, 3).reshape(
            n, b_heads, b_expand, b_vdim)

    rq = _tile_rows(nb, 8)
    ob_s, st_s = pl.pallas_call(
        functools.partial(_hgrn_sample_kernel, t=ts, fdim=fdim, vdim=width),
        grid=(nb // rq,),
        in_specs=[pl.BlockSpec((rq * ts, d), lambda i: (i, 0)),
                  _const_spec((d, wb_cols)), _const_spec((2, fdim)), _const_spec((1, width)),
                  _const_spec((width, width)),
                  pl.BlockSpec((rq, n_pairs, LANES, b_expand), lambda i: (i, 0, 0, 0))],
        out_specs=[pl.BlockSpec((rq * ts, width), lambda i: (i, 0)),
                   pl.BlockSpec((rq, n_pairs, LANES, b_expand), lambda i: (i, 0, 0, 0))],
        out_shape=[jax.ShapeDtypeStruct((n_s, width), BF16),
                   jax.ShapeDtypeStruct((nb, n_pairs, LANES, b_expand), F32)],
        scratch_shapes=hgrn_scratch(rq * ts),
        compiler_params=_params(("parallel",)),
    )(h_s, w_b, lb2, gho, avg, to_pairs(state_hgrn.reshape(nb, b_heads, b_expand, b_vdim)))

    n_m = b * n_mem
    tmm = _tile_rows(n_m, 512)
    mk_p, mv_p = pl.pallas_call(
        _memkv_kernel,
        grid=(n_m // tmm,),
        in_specs=[pl.BlockSpec((tmm, d), lambda i: (i, 0)), _const_spec((1, d)), _const_spec(w_mem.shape)],
        out_specs=[pl.BlockSpec((tmm, width), lambda i: (i, 0)), pl.BlockSpec((tmm, width), lambda i: (i, 0))],
        out_shape=[jax.ShapeDtypeStruct((n_m, width), F32), jax.ShapeDtypeStruct((n_m, width), F32)],
        compiler_params=_params(("parallel",)),
    )(mem_prompt.reshape(n_m, d), row(g_mem_norm), w_mem)

    c_scale = c_hdim ** -0.5
    tx = _tile_rows(t, 512)
    memspec = pl.BlockSpec((1, n_mem, width), lambda bi, ti: (bi, 0, 0))
    oc_p = pl.pallas_call(
        functools.partial(_xattn_prompt_kernel, heads=c_heads, hdim=c_hdim, scale=c_scale),
        grid=(b, t // tx),
        in_specs=[pl.BlockSpec((1, tx, d), lambda bi, ti: (bi, ti, 0)), _const_spec(w_c.shape), memspec, memspec],
        out_specs=pl.BlockSpec((1, tx, width), lambda bi, ti: (bi, ti, 0)),
        out_shape=jax.ShapeDtypeStruct((b, t, width), BF16),
        compiler_params=_params(("parallel", "parallel")),
    )(h_p.reshape(b, t, d), w_c, mk_p.reshape(b, n_mem, width), mv_p.reshape(b, n_mem, width)).reshape(n_p, width)

    rx = _tile_rows(nb, 8)
    smem = pl.BlockSpec((rx, n_mem * c_heads, c_hdim), lambda i: (i, 0, 0))
    oc_s = pl.pallas_call(
        functools.partial(_xattn_sample_kernel, heads=c_heads, hdim=c_hdim, scale=c_scale, t=ts),
        grid=(nb // rx,),
        in_specs=[pl.BlockSpec((rx * ts, d), lambda i: (i, 0)), _const_spec(w_c.shape), smem, smem],
        out_specs=pl.BlockSpec((rx * ts, width), lambda i: (i, 0)),
        out_shape=jax.ShapeDtypeStruct((n_s, width), F32),
        compiler_params=_params(("parallel",)),
    )(h_s, w_c, cache_mem_k.reshape(nb, n_mem * c_heads, c_hdim), cache_mem_v.reshape(nb, n_mem * c_heads, c_hdim))

    def merge(x2, h2, oa, ob, oc, from_latent):
        n = x2.shape[0]
        tmg = _tile_rows(n, 512)
        rs = lambda w: pl.BlockSpec((tmg, w), lambda i: (i, 0))
        return pl.pallas_call(
            functools.partial(_merge_kernel, width=width, from_latent=from_latent),
            grid=(n // tmg,),
            in_specs=[rs(d), rs(d), rs(oa.shape[1]), rs(width), rs(width), _const_spec(w_uv_bd.shape),
                      _const_spec(w_m.shape), _const_spec(w_br.shape), _const_spec(w_o.shape),
                      _const_spec((1, d))],
            out_specs=rs(d),
            out_shape=jax.ShapeDtypeStruct((n, d), F32),
            compiler_params=_params(("parallel",)),
        )(x2, h2, oa, ob, oc, w_uv_bd, w_m, w_br, w_o, row(g_final))

    y_p = merge(xp, h_p, oa_p, ob_p, oc_p, False).reshape(b, t, d)
    y_s = merge(xs, h_s, olat_s, ob_s, oc_s, True).reshape(nb, ts, d)

    return (y_p, y_s,
            ckv_p.reshape(1, b, t, kv_lora), kpe_p.reshape(1, b, t, rope),
            from_pairs(st_p)[None],
            mk_p.reshape(1, b, n_mem, c_heads, c_hdim), mv_p.reshape(1, b, n_mem, c_heads, c_hdim),
            ckv_s.reshape(1, nb, ts, kv_lora), kpe_s.reshape(1, nb, ts, rope),
            from_pairs(st_s)[None])
```

```python
import functools
import math

import numpy as np
import jax
import jax.numpy as jnp
from jax import lax
from jax.experimental import pallas as pl
from jax.experimental.pallas import tpu as pltpu

F32 = jnp.float32
BF16 = jnp.bfloat16
EPS = 1e-6
ROPE_BASE = 10000.0
N_BRANCH = 3
LANES = 128
SUBLANES = 8
BF16_ROWS = 16
PAGED_SLOTS = 4
PAGED_AHEAD = 3
VMEM_LIMIT = 56 * 1024 * 1024
NT = (((1,), (1,)), ((), ()))
TN = (((0,), (0,)), ((), ()))
NN = (((1,), (0,)), ((), ()))


def _params(sem):
    return pltpu.CompilerParams(dimension_semantics=sem, vmem_limit_bytes=VMEM_LIMIT)


def _rms(x, g):
    return x * lax.rsqrt(jnp.mean(x * x, axis=-1, keepdims=True) + EPS) * g


def _silu(x):
    return x * jax.nn.sigmoid(x)


def _dot(a, b):
    return jnp.dot(a, b, preferred_element_type=F32)


def _const_spec(shape):
    nd = len(shape)
    return pl.BlockSpec(shape, lambda *_: (0,) * nd)


def _mla_common(x_ref, gn_ref, wa_ref, gq_ref, gkv_ref, ck_ref, sk_ref, h_ref, ckv_ref, kpe_ref,
                q_lora, kv_lora, rope):
    h = _rms(x_ref[...], gn_ref[...]).astype(BF16)
    h_ref[...] = h
    z = _dot(h, wa_ref[...])
    cq_n = _rms(z[:, :q_lora], gq_ref[...]).astype(BF16)
    ckv = _rms(z[:, q_lora:q_lora + kv_lora], gkv_ref[...])
    ckv_ref[...] = ckv
    o = q_lora + kv_lora
    kpe = z[:, o:o + LANES] * ck_ref[...] + z[:, o + LANES:o + 2 * LANES] * sk_ref[...]
    kpe_ref[...] = kpe[:, :rope]
    return cq_n, ckv, kpe


def _mla_prep_prompt_kernel(x_ref, gn_ref, wa_ref, gq_ref, wq_ref, gkv_ref, wk_ref, wvt_ref, cq_ref, sq_ref,
                            ck_ref, sk_ref, h_ref, q_ref, k_ref, vt_ref, ckv_ref, kpe_ref, *,
                            scale, q_lora, kv_lora, rope):
    cq_n, ckv, kpe = _mla_common(x_ref, gn_ref, wa_ref, gq_ref, gkv_ref, ck_ref, sk_ref, h_ref,
                                 ckv_ref, kpe_ref, q_lora, kv_lora, rope)
    q2 = _dot(cq_n, wq_ref[...])
    w = q2.shape[1] // 2
    reps = w // LANES
    cos = jnp.tile(cq_ref[...], (1, reps))
    sin = jnp.tile(sq_ref[...], (1, reps))
    q_ref[...] = ((q2[:, :w] * cos + q2[:, w:] * sin) * scale).astype(BF16)
    ckv_bf = ckv.astype(BF16)
    k_ref[...] = _dot(jnp.concatenate([ckv_bf, kpe.astype(BF16)], axis=1), wk_ref[...]).astype(BF16)
    vt = lax.dot_general(wvt_ref[...], ckv_bf, NT, preferred_element_type=F32)
    r = lax.broadcasted_iota(jnp.int32, vt.shape, 0)
    sum_row = jnp.where((r // LANES) % 2 == 0, _sum_row(0), _sum_row(1))
    vt_ref[0, 0] = jnp.where(r % LANES == sum_row, 1.0, vt).astype(BF16)


def _mla_prep_sample_kernel(x_ref, gn_ref, wa_ref, gq_ref, wq_ref, gkv_ref, wabs_ref, ck_ref, sk_ref,
                            h_ref, qlat_ref, qrope_ref, ckv_ref, kpe_ref, *,
                            scale, q_lora, kv_lora, rope, nope_w):
    cq_n, _, _ = _mla_common(x_ref, gn_ref, wa_ref, gq_ref, gkv_ref, ck_ref, sk_ref, h_ref,
                             ckv_ref, kpe_ref, q_lora, kv_lora, rope)
    q2 = _dot(cq_n, wq_ref[...])
    q_nope = (q2[:, :nope_w] * scale).astype(BF16)
    qlat_ref[...] = _dot(q_nope, wabs_ref[...])
    w = (q2.shape[1] - nope_w) // 2
    reps = w // LANES
    cos = jnp.tile(ck_ref[...], (1, reps))
    sin = jnp.tile(sk_ref[...], (1, reps))
    qrope_ref[...] = (q2[:, nope_w:nope_w + w] * cos + q2[:, nope_w + w:] * sin) * scale


def _sum_row(head):
    return 0 if head % 2 else LANES - 1


def _flash_kernel(it_ref, jt_ref, q_ref, k_ref, vt_ref, o_ref, m_scr, acc_scr, *, heads):
    step = pl.program_id(1)
    i = it_ref[step]
    j = jt_ref[step]
    tq = q_ref.shape[1]

    @pl.when(j == 0)
    def _init():
        m_scr[...] = jnp.full(m_scr.shape, -jnp.inf, F32)
        acc_scr[...] = jnp.zeros(acc_scr.shape, F32)

    def update(masked):
        if masked:
            key = lax.broadcasted_iota(jnp.int32, (tq, tq), 0)
            query = lax.broadcasted_iota(jnp.int32, (tq, tq), 1)
            keep = key <= query
        all_scores = [lax.dot_general(k_ref[0, :, h * LANES:(h + 1) * LANES], q_ref[0, :, h * LANES:(h + 1) * LANES],
                                      NT, preferred_element_type=F32) for h in range(heads)]
        for h, s in enumerate(all_scores):
            if masked:
                s = jnp.where(keep, s, -jnp.inf)
            m_prev = m_scr[h]
            m_new = jnp.maximum(m_prev, jnp.max(s, axis=0, keepdims=True))
            p = jnp.exp2(s - m_new[0:1, :])
            acc_scr[h] = (jnp.exp2(m_prev - m_new)[0:1, :] * acc_scr[h]
                          + _dot(vt_ref[0, 0, h * LANES:(h + 1) * LANES, :], p.astype(BF16)))
            m_scr[h] = m_new

    @pl.when(j < i)
    def _off_diagonal():
        update(False)

    @pl.when(j == i)
    def _diagonal():
        update(True)
        top_half = lax.broadcasted_iota(jnp.int32, (LANES, tq), 0) < LANES // 2
        for p in range(heads // 2):
            even, odd = acc_scr[2 * p], acc_scr[2 * p + 1]
            l_even = even[_sum_row(0):_sum_row(0) + 1, :]
            l_odd = odd[_sum_row(1):_sum_row(1) + 1, :]
            o_t = jnp.where(top_half, even / l_even, odd / l_odd)
            o_ref[0, :, p * LANES:(p + 1) * LANES] = o_t.T.astype(o_ref.dtype)


def _paged_kernel(pt_ref, qlat_ref, qrope_ref, ckv_ref, kpe_ref, lat_hbm, rope_hbm, o_ref,
                  *scratch, heads, group, page, n_chunks, block_keys, kv_lora, rope):
    r = pl.program_id(0)
    n_req = pl.num_programs(0)
    t = qlat_ref.shape[0]
    lat_bufs = scratch[:PAGED_SLOTS]
    rope_bufs = scratch[PAGED_SLOTS:2 * PAGED_SLOTS]
    sem = scratch[2 * PAGED_SLOTS]

    def page_copies(req, chunk, slot, g):
        pg = pt_ref[req, chunk * group + g]
        rows = pl.ds(g * page, page)
        return (pltpu.make_async_copy(lat_hbm.at[pg], lat_bufs[slot].at[rows], sem.at[0, slot]),
                pltpu.make_async_copy(rope_hbm.at[pg], rope_bufs[slot].at[g], sem.at[1, slot]))

    def start_chunk(req, chunk, slot):
        for g in range(group):
            for cp in page_copies(req, chunk, slot, g):
                cp.start()

    def wait_chunk(req, chunk, slot):
        for g in range(group):
            for cp in page_copies(req, chunk, slot, g):
                cp.wait()

    @pl.when(r == 0)
    def _prime():
        for c in range(PAGED_AHEAD):
            start_chunk(0, c, c)

    q_lat = jnp.concatenate([qlat_ref[:, h * kv_lora:(h + 1) * kv_lora] for h in range(heads)], axis=0)
    q_rope = jnp.concatenate([qrope_ref[:, h * LANES:(h + 1) * LANES] for h in range(heads)],
                             axis=0)[:, :rope]

    def scores(lat, rp, rp_dims):
        return (lax.dot_general(q_lat.astype(lat.dtype), lat, NT, preferred_element_type=F32)
                + lax.dot_general(q_rope.astype(rp.dtype), rp, rp_dims, preferred_element_type=F32))

    def partial_softmax(s, lat):
        m = jnp.max(s, axis=1, keepdims=True)
        p = jnp.exp(s - m)
        return m, jnp.sum(p, axis=1, keepdims=True), _dot(p.astype(lat.dtype), lat)

    def combine(carry, parts):
        m_prev, l_prev, acc = carry
        m_new = m_prev
        for m, _, _ in parts:
            m_new = jnp.maximum(m_new, m)
        alpha = jnp.exp(m_prev - m_new)
        l_new, acc = alpha * l_prev, alpha * acc
        for m, l, a in parts:
            w = jnp.exp(m - m_new)
            l_new, acc = l_new + w * l, acc + w * a
        return m_new, l_new, acc

    def chunk_step(c, slot, carry):
        wait_chunk(r, c, slot)
        wrap = jnp.where(c + PAGED_AHEAD >= n_chunks, 1, 0)
        start_chunk(jnp.minimum(r + wrap, n_req - 1), c + PAGED_AHEAD - wrap * n_chunks,
                    (slot + PAGED_AHEAD) % PAGED_SLOTS)
        pages = block_keys // page
        lat_block = lambda blk: lat_bufs[slot][blk * block_keys:(blk + 1) * block_keys].astype(BF16)
        all_scores = []
        for blk in range(group // pages):
            rope_t = jnp.concatenate([rope_bufs[slot][blk * pages + g] for g in range(pages)], axis=1)
            all_scores.append(scores(lat_block(blk), rope_t.astype(BF16), NN))
        return combine(carry, [partial_softmax(s, lat_block(blk)) for blk, s in enumerate(all_scores)])

    def body(it, carry):
        for slot in range(PAGED_SLOTS):
            carry = chunk_step(PAGED_SLOTS * it + slot, slot, carry)
        return carry

    rows = heads * t
    init = (jnp.full((rows, 1), -jnp.inf, F32), jnp.zeros((rows, 1), F32), jnp.zeros((rows, kv_lora), F32))
    carry = lax.fori_loop(0, n_chunks // PAGED_SLOTS, body, init)

    @pl.when(r == n_req - 1)
    def _drain():
        for c in range(PAGED_AHEAD):
            wait_chunk(r, c, c)

    lat_new = ckv_ref[...]
    s_new = scores(lat_new, kpe_ref[...], NT)
    tok = lax.broadcasted_iota(jnp.int32, (rows, t), 0) % t
    key = lax.broadcasted_iota(jnp.int32, (rows, t), 1)
    s_new = jnp.where(key <= tok, s_new, -jnp.inf)
    _, l_fin, acc = combine(carry, [partial_softmax(s_new, lat_new)])
    o = acc / l_fin
    for h in range(heads):
        o_ref[:, h * kv_lora:(h + 1) * kv_lora] = o[h * t:(h + 1) * t].astype(o_ref.dtype)


def _hgrn_project(h_ref, wb_ref, lb_ref, g_scr, k_scr, q_scr, v_scr, fdim, vdim):
    z = _dot(h_ref[...], wb_ref[...])
    l0 = lb_ref[0:1, :]
    l1 = lb_ref[1:2, :]
    mx = jnp.maximum(l0, l1)
    e0 = jnp.exp(l0 - mx)
    e1 = jnp.exp(l1 - mx)
    lb = e1 / (e0 + e1)
    f = lb + (1.0 - lb) * jax.nn.sigmoid(z[:, :fdim])
    g_scr[...] = jnp.log(f)
    k_scr[...] = 1.0 - f
    q_scr[...] = _silu(z[:, fdim:2 * fdim])
    v_scr[...] = z[:, 2 * fdim:2 * fdim + vdim]
    return z[:, 2 * fdim + vdim:]


def _cumsum_rows(x):
    row = lax.broadcasted_iota(jnp.int32, x.shape, 0)
    shift = 1
    while shift < x.shape[0]:
        x = x + jnp.where(row >= shift, pltpu.roll(x, shift, 0), 0.0)
        shift *= 2
    return x


def _hgrn_chunk(qq, kk, g, v, st_ref, o_ref, rows, *, sub):
    c = qq.shape[0]
    n_pairs = st_ref.shape[0]
    hv = st_ref.shape[1] // 2
    hk = st_ref.shape[2] // 2
    mm = BF16 if sub % BF16_ROWS == 0 else F32
    bcum = _cumsum_rows(g)
    last = bcum[c - 1:c, :]
    q_in = (qq * jnp.exp(bcum)).astype(mm)
    k_dec = (kk * jnp.exp(last - bcum)).astype(mm)
    v_mm = v.astype(mm)
    n_sub = c // sub
    row_blk = lax.broadcasted_iota(jnp.int32, (c, hk), 0) // sub
    causal = lax.broadcasted_iota(jnp.int32, (c, c), 0) >= lax.broadcasted_iota(jnp.int32, (c, c), 1)
    lane_v = lax.broadcasted_iota(jnp.int32, (1, 2 * hv), 1) // hv
    bd = (lax.broadcasted_iota(jnp.int32, (2 * hv, 2 * hk), 0) // hv
          == lax.broadcasted_iota(jnp.int32, (2 * hv, 2 * hk), 1) // hk)
    def head_scores(hl):
        b_h, q_h = bcum[:, hl], qq[:, hl]
        refs = [b_h[j * sub + sub // 2:j * sub + sub // 2 + 1] for j in range(n_sub)]
        ref_rows = jnp.concatenate([jnp.broadcast_to(rj, (sub, hk)) for rj in refs], axis=0)
        k_all = kk[:, hl] * jnp.exp(ref_rows - b_h)
        k_cat = jnp.concatenate([jnp.where(row_blk == j, k_all, 0.0).astype(mm) for j in range(n_sub)], axis=1)
        q_cat = jnp.concatenate(
            [jnp.concatenate([jnp.zeros((j * sub, hk), F32)] * (j > 0)
                             + [q_h[j * sub:] * jnp.exp(b_h[j * sub:] - refs[j])], axis=0).astype(mm)
             for j in range(n_sub)], axis=1)
        att = lax.dot_general(q_cat, k_cat, NT, preferred_element_type=F32)
        return jnp.where(causal, att, 0.0).astype(mm)

    atts = [head_scores(slice(h * hk, (h + 1) * hk)) for h in range(2 * n_pairs)]
    for p in range(n_pairs):
        pk = slice(p * 2 * hk, (p + 1) * 2 * hk)
        pv = slice(p * 2 * hv, (p + 1) * 2 * hv)
        vp = v_mm[:, pv]
        st = st_ref[p]
        o = lax.dot_general(q_in[:, pk], st.astype(mm), NT, preferred_element_type=F32)
        for e in range(2):
            o = o + _dot(atts[2 * p + e], jnp.where(lane_v == e, vp, jnp.zeros_like(vp)))
        o_ref[rows, pv] = o
        upd = lax.dot_general(vp, k_dec[:, pk], TN, preferred_element_type=F32)
        st_ref[p] = st * jnp.exp(last[:, pk]) + jnp.where(bd, upd, 0.0)


def _hgrn_finish(o, zb, gho_ref, avg_ref, out_ref):
    ms = _dot((o * o).astype(BF16), avg_ref[...])
    out_ref[...] = (o * lax.rsqrt(ms + EPS) * gho_ref[...] * _silu(zb)).astype(out_ref.dtype)


def _compact_state(st):
    hv = st.shape[0] // 2
    hk = st.shape[1] // 2
    return jnp.concatenate([st[:hv, :hk], st[hv:, hk:]], axis=0)


def _expand_state(sc):
    hv = sc.shape[0] // 2
    z = jnp.zeros((hv, sc.shape[1]), F32)
    return jnp.concatenate([jnp.concatenate([sc[:hv], z], axis=1),
                            jnp.concatenate([z, sc[hv:]], axis=1)], axis=0)


def _hgrn_prompt_kernel(h_ref, wb_ref, lb_ref, gho_ref, avg_ref, out_ref, sfin_ref,
                        st_scr, g_scr, k_scr, q_scr, v_scr, o_scr, *, chunk, sub, fdim, vdim):
    tb = pl.program_id(1)

    @pl.when(tb == 0)
    def _init():
        st_scr[...] = jnp.zeros(st_scr.shape, F32)

    zb = _hgrn_project(h_ref, wb_ref, lb_ref, g_scr, k_scr, q_scr, v_scr, fdim, vdim)

    def body(ci, carry):
        rows = pl.ds(pl.multiple_of(ci * chunk, chunk), chunk)
        _hgrn_chunk(q_scr[rows, :], k_scr[rows, :], g_scr[rows, :], v_scr[rows, :], st_scr, o_scr, rows,
                    sub=sub)
        return carry

    lax.fori_loop(0, h_ref.shape[0] // chunk, body, 0, unroll=True)
    _hgrn_finish(o_scr[...], zb, gho_ref, avg_ref, out_ref)

    @pl.when(tb == pl.num_programs(1) - 1)
    def _final():
        for p in range(st_scr.shape[0]):
            sfin_ref[0, p] = _compact_state(st_scr[p])


def _hgrn_sample_kernel(h_ref, wb_ref, lb_ref, gho_ref, avg_ref, s0_ref, out_ref, sfin_ref,
                        st_scr, g_scr, k_scr, q_scr, v_scr, o_scr, *, t, fdim, vdim):
    zb = _hgrn_project(h_ref, wb_ref, lb_ref, g_scr, k_scr, q_scr, v_scr, fdim, vdim)

    for ri in range(s0_ref.shape[0]):
        st_ref = st_scr.at[ri]
        for p in range(st_ref.shape[0]):
            st_ref[p] = _expand_state(s0_ref[ri, p])
        rows = pl.ds(ri * t, t)
        _hgrn_chunk(q_scr[rows, :], k_scr[rows, :], g_scr[rows, :], v_scr[rows, :], st_ref, o_scr, rows,
                    sub=t)
        for p in range(st_ref.shape[0]):
            sfin_ref[ri, p] = _compact_state(st_ref[p])
    _hgrn_finish(o_scr[...], zb, gho_ref, avg_ref, out_ref)


def _xattn_heads(q, zc, head_k, head_v, heads, hdim):
    outs = []
    all_scores = [lax.dot_general(q[:, h * hdim:(h + 1) * hdim], head_k(h), NT, preferred_element_type=F32)
                  for h in range(heads)]
    for h, s in enumerate(all_scores):
        mv = head_v(h)
        s = s - jnp.max(s, axis=1, keepdims=True)
        p = jnp.exp(s)
        p = (p / jnp.sum(p, axis=1, keepdims=True)).astype(mv.dtype)
        outs.append(_dot(p, mv))
    return jnp.concatenate(outs, axis=1) * _silu(zc)


def _xattn_prompt_kernel(h_ref, wc_ref, mk_ref, mv_ref, out_ref, *, heads, hdim, scale):
    z = _dot(h_ref[0], wc_ref[...])
    w = heads * hdim
    q = (z[:, :w] * scale).astype(BF16)
    mk = mk_ref[0].astype(BF16)
    mv = mv_ref[0].astype(BF16)
    out_ref[0] = _xattn_heads(q, z[:, w:], lambda h: mk[:, h * hdim:(h + 1) * hdim],
                              lambda h: mv[:, h * hdim:(h + 1) * hdim], heads, hdim).astype(out_ref.dtype)


def _xattn_sample_kernel(h_ref, wc_ref, mk_ref, mv_ref, out_ref, *, heads, hdim, scale, t):
    z = _dot(h_ref[...], wc_ref[...])
    w = heads * hdim
    q = z[:, :w] * scale
    zc = z[:, w:]
    n_rows = mk_ref.shape[1]
    row_head = lax.broadcasted_iota(jnp.int32, (heads * t, n_rows), 0) // t
    col_head = lax.broadcasted_iota(jnp.int32, (heads * t, n_rows), 1) % heads
    all_scores = []
    for ri in range(mk_ref.shape[0]):
        qs = jnp.concatenate([q[ri * t:(ri + 1) * t, h * hdim:(h + 1) * hdim] for h in range(heads)], axis=0)
        all_scores.append(lax.dot_general(qs.astype(BF16), mk_ref[ri].astype(BF16), NT,
                                          preferred_element_type=F32))
    for ri, s in enumerate(all_scores):
        rows = slice(ri * t, (ri + 1) * t)
        s = jnp.where(row_head == col_head, s, -jnp.inf)
        p = jnp.exp(s - jnp.max(s, axis=1, keepdims=True))
        p = (p / jnp.sum(p, axis=1, keepdims=True)).astype(BF16)
        o = _dot(p, mv_ref[ri].astype(BF16))
        o = jnp.concatenate([o[h * t:(h + 1) * t] for h in range(heads)], axis=1)
        out_ref[rows, :] = (o * _silu(zc[rows])).astype(out_ref.dtype)


def _memkv_kernel(m_ref, g_ref, w_ref, k_ref, v_ref):
    kv = _dot(_rms(m_ref[...], g_ref[...]).astype(BF16), w_ref[...])
    w = kv.shape[1] // 2
    k_ref[...] = kv[:, :w]
    v_ref[...] = kv[:, w:]


def _merge_kernel(x_ref, h_ref, oa_ref, ob_ref, oc_ref, wuv_ref, wm_ref, wbr_ref, wout_ref, gf_ref, y_ref, *,
                  width, from_latent):
    h = h_ref[...]
    d = x_ref.shape[1]
    oa = oa_ref[...]
    if from_latent:
        oa = _dot(oa.astype(BF16), wuv_ref[...])
    else:
        oa = oa.astype(F32)
    za = _dot(h, wm_ref[:, :width])
    branches = ((oa * _silu(za)).astype(BF16), ob_ref[...].astype(BF16), oc_ref[...].astype(BF16))
    merged = jnp.zeros((x_ref.shape[0], d), F32)
    for n, o in enumerate(branches):
        gate = jax.nn.sigmoid(_dot(h, wm_ref[:, width + n * d:width + (n + 1) * d]))
        merged = merged + gate * _dot(o, wbr_ref[n])
    out = x_ref[...] + _dot(merged.astype(BF16), wout_ref[...])
    y_ref[...] = _rms(out, gf_ref[...])


def _rope_tables(pos, rope, lead):
    half = rope // 2
    inv = jnp.exp(-math.log(ROPE_BASE) * jnp.arange(half, dtype=F32) / half)
    ang = pos.astype(F32)[:, None] * inv[None, :]
    n = pos.shape[0]
    pad = jnp.zeros((n, LANES - lead - rope), F32)
    cos = jnp.concatenate([jnp.ones((n, lead), F32), jnp.cos(ang), jnp.cos(ang), pad], axis=1)
    sin = jnp.concatenate([jnp.zeros((n, lead), F32), jnp.sin(ang), jnp.sin(ang), pad], axis=1)
    return cos, sin


def _swap_halves(w):
    half = w.shape[-1] // 2
    return jnp.concatenate([-w[..., half:], w[..., :half]], axis=-1)


def _pad_cols(w, width):
    return jnp.pad(w, [(0, 0)] * (w.ndim - 1) + [(0, width - w.shape[-1])])


def _block_diag(blocks):
    rows = sum(b.shape[0] for b in blocks)
    cols = sum(b.shape[1] for b in blocks)
    out = jnp.zeros((rows, cols), blocks[0].dtype)
    r = c = 0
    for b in blocks:
        out = lax.dynamic_update_slice(out, b, (r, c))
        r += b.shape[0]
        c += b.shape[1]
    return out


def _tile_rows(n, pref):
    t = min(n, pref)
    assert n % t == 0
    return t


def kernel(x_prompt, x_sample, mem_prompt, cache_kv_latent, cache_k_rope, page_table, state_hgrn, cache_mem_k, cache_mem_v, g_norm, w_in, g_q_lora, w_uq, g_kv_lora, w_uk, w_uv, lb_logits, g_hgrn_out, g_mem_norm, w_mem_kv, w_branch, w_out, g_final):
    depth = w_in.shape[0]
    assert depth == 1, "one layer per step"
    b, t, d = x_prompt.shape
    nb, ts, _ = x_sample.shape
    n_mem = mem_prompt.shape[1]
    n_phys, page, kv_lora = cache_kv_latent.shape[1:]
    rope = cache_k_rope.shape[-1]
    n_pages = page_table.shape[1]
    q_lora = g_q_lora.shape[-1]
    a_heads, a_qk = w_uq.shape[2:]
    a_nope = w_uk.shape[-1]
    a_vdim = w_uv.shape[-1]
    b_heads, b_expand, b_vdim = state_hgrn.shape[2:]
    fdim = b_heads * b_expand
    c_heads, c_hdim = cache_mem_k.shape[3:]
    width = w_branch.shape[2]
    assert a_qk == a_nope + rope and a_heads * a_vdim == width and b_heads * b_vdim == width
    assert c_heads * c_hdim == width and a_qk <= LANES and 2 * a_vdim == LANES and 2 * b_vdim == LANES
    assert b_expand == LANES and a_heads % 2 == 0 and b_heads % 2 == 0

    splits = (q_lora, kv_lora, rope, width, fdim, fdim, width, width, width, width, N_BRANCH * d)
    offs = np.concatenate([[0], np.cumsum(splits)])
    assert offs[-1] == w_in.shape[-1]
    w_in0 = w_in[0]
    col = lambda i: w_in0[:, offs[i]:offs[i + 1]]

    a_scale = a_qk ** -0.5
    n_p = b * t
    n_s = nb * ts
    xp = x_prompt.reshape(n_p, d)
    xs = x_sample.reshape(n_s, d)
    row = lambda g: g.reshape(1, -1).astype(F32)

    w_a = jnp.concatenate([col(0), col(1), _pad_cols(col(2), LANES), _pad_cols(_swap_halves(col(2)), LANES)],
                          axis=1).astype(BF16)
    uq = w_uq[0]
    uq_nope, uq_rope = uq[..., :a_nope], uq[..., a_nope:]
    zero_nope = jnp.zeros_like(uq_nope)
    wq_prompt = jnp.concatenate([
        _pad_cols(uq, LANES).reshape(q_lora, -1),
        _pad_cols(jnp.concatenate([zero_nope, _swap_halves(uq_rope)], axis=-1), LANES).reshape(q_lora, -1),
    ], axis=1).astype(BF16)
    wq_sample = jnp.concatenate([
        uq_nope.reshape(q_lora, -1),
        _pad_cols(uq_rope, LANES).reshape(q_lora, -1),
        _pad_cols(_swap_halves(uq_rope), LANES).reshape(q_lora, -1),
    ], axis=1).astype(BF16)
    uk, uv = w_uk[0], w_uv[0]
    place = jnp.zeros((LANES, a_heads, LANES), F32)
    place = place.at[jnp.arange(rope)[:, None], jnp.arange(a_heads)[None, :],
                     a_nope + jnp.arange(rope)[:, None]].set(1.0)
    w_k = jnp.concatenate([_pad_cols(uk, LANES).reshape(kv_lora, -1), place.reshape(LANES, -1)],
                          axis=0).astype(BF16)
    uv_pad = jnp.stack([jnp.pad(uv[:, h], ((0, 0), ((h % 2) * a_vdim, LANES - a_vdim - (h % 2) * a_vdim)))
                        for h in range(a_heads)], axis=1)
    w_vt = uv_pad.reshape(kv_lora, -1).T.astype(BF16)
    w_abs = _block_diag([uk[:, h].T for h in range(a_heads)]).astype(BF16)
    w_uv_bd = _block_diag([uv[:, h] for h in range(a_heads)]).astype(BF16)
    w_b = jnp.concatenate([col(4), col(5), col(6), col(7)], axis=1).astype(BF16)
    w_c = jnp.concatenate([col(8), col(9)], axis=1).astype(BF16)
    w_m = jnp.concatenate([col(3), col(10)], axis=1).astype(BF16)
    w_br = w_branch[0].astype(BF16)
    w_o = w_out[0].astype(BF16)
    w_mem = w_mem_kv[0].astype(BF16)
    gho = jnp.tile(g_hgrn_out[0], b_heads).reshape(1, width).astype(F32)
    head_of = jnp.arange(width) // b_vdim
    avg = ((head_of[:, None] == head_of[None, :]).astype(F32) / b_vdim).astype(BF16)
    lb2 = lb_logits[:2].astype(F32)

    n_past = n_pages * page
    cq_p, sq_p = _rope_tables(jnp.arange(t), rope, a_nope)
    ck_p, sk_p = _rope_tables(jnp.arange(t), rope, 0)
    tm_s = _tile_rows(n_s, 256)
    assert tm_s % ts == 0
    ck_s, sk_s = _rope_tables(n_past + (jnp.arange(tm_s) % ts), rope, 0)

    tm = _tile_rows(t, 512)
    n_t = t // tm
    mla_common = dict(q_lora=q_lora, kv_lora=kv_lora, rope=rope)
    rowspec = lambda w: pl.BlockSpec((tm, w), lambda i: (i, 0))
    tabspec = pl.BlockSpec((tm, LANES), lambda i: (i % n_t, 0))
    hq = a_heads * LANES
    h_p, q_p, k_p, vt_p, ckv_p, kpe_p = pl.pallas_call(
        functools.partial(_mla_prep_prompt_kernel, scale=a_scale * math.log2(math.e), **mla_common),
        grid=(n_p // tm,),
        in_specs=[rowspec(d), _const_spec((1, d)), _const_spec(w_a.shape), _const_spec((1, q_lora)),
                  _const_spec(wq_prompt.shape), _const_spec((1, kv_lora)), _const_spec(w_k.shape),
                  _const_spec(w_vt.shape), tabspec, tabspec, tabspec, tabspec],
        out_specs=[rowspec(d), rowspec(hq), rowspec(hq),
                   pl.BlockSpec((1, 1, hq, tm), lambda i: (i // n_t, i % n_t, 0, 0)),
                   rowspec(kv_lora), rowspec(rope)],
        out_shape=[jax.ShapeDtypeStruct((n_p, d), BF16), jax.ShapeDtypeStruct((n_p, hq), BF16),
                   jax.ShapeDtypeStruct((n_p, hq), BF16), jax.ShapeDtypeStruct((b, n_t, hq, tm), BF16),
                   jax.ShapeDtypeStruct((n_p, kv_lora), F32), jax.ShapeDtypeStruct((n_p, rope), F32)],
        compiler_params=_params(("parallel",)),
    )(xp, row(g_norm), w_a, row(g_q_lora), wq_prompt, row(g_kv_lora), w_k, w_vt, cq_p, sq_p, ck_p, sk_p)

    srow = lambda w: pl.BlockSpec((tm_s, w), lambda i: (i, 0))
    hl = a_heads * kv_lora
    h_s, qlat_s, qrope_s, ckv_s, kpe_s = pl.pallas_call(
        functools.partial(_mla_prep_sample_kernel, scale=a_scale, nope_w=a_heads * a_nope, **mla_common),
        grid=(n_s // tm_s,),
        in_specs=[srow(d), _const_spec((1, d)), _const_spec(w_a.shape), _const_spec((1, q_lora)),
                  _const_spec(wq_sample.shape), _const_spec((1, kv_lora)), _const_spec(w_abs.shape),
                  _const_spec((tm_s, LANES)), _const_spec((tm_s, LANES))],
        out_specs=[srow(d), srow(hl), srow(hq), srow(kv_lora), srow(rope)],
        out_shape=[jax.ShapeDtypeStruct((n_s, d), BF16), jax.ShapeDtypeStruct((n_s, hl), F32),
                   jax.ShapeDtypeStruct((n_s, hq), F32), jax.ShapeDtypeStruct((n_s, kv_lora), F32),
                   jax.ShapeDtypeStruct((n_s, rope), F32)],
        compiler_params=_params(("parallel",)),
    )(xs, row(g_norm), w_a, row(g_q_lora), wq_sample, row(g_kv_lora), w_abs, ck_s, sk_s)

    tq = tm
    nq = t // tq
    pairs = [(i, j) for i in range(nq) for j in range(i + 1)]
    i_tab = jnp.asarray([p[0] for p in pairs], jnp.int32)
    j_tab = jnp.asarray([p[1] for p in pairs], jnp.int32)
    qkv = lambda a: a.reshape(b, t, hq)
    oa_p = pl.pallas_call(
        functools.partial(_flash_kernel, heads=a_heads),
        grid_spec=pltpu.PrefetchScalarGridSpec(
            num_scalar_prefetch=2, grid=(b, len(pairs)),
            in_specs=[pl.BlockSpec((1, tq, hq), lambda bi, s, it, jt: (bi, it[s], 0)),
                      pl.BlockSpec((1, tq, hq), lambda bi, s, it, jt: (bi, jt[s], 0)),
                      pl.BlockSpec((1, 1, hq, tq), lambda bi, s, it, jt: (bi, jt[s], 0, 0))],
            out_specs=pl.BlockSpec((1, tq, width), lambda bi, s, it, jt: (bi, it[s], 0)),
            scratch_shapes=[pltpu.VMEM((a_heads, SUBLANES, tq), F32), pltpu.VMEM((a_heads, LANES, tq), F32)]),
        out_shape=jax.ShapeDtypeStruct((b, t, width), BF16),
        compiler_params=_params(("parallel", "arbitrary")),
    )(i_tab, j_tab, qkv(q_p), qkv(k_p), vt_p).reshape(n_p, width)

    assert n_pages % PAGED_SLOTS == 0
    group = math.gcd(n_pages // PAGED_SLOTS, 32)
    n_chunks = n_pages // group
    block_keys = math.gcd(group, 8) * page
    req = lambda w: pl.BlockSpec((ts, w), lambda r, pt: (r, 0))
    olat_s = pl.pallas_call(
        functools.partial(_paged_kernel, heads=a_heads, group=group, page=page, n_chunks=n_chunks,
                          block_keys=block_keys,
                          kv_lora=kv_lora, rope=rope),
        grid_spec=pltpu.PrefetchScalarGridSpec(
            num_scalar_prefetch=1, grid=(nb,),
            in_specs=[req(hl), req(hq), req(kv_lora), req(rope),
                      pl.BlockSpec(memory_space=pl.ANY), pl.BlockSpec(memory_space=pl.ANY)],
            out_specs=req(hl),
            scratch_shapes=[pltpu.VMEM((group * page, kv_lora), F32)] * PAGED_SLOTS
            + [pltpu.VMEM((group, rope, page), F32)] * PAGED_SLOTS
            + [pltpu.SemaphoreType.DMA((2, PAGED_SLOTS))]),
        out_shape=jax.ShapeDtypeStruct((n_s, hl), F32),
        compiler_params=_params(("arbitrary",)),
    )(page_table, qlat_s, qrope_s, ckv_s, kpe_s, cache_kv_latent.reshape(n_phys, page, kv_lora),
      jnp.swapaxes(cache_k_rope, 2, 3).reshape(n_phys, rope, page))

    n_pairs = b_heads // 2
    tb = _tile_rows(t, 256)
    chunk = _tile_rows(tb, 64)
    sub = _tile_rows(chunk, 16)
    wb_cols = w_b.shape[1]
    hgrn_scratch = lambda rows, *lead: [pltpu.VMEM(lead + (n_pairs, LANES, 2 * b_expand), F32),
                                 pltpu.VMEM((rows, fdim), F32), pltpu.VMEM((rows, fdim), F32),
                                 pltpu.VMEM((rows, fdim), F32), pltpu.VMEM((rows, width), F32),
                                 pltpu.VMEM((rows, width), F32)]
    ob_p, st_p = pl.pallas_call(
        functools.partial(_hgrn_prompt_kernel, chunk=chunk, sub=sub, fdim=fdim, vdim=width),
        grid=(b, t // tb),
        in_specs=[pl.BlockSpec((tb, d), lambda bi, ti: (bi * (t // tb) + ti, 0)),
                  _const_spec((d, wb_cols)), _const_spec((2, fdim)), _const_spec((1, width)),
                  _const_spec((width, width))],
        out_specs=[pl.BlockSpec((tb, width), lambda bi, ti: (bi * (t // tb) + ti, 0)),
                   pl.BlockSpec((1, n_pairs, LANES, b_expand), lambda bi, ti: (bi, 0, 0, 0))],
        out_shape=[jax.ShapeDtypeStruct((n_p, width), BF16),
                   jax.ShapeDtypeStruct((b, n_pairs, LANES, b_expand), F32)],
        scratch_shapes=hgrn_scratch(tb),
        compiler_params=_params(("parallel", "arbitrary")),
    )(h_p, w_b, lb2, gho, avg)

    def to_pairs(s):
        n = s.shape[0]
        return s.reshape(n, n_pairs, 2, b_expand, b_vdim).transpose(0, 1, 2, 4, 3).reshape(
            n, n_pairs, LANES, b_expand)

    def from_pairs(s):
        n = s.shape[0]
        return s.reshape(n, n_pairs, 2, b_vdim, b_expand).transpose(0, 1, 2, 4, 3).reshape(
            n, b_heads, b_expand, b_vdim)

    rq = _tile_rows(nb, 8)
    ob_s, st_s = pl.pallas_call(
        functools.partial(_hgrn_sample_kernel, t=ts, fdim=fdim, vdim=width),
        grid=(nb // rq,),
        in_specs=[pl.BlockSpec((rq * ts, d), lambda i: (i, 0)),
                  _const_spec((d, wb_cols)), _const_spec((2, fdim)), _const_spec((1, width)),
                  _const_spec((width, width)),
                  pl.BlockSpec((rq, n_pairs, LANES, b_expand), lambda i: (i, 0, 0, 0))],
        out_specs=[pl.BlockSpec((rq * ts, width), lambda i: (i, 0)),
                   pl.BlockSpec((rq, n_pairs, LANES, b_expand), lambda i: (i, 0, 0, 0))],
        out_shape=[jax.ShapeDtypeStruct((n_s, width), BF16),
                   jax.ShapeDtypeStruct((nb, n_pairs, LANES, b_expand), F32)],
        scratch_shapes=hgrn_scratch(rq * ts, rq),
        compiler_params=_params(("parallel",)),
    )(h_s, w_b, lb2, gho, avg, to_pairs(state_hgrn.reshape(nb, b_heads, b_expand, b_vdim)))

    n_m = b * n_mem
    tmm = _tile_rows(n_m, 512)
    mk_p, mv_p = pl.pallas_call(
        _memkv_kernel,
        grid=(n_m // tmm,),
        in_specs=[pl.BlockSpec((tmm, d), lambda i: (i, 0)), _const_spec((1, d)), _const_spec(w_mem.shape)],
        out_specs=[pl.BlockSpec((tmm, width), lambda i: (i, 0)), pl.BlockSpec((tmm, width), lambda i: (i, 0))],
        out_shape=[jax.ShapeDtypeStruct((n_m, width), F32), jax.ShapeDtypeStruct((n_m, width), F32)],
        compiler_params=_params(("parallel",)),
    )(mem_prompt.reshape(n_m, d), row(g_mem_norm), w_mem)

    c_scale = c_hdim ** -0.5
    tx = _tile_rows(t, 512)
    memspec = pl.BlockSpec((1, n_mem, width), lambda bi, ti: (bi, 0, 0))
    oc_p = pl.pallas_call(
        functools.partial(_xattn_prompt_kernel, heads=c_heads, hdim=c_hdim, scale=c_scale),
        grid=(b, t // tx),
        in_specs=[pl.BlockSpec((1, tx, d), lambda bi, ti: (bi, ti, 0)), _const_spec(w_c.shape), memspec, memspec],
        out_specs=pl.BlockSpec((1, tx, width), lambda bi, ti: (bi, ti, 0)),
        out_shape=jax.ShapeDtypeStruct((b, t, width), BF16),
        compiler_params=_params(("parallel", "parallel")),
    )(h_p.reshape(b, t, d), w_c, mk_p.reshape(b, n_mem, width), mv_p.reshape(b, n_mem, width)).reshape(n_p, width)

    rx = _tile_rows(nb, 8)
    smem = pl.BlockSpec((rx, n_mem * c_heads, c_hdim), lambda i: (i, 0, 0))
    oc_s = pl.pallas_call(
        functools.partial(_xattn_sample_kernel, heads=c_heads, hdim=c_hdim, scale=c_scale, t=ts),
        grid=(nb // rx,),
        in_specs=[pl.BlockSpec((rx * ts, d), lambda i: (i, 0)), _const_spec(w_c.shape), smem, smem],
        out_specs=pl.BlockSpec((rx * ts, width), lambda i: (i, 0)),
        out_shape=jax.ShapeDtypeStruct((n_s, width), F32),
        compiler_params=_params(("parallel",)),
    )(h_s, w_c, cache_mem_k.reshape(nb, n_mem * c_heads, c_hdim), cache_mem_v.reshape(nb, n_mem * c_heads, c_hdim))

    def merge(x2, h2, oa, ob, oc, from_latent):
        n = x2.shape[0]
        tmg = _tile_rows(n, 512)
        rs = lambda w: pl.BlockSpec((tmg, w), lambda i: (i, 0))
        return pl.pallas_call(
            functools.partial(_merge_kernel, width=width, from_latent=from_latent),
            grid=(n // tmg,),
            in_specs=[rs(d), rs(d), rs(oa.shape[1]), rs(width), rs(width), _const_spec(w_uv_bd.shape),
                      _const_spec(w_m.shape), _const_spec(w_br.shape), _const_spec(w_o.shape),
                      _const_spec((1, d))],
            out_specs=rs(d),
            out_shape=jax.ShapeDtypeStruct((n, d), F32),
            compiler_params=_params(("parallel",)),
        )(x2, h2, oa, ob, oc, w_uv_bd, w_m, w_br, w_o, row(g_final))

    y_p = merge(xp, h_p, oa_p, ob_p, oc_p, False).reshape(b, t, d)
    y_s = merge(xs, h_s, olat_s, ob_s, oc_s, True).reshape(nb, ts, d)

    return (y_p, y_s,
            ckv_p.reshape(1, b, t, kv_lora), kpe_p.reshape(1, b, t, rope),
            from_pairs(st_p)[None],
            mk_p.reshape(1, b, n_mem, c_heads, c_hdim), mv_p.reshape(1, b, n_mem, c_heads, c_hdim),
            ckv_s.reshape(1, nb, ts, kv_lora), kpe_s.reshape(1, nb, ts, rope),
            from_pairs(st_s)[None])
```

```python
import functools
import math

import numpy as np
import jax
import jax.numpy as jnp
from jax import lax
from jax.experimental import pallas as pl
from jax.experimental.pallas import tpu as pltpu

F32 = jnp.float32
BF16 = jnp.bfloat16
EPS = 1e-6
ROPE_BASE = 10000.0
N_BRANCH = 3
LANES = 128
BF16_ROWS = 16
PAGED_SLOTS = 4
PAGED_AHEAD = 3
VMEM_LIMIT = 56 * 1024 * 1024
NT = (((1,), (1,)), ((), ()))
TN = (((0,), (0,)), ((), ()))
NN = (((1,), (0,)), ((), ()))


def _params(sem):
    return pltpu.CompilerParams(dimension_semantics=sem, vmem_limit_bytes=VMEM_LIMIT)


def _rms(x, g):
    return x * lax.rsqrt(jnp.mean(x * x, axis=-1, keepdims=True) + EPS) * g


def _silu(x):
    return x * jax.nn.sigmoid(x)


def _dot(a, b):
    return jnp.dot(a, b, preferred_element_type=F32)


def _const_spec(shape):
    nd = len(shape)
    return pl.BlockSpec(shape, lambda *_: (0,) * nd)


def _rotate(x, cos, sin_signed, half):
    n = x.shape[1]
    first_half = (lax.broadcasted_iota(jnp.int32, x.shape, 1) // half) % 2 == 0
    partner = jnp.where(first_half, pltpu.roll(x, n - half, 1), pltpu.roll(x, half, 1))
    reps = n // cos.shape[1]
    return x * jnp.tile(cos, (1, reps)) + partner * jnp.tile(sin_signed, (1, reps))


def _mla_common(x_ref, gn_ref, wa_ref, gq_ref, gkv_ref, ck_ref, sk_ref, h_ref, ckv_ref, kpe_ref,
                q_lora, kv_lora, rope):
    h = _rms(x_ref[...], gn_ref[...]).astype(BF16)
    h_ref[...] = h
    z = _dot(h, wa_ref[...])
    cq_n = _rms(z[:, :q_lora], gq_ref[...]).astype(BF16)
    ckv = _rms(z[:, q_lora:q_lora + kv_lora], gkv_ref[...])
    ckv_ref[...] = ckv
    o = q_lora + kv_lora
    kpe = _rotate(z[:, o:o + LANES], ck_ref[...], sk_ref[...], rope // 2)
    kpe_ref[...] = kpe[:, :rope]
    return cq_n, ckv, kpe


def _mla_prep_prompt_kernel(x_ref, gn_ref, wa_ref, gq_ref, wq_ref, gkv_ref, wkv_ref, cq_ref, sq_ref,
                            ck_ref, sk_ref, h_ref, q_ref, k_ref, v_ref, ckv_ref, kpe_ref, *,
                            scale, q_lora, kv_lora, rope, nope):
    cq_n, ckv, kpe = _mla_common(x_ref, gn_ref, wa_ref, gq_ref, gkv_ref, ck_ref, sk_ref, h_ref,
                                 ckv_ref, kpe_ref, q_lora, kv_lora, rope)
    q2 = _dot(cq_n, wq_ref[...])
    w = q2.shape[1] // 2
    reps = w // LANES
    q_ref[...] = ((q2[:, :w] * jnp.tile(cq_ref[...], (1, reps))
                   + q2[:, w:] * jnp.tile(sq_ref[...], (1, reps))) * scale).astype(BF16)
    kv = _dot(ckv.astype(BF16), wkv_ref[...])
    w = kv.shape[1] // 2
    heads = w // LANES
    k_ref[...] = (kv[:, :w] + jnp.tile(pltpu.roll(kpe, nope, 1), (1, heads))).astype(BF16)
    lane = lax.broadcasted_iota(jnp.int32, (1, w), 1)
    sum_lane = jnp.where((lane // LANES) % 2 == 0, _sum_lane(0), _sum_lane(1))
    v_ref[...] = jnp.where(lane % LANES == sum_lane, 1.0, kv[:, w:]).astype(BF16)


def _mla_prep_sample_kernel(x_ref, gn_ref, wa_ref, gq_ref, wq_ref, gkv_ref, wabs_ref, ck_ref, sk_ref,
                            h_ref, qlat_ref, qrope_ref, ckv_ref, kpe_ref, *,
                            scale, q_lora, kv_lora, rope, nope_w):
    cq_n, _, _ = _mla_common(x_ref, gn_ref, wa_ref, gq_ref, gkv_ref, ck_ref, sk_ref, h_ref,
                             ckv_ref, kpe_ref, q_lora, kv_lora, rope)
    q2 = _dot(cq_n, wq_ref[...])
    q_nope = (q2[:, :nope_w] * scale).astype(BF16)
    qlat_ref[...] = _dot(q_nope, wabs_ref[...])
    qrope_ref[...] = _rotate(q2[:, nope_w:], ck_ref[...], sk_ref[...], rope // 2) * scale


def _sum_lane(head):
    return 0 if head % 2 else LANES - 1


def _flash_kernel(it_ref, jt_ref, q_ref, k_ref, v_ref, o_ref, m_scr, acc_scr, *, heads):
    step = pl.program_id(1)
    i = it_ref[step]
    j = jt_ref[step]
    tq = q_ref.shape[1]

    @pl.when(j == 0)
    def _init():
        m_scr[...] = jnp.full(m_scr.shape, -jnp.inf, F32)
        acc_scr[...] = jnp.zeros(acc_scr.shape, F32)

    def update(masked):
        if masked:
            row = lax.broadcasted_iota(jnp.int32, (tq, tq), 0)
            col = lax.broadcasted_iota(jnp.int32, (tq, tq), 1)
            keep = row >= col
        all_scores = [lax.dot_general(q_ref[0, :, h * LANES:(h + 1) * LANES], k_ref[0, :, h * LANES:(h + 1) * LANES],
                                      NT, preferred_element_type=F32) for h in range(heads)]
        for h, s in enumerate(all_scores):
            sl = slice(h * LANES, (h + 1) * LANES)
            if masked:
                s = jnp.where(keep, s, -jnp.inf)
            m_prev = m_scr[h]
            m_new = jnp.maximum(m_prev, jnp.max(s, axis=1, keepdims=True))
            p = jnp.exp2(s - jnp.tile(m_new, (1, tq // LANES)))
            acc_scr[h] = jnp.exp2(m_prev - m_new) * acc_scr[h] + _dot(p.astype(BF16), v_ref[0, :, sl])
            m_scr[h] = m_new

    @pl.when(j < i)
    def _off_diagonal():
        update(False)

    @pl.when(j == i)
    def _diagonal():
        update(True)
        low_half = lax.broadcasted_iota(jnp.int32, (tq, LANES), 1) < LANES // 2
        for p in range(heads // 2):
            even, odd = acc_scr[2 * p], acc_scr[2 * p + 1]
            l_even = even[:, _sum_lane(0):_sum_lane(0) + 1]
            l_odd = odd[:, _sum_lane(1):_sum_lane(1) + 1]
            o_ref[0, :, p * LANES:(p + 1) * LANES] = jnp.where(low_half, even / l_even,
                                                               odd / l_odd).astype(o_ref.dtype)


def _paged_kernel(pt_ref, qlat_ref, qrope_ref, ckv_ref, kpe_ref, lat_hbm, rope_hbm, o_ref,
                  *scratch, heads, group, page, n_chunks, block_keys, kv_lora, rope):
    r = pl.program_id(0)
    n_req = pl.num_programs(0)
    t = qlat_ref.shape[0]
    lat_bufs = scratch[:PAGED_SLOTS]
    rope_bufs = scratch[PAGED_SLOTS:2 * PAGED_SLOTS]
    sem = scratch[2 * PAGED_SLOTS]

    def page_copies(req, chunk, slot, g):
        pg = pt_ref[req, chunk * group + g]
        rows = pl.ds(g * page, page)
        return (pltpu.make_async_copy(lat_hbm.at[pg], lat_bufs[slot].at[rows], sem.at[0, slot]),
                pltpu.make_async_copy(rope_hbm.at[pg], rope_bufs[slot].at[g], sem.at[1, slot]))

    def start_chunk(req, chunk, slot):
        for g in range(group):
            for cp in page_copies(req, chunk, slot, g):
                cp.start()

    def wait_chunk(req, chunk, slot):
        for g in range(group):
            for cp in page_copies(req, chunk, slot, g):
                cp.wait()

    @pl.when(r == 0)
    def _prime():
        for c in range(PAGED_AHEAD):
            start_chunk(0, c, c)

    q_lat = jnp.concatenate([qlat_ref[:, h * kv_lora:(h + 1) * kv_lora] for h in range(heads)], axis=0)
    q_rope = jnp.concatenate([qrope_ref[:, h * LANES:(h + 1) * LANES] for h in range(heads)],
                             axis=0)[:, :rope]

    def scores(lat, rp, rp_dims):
        return (lax.dot_general(q_lat.astype(lat.dtype), lat, NT, preferred_element_type=F32)
                + lax.dot_general(q_rope.astype(rp.dtype), rp, rp_dims, preferred_element_type=F32))

    def partial_softmax(s, lat):
        m = jnp.max(s, axis=1, keepdims=True)
        p = jnp.exp(s - m)
        return m, jnp.sum(p, axis=1, keepdims=True), _dot(p.astype(lat.dtype), lat)

    def combine(carry, parts):
        m_prev, l_prev, acc = carry
        m_new = m_prev
        for m, _, _ in parts:
            m_new = jnp.maximum(m_new, m)
        alpha = jnp.exp(m_prev - m_new)
        l_new, acc = alpha * l_prev, alpha * acc
        for m, l, a in parts:
            w = jnp.exp(m - m_new)
            l_new, acc = l_new + w * l, acc + w * a
        return m_new, l_new, acc

    def chunk_step(c, slot, carry):
        wait_chunk(r, c, slot)
        wrap = jnp.where(c + PAGED_AHEAD >= n_chunks, 1, 0)
        start_chunk(jnp.minimum(r + wrap, n_req - 1), c + PAGED_AHEAD - wrap * n_chunks,
                    (slot + PAGED_AHEAD) % PAGED_SLOTS)
        pages = block_keys // page
        lat_block = lambda blk: lat_bufs[slot][blk * block_keys:(blk + 1) * block_keys].astype(BF16)
        all_scores = []
        for blk in range(group // pages):
            rope_t = jnp.concatenate([rope_bufs[slot][blk * pages + g] for g in range(pages)], axis=1)
            all_scores.append(scores(lat_block(blk), rope_t.astype(BF16), NN))
        return combine(carry, [partial_softmax(s, lat_block(blk)) for blk, s in enumerate(all_scores)])

    def body(it, carry):
        for slot in range(PAGED_SLOTS):
            carry = chunk_step(PAGED_SLOTS * it + slot, slot, carry)
        return carry

    rows = heads * t
    init = (jnp.full((rows, 1), -jnp.inf, F32), jnp.zeros((rows, 1), F32), jnp.zeros((rows, kv_lora), F32))
    carry = lax.fori_loop(0, n_chunks // PAGED_SLOTS, body, init)

    @pl.when(r == n_req - 1)
    def _drain():
        for c in range(PAGED_AHEAD):
            wait_chunk(r, c, c)

    lat_new = ckv_ref[...]
    s_new = scores(lat_new, kpe_ref[...], NT)
    tok = lax.broadcasted_iota(jnp.int32, (rows, t), 0) % t
    key = lax.broadcasted_iota(jnp.int32, (rows, t), 1)
    s_new = jnp.where(key <= tok, s_new, -jnp.inf)
    _, l_fin, acc = combine(carry, [partial_softmax(s_new, lat_new)])
    o = acc / l_fin
    for h in range(heads):
        o_ref[:, h * kv_lora:(h + 1) * kv_lora] = o[h * t:(h + 1) * t].astype(o_ref.dtype)


def _hgrn_project(h_ref, wb_ref, lb_ref, g_scr, k_scr, q_scr, v_scr, fdim, vdim):
    z = _dot(h_ref[...], wb_ref[...])
    l0 = lb_ref[0:1, :]
    l1 = lb_ref[1:2, :]
    mx = jnp.maximum(l0, l1)
    e0 = jnp.exp(l0 - mx)
    e1 = jnp.exp(l1 - mx)
    lb = e1 / (e0 + e1)
    f = lb + (1.0 - lb) * jax.nn.sigmoid(z[:, :fdim])
    g_scr[...] = jnp.log(f)
    k_scr[...] = 1.0 - f
    q_scr[...] = _silu(z[:, fdim:2 * fdim])
    v_scr[...] = z[:, 2 * fdim:2 * fdim + vdim]
    return z[:, 2 * fdim + vdim:]


def _cumsum_rows(x):
    row = lax.broadcasted_iota(jnp.int32, x.shape, 0)
    shift = 1
    while shift < x.shape[0]:
        x = x + jnp.where(row >= shift, pltpu.roll(x, shift, 0), 0.0)
        shift *= 2
    return x


def _hgrn_chunk(qq, kk, g, v, st_ref, o_ref, rows, *, sub):
    c = qq.shape[0]
    n_pairs = st_ref.shape[0]
    hv = st_ref.shape[1] // 2
    hk = st_ref.shape[2] // 2
    mm = BF16 if sub % BF16_ROWS == 0 else F32
    bcum = _cumsum_rows(g)
    last = bcum[c - 1:c, :]
    q_in = (qq * jnp.exp(bcum)).astype(mm)
    k_dec = (kk * jnp.exp(last - bcum)).astype(mm)
    v_mm = v.astype(mm)
    n_sub = c // sub
    row_blk = lax.broadcasted_iota(jnp.int32, (c, hk), 0) // sub
    causal = lax.broadcasted_iota(jnp.int32, (c, c), 0) >= lax.broadcasted_iota(jnp.int32, (c, c), 1)
    lane_v = lax.broadcasted_iota(jnp.int32, (1, 2 * hv), 1) // hv
    bd = (lax.broadcasted_iota(jnp.int32, (2 * hv, 2 * hk), 0) // hv
          == lax.broadcasted_iota(jnp.int32, (2 * hv, 2 * hk), 1) // hk)
    def head_scores(hl):
        b_h, q_h = bcum[:, hl], qq[:, hl]
        refs = [b_h[j * sub + sub // 2:j * sub + sub // 2 + 1] for j in range(n_sub)]
        ref_rows = jnp.concatenate([jnp.broadcast_to(rj, (sub, hk)) for rj in refs], axis=0)
        k_all = kk[:, hl] * jnp.exp(ref_rows - b_h)
        k_cat = jnp.concatenate([jnp.where(row_blk == j, k_all, 0.0).astype(mm) for j in range(n_sub)], axis=1)
        q_cat = jnp.concatenate(
            [jnp.concatenate([jnp.zeros((j * sub, hk), F32)] * (j > 0)
                             + [q_h[j * sub:] * jnp.exp(b_h[j * sub:] - refs[j])], axis=0).astype(mm)
             for j in range(n_sub)], axis=1)
        att = lax.dot_general(q_cat, k_cat, NT, preferred_element_type=F32)
        return jnp.where(causal, att, 0.0).astype(mm)

    atts = [head_scores(slice(h * hk, (h + 1) * hk)) for h in range(2 * n_pairs)]
    for p in range(n_pairs):
        pk = slice(p * 2 * hk, (p + 1) * 2 * hk)
        pv = slice(p * 2 * hv, (p + 1) * 2 * hv)
        vp = v_mm[:, pv]
        st = st_ref[p]
        o = lax.dot_general(q_in[:, pk], st.astype(mm), NT, preferred_element_type=F32)
        for e in range(2):
            o = o + _dot(atts[2 * p + e], jnp.where(lane_v == e, vp, jnp.zeros_like(vp)))
        o_ref[rows, pv] = o
        upd = lax.dot_general(vp, k_dec[:, pk], TN, preferred_element_type=F32)
        st_ref[p] = st * jnp.exp(last[:, pk]) + jnp.where(bd, upd, 0.0)


def _hgrn_finish(o, zb, gho_ref, avg_ref, out_ref):
    ms = _dot((o * o).astype(BF16), avg_ref[...])
    out_ref[...] = (o * lax.rsqrt(ms + EPS) * gho_ref[...] * _silu(zb)).astype(out_ref.dtype)


def _compact_state(st):
    hv = st.shape[0] // 2
    hk = st.shape[1] // 2
    return jnp.concatenate([st[:hv, :hk], st[hv:, hk:]], axis=0)


def _expand_state(sc):
    hv = sc.shape[0] // 2
    z = jnp.zeros((hv, sc.shape[1]), F32)
    return jnp.concatenate([jnp.concatenate([sc[:hv], z], axis=1),
                            jnp.concatenate([z, sc[hv:]], axis=1)], axis=0)


def _hgrn_prompt_kernel(h_ref, wb_ref, lb_ref, gho_ref, avg_ref, out_ref, sfin_ref,
                        st_scr, g_scr, k_scr, q_scr, v_scr, o_scr, *, chunk, sub, fdim, vdim):
    tb = pl.program_id(1)

    @pl.when(tb == 0)
    def _init():
        st_scr[...] = jnp.zeros(st_scr.shape, F32)

    zb = _hgrn_project(h_ref, wb_ref, lb_ref, g_scr, k_scr, q_scr, v_scr, fdim, vdim)

    def body(ci, carry):
        rows = pl.ds(pl.multiple_of(ci * chunk, chunk), chunk)
        _hgrn_chunk(q_scr[rows, :], k_scr[rows, :], g_scr[rows, :], v_scr[rows, :], st_scr, o_scr, rows,
                    sub=sub)
        return carry

    lax.fori_loop(0, h_ref.shape[0] // chunk, body, 0, unroll=True)
    _hgrn_finish(o_scr[...], zb, gho_ref, avg_ref, out_ref)

    @pl.when(tb == pl.num_programs(1) - 1)
    def _final():
        for p in range(st_scr.shape[0]):
            sfin_ref[0, p] = _compact_state(st_scr[p])


def _hgrn_sample_kernel(h_ref, wb_ref, lb_ref, gho_ref, avg_ref, s0_ref, out_ref, sfin_ref,
                        st_scr, g_scr, k_scr, q_scr, v_scr, o_scr, *, t, fdim, vdim):
    zb = _hgrn_project(h_ref, wb_ref, lb_ref, g_scr, k_scr, q_scr, v_scr, fdim, vdim)

    for ri in range(s0_ref.shape[0]):
        st_ref = st_scr.at[ri]
        for p in range(st_ref.shape[0]):
            st_ref[p] = _expand_state(s0_ref[ri, p])
        rows = pl.ds(ri * t, t)
        _hgrn_chunk(q_scr[rows, :], k_scr[rows, :], g_scr[rows, :], v_scr[rows, :], st_ref, o_scr, rows,
                    sub=t)
        for p in range(st_ref.shape[0]):
            sfin_ref[ri, p] = _compact_state(st_ref[p])
    _hgrn_finish(o_scr[...], zb, gho_ref, avg_ref, out_ref)


def _xattn_heads(q, zc, head_k, head_v, heads, hdim):
    outs = []
    all_scores = [lax.dot_general(q[:, h * hdim:(h + 1) * hdim], head_k(h), NT, preferred_element_type=F32)
                  for h in range(heads)]
    for h, s in enumerate(all_scores):
        mv = head_v(h)
        s = s - jnp.max(s, axis=1, keepdims=True)
        p = jnp.exp(s)
        p = (p / jnp.sum(p, axis=1, keepdims=True)).astype(mv.dtype)
        outs.append(_dot(p, mv))
    return jnp.concatenate(outs, axis=1) * _silu(zc)


def _xattn_prompt_kernel(h_ref, wc_ref, mk_ref, mv_ref, out_ref, *, heads, hdim, scale):
    z = _dot(h_ref[0], wc_ref[...])
    w = heads * hdim
    q = (z[:, :w] * scale).astype(BF16)
    mk = mk_ref[0].astype(BF16)
    mv = mv_ref[0].astype(BF16)
    out_ref[0] = _xattn_heads(q, z[:, w:], lambda h: mk[:, h * hdim:(h + 1) * hdim],
                              lambda h: mv[:, h * hdim:(h + 1) * hdim], heads, hdim).astype(out_ref.dtype)


def _xattn_sample_kernel(h_ref, wc_ref, mk_ref, mv_ref, out_ref, *, heads, hdim, scale, t):
    z = _dot(h_ref[...], wc_ref[...])
    w = heads * hdim
    q = z[:, :w] * scale
    zc = z[:, w:]
    n_rows = mk_ref.shape[1]
    row_head = lax.broadcasted_iota(jnp.int32, (heads * t, n_rows), 0) // t
    col_head = lax.broadcasted_iota(jnp.int32, (heads * t, n_rows), 1) % heads
    all_scores = []
    for ri in range(mk_ref.shape[0]):
        qs = jnp.concatenate([q[ri * t:(ri + 1) * t, h * hdim:(h + 1) * hdim] for h in range(heads)], axis=0)
        all_scores.append(lax.dot_general(qs.astype(BF16), mk_ref[ri].astype(BF16), NT,
                                          preferred_element_type=F32))
    for ri, s in enumerate(all_scores):
        rows = slice(ri * t, (ri + 1) * t)
        s = jnp.where(row_head == col_head, s, -jnp.inf)
        p = jnp.exp(s - jnp.max(s, axis=1, keepdims=True))
        p = (p / jnp.sum(p, axis=1, keepdims=True)).astype(BF16)
        o = _dot(p, mv_ref[ri].astype(BF16))
        o = jnp.concatenate([o[h * t:(h + 1) * t] for h in range(heads)], axis=1)
        out_ref[rows, :] = (o * _silu(zc[rows])).astype(out_ref.dtype)


def _memkv_kernel(m_ref, g_ref, w_ref, k_ref, v_ref):
    kv = _dot(_rms(m_ref[...], g_ref[...]).astype(BF16), w_ref[...])
    w = kv.shape[1] // 2
    k_ref[...] = kv[:, :w]
    v_ref[...] = kv[:, w:]


def _merge_kernel(x_ref, h_ref, oa_ref, ob_ref, oc_ref, wuv_ref, wm_ref, wbr_ref, wout_ref, gf_ref, y_ref, *,
                  width, from_latent):
    h = h_ref[...]
    d = x_ref.shape[1]
    oa = oa_ref[...]
    if from_latent:
        oa = _dot(oa.astype(BF16), wuv_ref[...])
    else:
        oa = oa.astype(F32)
    za = _dot(h, wm_ref[:, :width])
    branches = ((oa * _silu(za)).astype(BF16), ob_ref[...].astype(BF16), oc_ref[...].astype(BF16))
    merged = jnp.zeros((x_ref.shape[0], d), F32)
    for n, o in enumerate(branches):
        gate = jax.nn.sigmoid(_dot(h, wm_ref[:, width + n * d:width + (n + 1) * d]))
        merged = merged + gate * _dot(o, wbr_ref[n])
    out = x_ref[...] + _dot(merged.astype(BF16), wout_ref[...])
    y_ref[...] = _rms(out, gf_ref[...])


def _rope_tables(pos, rope, lead):
    half = rope // 2
    inv = jnp.exp(-math.log(ROPE_BASE) * jnp.arange(half, dtype=F32) / half)
    ang = pos.astype(F32)[:, None] * inv[None, :]
    n = pos.shape[0]
    pad = jnp.zeros((n, LANES - lead - rope), F32)
    cos = jnp.concatenate([jnp.ones((n, lead), F32), jnp.cos(ang), jnp.cos(ang), pad], axis=1)
    sin_signed = jnp.concatenate([jnp.zeros((n, lead), F32), -jnp.sin(ang), jnp.sin(ang), pad], axis=1)
    return cos, sin_signed


def _pad_cols(w, width):
    return jnp.pad(w, [(0, 0)] * (w.ndim - 1) + [(0, width - w.shape[-1])])


def _block_diag(blocks):
    rows = sum(b.shape[0] for b in blocks)
    cols = sum(b.shape[1] for b in blocks)
    out = jnp.zeros((rows, cols), blocks[0].dtype)
    r = c = 0
    for b in blocks:
        out = lax.dynamic_update_slice(out, b, (r, c))
        r += b.shape[0]
        c += b.shape[1]
    return out


def _tile_rows(n, pref):
    t = min(n, pref)
    assert n % t == 0
    return t


def kernel(x_prompt, x_sample, mem_prompt, cache_kv_latent, cache_k_rope, page_table, state_hgrn, cache_mem_k, cache_mem_v, g_norm, w_in, g_q_lora, w_uq, g_kv_lora, w_uk, w_uv, lb_logits, g_hgrn_out, g_mem_norm, w_mem_kv, w_branch, w_out, g_final):
    depth = w_in.shape[0]
    assert depth == 1, "one layer per step"
    b, t, d = x_prompt.shape
    nb, ts, _ = x_sample.shape
    n_mem = mem_prompt.shape[1]
    n_phys, page, kv_lora = cache_kv_latent.shape[1:]
    rope = cache_k_rope.shape[-1]
    n_pages = page_table.shape[1]
    q_lora = g_q_lora.shape[-1]
    a_heads, a_qk = w_uq.shape[2:]
    a_nope = w_uk.shape[-1]
    a_vdim = w_uv.shape[-1]
    b_heads, b_expand, b_vdim = state_hgrn.shape[2:]
    fdim = b_heads * b_expand
    c_heads, c_hdim = cache_mem_k.shape[3:]
    width = w_branch.shape[2]
    assert a_qk == a_nope + rope and a_heads * a_vdim == width and b_heads * b_vdim == width
    assert c_heads * c_hdim == width and a_qk <= LANES and 2 * a_vdim == LANES and 2 * b_vdim == LANES
    assert b_expand == LANES and a_heads % 2 == 0 and b_heads % 2 == 0

    splits = (q_lora, kv_lora, rope, width, fdim, fdim, width, width, width, width, N_BRANCH * d)
    offs = np.concatenate([[0], np.cumsum(splits)])
    assert offs[-1] == w_in.shape[-1]
    w_in_t = jnp.swapaxes(w_in, 1, 2).reshape(w_in.shape[-1], d)

    def in_proj(*pieces, pad_to=None):
        rows = [w_in_t[offs[i]:offs[i + 1]] for i in pieces]
        if pad_to is not None:
            rows.append(jnp.zeros((pad_to - sum(r.shape[0] for r in rows), d), F32))
        return jnp.concatenate(rows, axis=0).astype(BF16).T

    a_scale = a_qk ** -0.5
    n_p = b * t
    n_s = nb * ts
    xp = x_prompt.reshape(n_p, d)
    xs = x_sample.reshape(n_s, d)
    row = lambda g: g.reshape(1, -1).astype(F32)

    w_a = in_proj(0, 1, 2, pad_to=q_lora + kv_lora + LANES)
    uq = w_uq.reshape(q_lora, a_heads, a_qk)
    uq_swapped = jnp.concatenate([jnp.zeros_like(uq[..., :a_nope]), uq[..., a_nope + rope // 2:],
                                  uq[..., a_nope:a_nope + rope // 2]], axis=-1)
    wq_prompt = jnp.concatenate([_pad_cols(uq, LANES).reshape(q_lora, -1),
                                 _pad_cols(uq_swapped, LANES).reshape(q_lora, -1)], axis=1).astype(BF16)
    wq_sample = jnp.concatenate([uq[..., :a_nope].reshape(q_lora, -1),
                                 _pad_cols(uq[..., a_nope:], LANES).reshape(q_lora, -1)], axis=1).astype(BF16)
    uk = w_uk.reshape(kv_lora, a_heads, a_nope)
    uv = w_uv.reshape(kv_lora, a_heads, a_vdim)
    uv_pad = jnp.stack([jnp.pad(uv[:, h], ((0, 0), ((h % 2) * a_vdim, LANES - a_vdim - (h % 2) * a_vdim)))
                        for h in range(a_heads)], axis=1)
    w_kv = jnp.concatenate([_pad_cols(uk, LANES).reshape(kv_lora, -1), uv_pad.reshape(kv_lora, -1)],
                           axis=1).astype(BF16)
    w_abs = _block_diag([uk[:, h].T for h in range(a_heads)]).astype(BF16)
    w_uv_bd = _block_diag([uv[:, h] for h in range(a_heads)]).astype(BF16)
    w_b = in_proj(4, 5, 6, 7)
    w_c = in_proj(8, 9)
    w_m = in_proj(3, 10)
    w_br = w_branch[0].astype(BF16)
    w_o = w_out[0].astype(BF16)
    w_mem = w_mem_kv[0].astype(BF16)
    gho = jnp.tile(g_hgrn_out[0], b_heads).reshape(1, width).astype(F32)
    head_of = jnp.arange(width) // b_vdim
    avg = ((head_of[:, None] == head_of[None, :]).astype(F32) / b_vdim).astype(BF16)
    lb2 = lb_logits[:2].astype(F32)

    n_past = n_pages * page
    cq_p, sq_p = _rope_tables(jnp.arange(t), rope, a_nope)
    ck_p, sk_p = _rope_tables(jnp.arange(t), rope, 0)
    tm_s = _tile_rows(n_s, 256)
    assert tm_s % ts == 0
    ck_s, sk_s = _rope_tables(n_past + (jnp.arange(tm_s) % ts), rope, 0)

    tm = _tile_rows(t, 512)
    n_t = t // tm
    mla_common = dict(q_lora=q_lora, kv_lora=kv_lora, rope=rope)
    rowspec = lambda w: pl.BlockSpec((tm, w), lambda i: (i, 0))
    tabspec = pl.BlockSpec((tm, LANES), lambda i: (i % n_t, 0))
    hq = a_heads * LANES
    h_p, q_p, k_p, v_p, ckv_p, kpe_p = pl.pallas_call(
        functools.partial(_mla_prep_prompt_kernel, scale=a_scale * math.log2(math.e), nope=a_nope,
                          **mla_common),
        grid=(n_p // tm,),
        in_specs=[rowspec(d), _const_spec((1, d)), _const_spec(w_a.shape), _const_spec((1, q_lora)),
                  _const_spec(wq_prompt.shape), _const_spec((1, kv_lora)), _const_spec(w_kv.shape),
                  tabspec, tabspec, tabspec, tabspec],
        out_specs=[rowspec(d), rowspec(hq), rowspec(hq), rowspec(hq), rowspec(kv_lora), rowspec(rope)],
        out_shape=[jax.ShapeDtypeStruct((n_p, d), BF16), jax.ShapeDtypeStruct((n_p, hq), BF16),
                   jax.ShapeDtypeStruct((n_p, hq), BF16), jax.ShapeDtypeStruct((n_p, hq), BF16),
                   jax.ShapeDtypeStruct((n_p, kv_lora), F32), jax.ShapeDtypeStruct((n_p, rope), F32)],
        compiler_params=_params(("parallel",)),
    )(xp, row(g_norm), w_a, row(g_q_lora), wq_prompt, row(g_kv_lora), w_kv, cq_p, sq_p, ck_p, sk_p)

    srow = lambda w: pl.BlockSpec((tm_s, w), lambda i: (i, 0))
    hl = a_heads * kv_lora
    h_s, qlat_s, qrope_s, ckv_s, kpe_s = pl.pallas_call(
        functools.partial(_mla_prep_sample_kernel, scale=a_scale, nope_w=a_heads * a_nope, **mla_common),
        grid=(n_s // tm_s,),
        in_specs=[srow(d), _const_spec((1, d)), _const_spec(w_a.shape), _const_spec((1, q_lora)),
                  _const_spec(wq_sample.shape), _const_spec((1, kv_lora)), _const_spec(w_abs.shape),
                  _const_spec((tm_s, LANES)), _const_spec((tm_s, LANES))],
        out_specs=[srow(d), srow(hl), srow(hq), srow(kv_lora), srow(rope)],
        out_shape=[jax.ShapeDtypeStruct((n_s, d), BF16), jax.ShapeDtypeStruct((n_s, hl), F32),
                   jax.ShapeDtypeStruct((n_s, hq), F32), jax.ShapeDtypeStruct((n_s, kv_lora), F32),
                   jax.ShapeDtypeStruct((n_s, rope), F32)],
        compiler_params=_params(("parallel",)),
    )(xs, row(g_norm), w_a, row(g_q_lora), wq_sample, row(g_kv_lora), w_abs, ck_s, sk_s)

    tq = _tile_rows(t, 512)
    assert tq % LANES == 0
    nq = t // tq
    pairs = [(i, j) for i in range(nq) for j in range(i + 1)]
    i_tab = jnp.asarray([p[0] for p in pairs], jnp.int32)
    j_tab = jnp.asarray([p[1] for p in pairs], jnp.int32)
    qkv = lambda a: a.reshape(b, t, hq)
    oa_p = pl.pallas_call(
        functools.partial(_flash_kernel, heads=a_heads),
        grid_spec=pltpu.PrefetchScalarGridSpec(
            num_scalar_prefetch=2, grid=(b, len(pairs)),
            in_specs=[pl.BlockSpec((1, tq, hq), lambda bi, s, it, jt: (bi, it[s], 0)),
                      pl.BlockSpec((1, tq, hq), lambda bi, s, it, jt: (bi, jt[s], 0)),
                      pl.BlockSpec((1, tq, hq), lambda bi, s, it, jt: (bi, jt[s], 0))],
            out_specs=pl.BlockSpec((1, tq, width), lambda bi, s, it, jt: (bi, it[s], 0)),
            scratch_shapes=[pltpu.VMEM((a_heads, tq, LANES), F32)] * 2),
        out_shape=jax.ShapeDtypeStruct((b, t, width), BF16),
        compiler_params=_params(("parallel", "arbitrary")),
    )(i_tab, j_tab, qkv(q_p), qkv(k_p), qkv(v_p)).reshape(n_p, width)

    assert n_pages % PAGED_SLOTS == 0
    group = math.gcd(n_pages // PAGED_SLOTS, 32)
    n_chunks = n_pages // group
    block_keys = math.gcd(group, 8) * page
    req = lambda w: pl.BlockSpec((ts, w), lambda r, pt: (r, 0))
    olat_s = pl.pallas_call(
        functools.partial(_paged_kernel, heads=a_heads, group=group, page=page, n_chunks=n_chunks,
                          block_keys=block_keys,
                          kv_lora=kv_lora, rope=rope),
        grid_spec=pltpu.PrefetchScalarGridSpec(
            num_scalar_prefetch=1, grid=(nb,),
            in_specs=[req(hl), req(hq), req(kv_lora), req(rope),
                      pl.BlockSpec(memory_space=pl.ANY), pl.BlockSpec(memory_space=pl.ANY)],
            out_specs=req(hl),
            scratch_shapes=[pltpu.VMEM((group * page, kv_lora), F32)] * PAGED_SLOTS
            + [pltpu.VMEM((group, rope, page), F32)] * PAGED_SLOTS
            + [pltpu.SemaphoreType.DMA((2, PAGED_SLOTS))]),
        out_shape=jax.ShapeDtypeStruct((n_s, hl), F32),
        compiler_params=_params(("arbitrary",)),
    )(page_table, qlat_s, qrope_s, ckv_s, kpe_s, cache_kv_latent.reshape(n_phys, page, kv_lora),
      jnp.swapaxes(cache_k_rope, 2, 3).reshape(n_phys, rope, page))

    n_pairs = b_heads // 2
    tb = _tile_rows(t, 256)
    chunk = _tile_rows(tb, 64)
    sub = _tile_rows(chunk, 16)
    wb_cols = w_b.shape[1]
    hgrn_scratch = lambda rows, *lead: [pltpu.VMEM(lead + (n_pairs, LANES, 2 * b_expand), F32),
                                 pltpu.VMEM((rows, fdim), F32), pltpu.VMEM((rows, fdim), F32),
                                 pltpu.VMEM((rows, fdim), F32), pltpu.VMEM((rows, width), F32),
                                 pltpu.VMEM((rows, width), F32)]
    ob_p, st_p = pl.pallas_call(
        functools.partial(_hgrn_prompt_kernel, chunk=chunk, sub=sub, fdim=fdim, vdim=width),
        grid=(b, t // tb),
        in_specs=[pl.BlockSpec((tb, d), lambda bi, ti: (bi * (t // tb) + ti, 0)),
                  _const_spec((d, wb_cols)), _const_spec((2, fdim)), _const_spec((1, width)),
                  _const_spec((width, width))],
        out_specs=[pl.BlockSpec((tb, width), lambda bi, ti: (bi * (t // tb) + ti, 0)),
                   pl.BlockSpec((1, n_pairs, LANES, b_expand), lambda bi, ti: (bi, 0, 0, 0))],
        out_shape=[jax.ShapeDtypeStruct((n_p, width), BF16),
                   jax.ShapeDtypeStruct((b, n_pairs, LANES, b_expand), F32)],
        scratch_shapes=hgrn_scratch(tb),
        compiler_params=_params(("parallel", "arbitrary")),
    )(h_p, w_b, lb2, gho, avg)

    def to_pairs(s):
        n = s.shape[0]
        return s.reshape(n, n_pairs, 2, b_expand, b_vdim).transpose(0, 1, 2, 4, 3).reshape(
            n, n_pairs, LANES, b_expand)

    def from_pairs(s):
        n = s.shape[0]
        return s.reshape(n, n_pairs, 2, b_vdim, b_expand).transpose(0, 1, 2, 4, 3).reshape(
            n, b_heads, b_expand, b_vdim)

    rq = _tile_rows(nb, 8)
    ob_s, st_s = pl.pallas_call(
        functools.partial(_hgrn_sample_kernel, t=ts, fdim=fdim, vdim=width),
        grid=(nb // rq,),
        in_specs=[pl.BlockSpec((rq * ts, d), lambda i: (i, 0)),
                  _const_spec((d, wb_cols)), _const_spec((2, fdim)), _const_spec((1, width)),
                  _const_spec((width, width)),
                  pl.BlockSpec((rq, n_pairs, LANES, b_expand), lambda i: (i, 0, 0, 0))],
        out_specs=[pl.BlockSpec((rq * ts, width), lambda i: (i, 0)),
                   pl.BlockSpec((rq, n_pairs, LANES, b_expand), lambda i: (i, 0, 0, 0))],
        out_shape=[jax.ShapeDtypeStruct((n_s, width), BF16),
                   jax.ShapeDtypeStruct((nb, n_pairs, LANES, b_expand), F32)],
        scratch_shapes=hgrn_scratch(rq * ts, rq),
        compiler_params=_params(("parallel",)),
    )(h_s, w_b, lb2, gho, avg, to_pairs(state_hgrn.reshape(nb, b_heads, b_expand, b_vdim)))

    n_m = b * n_mem
    tmm = _tile_rows(n_m, 512)
    mk_p, mv_p = pl.pallas_call(
        _memkv_kernel,
        grid=(n_m // tmm,),
        in_specs=[pl.BlockSpec((tmm, d), lambda i: (i, 0)), _const_spec((1, d)), _const_spec(w_mem.shape)],
        out_specs=[pl.BlockSpec((tmm, width), lambda i: (i, 0)), pl.BlockSpec((tmm, width), lambda i: (i, 0))],
        out_shape=[jax.ShapeDtypeStruct((n_m, width), F32), jax.ShapeDtypeStruct((n_m, width), F32)],
        compiler_params=_params(("parallel",)),
    )(mem_prompt.reshape(n_m, d), row(g_mem_norm), w_mem)

    c_scale = c_hdim ** -0.5
    tx = _tile_rows(t, 512)
    memspec = pl.BlockSpec((1, n_mem, width), lambda bi, ti: (bi, 0, 0))
    oc_p = pl.pallas_call(
        functools.partial(_xattn_prompt_kernel, heads=c_heads, hdim=c_hdim, scale=c_scale),
        grid=(b, t // tx),
        in_specs=[pl.BlockSpec((1, tx, d), lambda bi, ti: (bi, ti, 0)), _const_spec(w_c.shape), memspec, memspec],
        out_specs=pl.BlockSpec((1, tx, width), lambda bi, ti: (bi, ti, 0)),
        out_shape=jax.ShapeDtypeStruct((b, t, width), BF16),
        compiler_params=_params(("parallel", "parallel")),
    )(h_p.reshape(b, t, d), w_c, mk_p.reshape(b, n_mem, width), mv_p.reshape(b, n_mem, width)).reshape(n_p, width)

    rx = _tile_rows(nb, 8)
    smem = pl.BlockSpec((rx, n_mem * c_heads, c_hdim), lambda i: (i, 0, 0))
    oc_s = pl.pallas_call(
        functools.partial(_xattn_sample_kernel, heads=c_heads, hdim=c_hdim, scale=c_scale, t=ts),
        grid=(nb // rx,),
        in_specs=[pl.BlockSpec((rx * ts, d), lambda i: (i, 0)), _const_spec(w_c.shape), smem, smem],
        out_specs=pl.BlockSpec((rx * ts, width), lambda i: (i, 0)),
        out_shape=jax.ShapeDtypeStruct((n_s, width), F32),
        compiler_params=_params(("parallel",)),
    )(h_s, w_c, cache_mem_k.reshape(nb, n_mem * c_heads, c_hdim), cache_mem_v.reshape(nb, n_mem * c_heads, c_hdim))

    def merge(x2, h2, oa, ob, oc, from_latent):
        n = x2.shape[0]
        tmg = _tile_rows(n, 512)
        rs = lambda w: pl.BlockSpec((tmg, w), lambda i: (i, 0))
        return pl.pallas_call(
            functools.partial(_merge_kernel, width=width, from_latent=from_latent),
            grid=(n // tmg,),
            in_specs=[rs(d), rs(d), rs(oa.shape[1]), rs(width), rs(width), _const_spec(w_uv_bd.shape),
                      _const_spec(w_m.shape), _const_spec(w_br.shape), _const_spec(w_o.shape),
                      _const_spec((1, d))],
            out_specs=rs(d),
            out_shape=jax.ShapeDtypeStruct((n, d), F32),
            compiler_params=_params(("parallel",)),
        )(x2, h2, oa, ob, oc, w_uv_bd, w_m, w_br, w_o, row(g_final))

    y_p = merge(xp, h_p, oa_p, ob_p, oc_p, False).reshape(b, t, d)
    y_s = merge(xs, h_s, olat_s, ob_s, oc_s, True).reshape(nb, ts, d)

    return (y_p, y_s,
            ckv_p.reshape(1, b, t, kv_lora), kpe_p.reshape(1, b, t, rope),
            from_pairs(st_p)[None],
            mk_p.reshape(1, b, n_mem, c_heads, c_hdim), mv_p.reshape(1, b, n_mem, c_heads, c_hdim),
            ckv_s.reshape(1, nb, ts, kv_lora), kpe_s.reshape(1, nb, ts, rope),
            from_pairs(st_s)[None])
```

```python
import functools
import math

import numpy as np
import jax
import jax.numpy as jnp
from jax import lax
from jax.experimental import pallas as pl
from jax.experimental.pallas import tpu as pltpu

F32 = jnp.float32
BF16 = jnp.bfloat16
EPS = 1e-6
ROPE_BASE = 10000.0
N_BRANCH = 3
LANES = 128
BF16_ROWS = 16
PAGED_SLOTS = 4
PAGED_AHEAD = 3
VMEM_LIMIT = 56 * 1024 * 1024
NT = (((1,), (1,)), ((), ()))
TN = (((0,), (0,)), ((), ()))
NN = (((1,), (0,)), ((), ()))


def _params(sem):
    return pltpu.CompilerParams(dimension_semantics=sem, vmem_limit_bytes=VMEM_LIMIT)


def _rms(x, g):
    return x * lax.rsqrt(jnp.mean(x * x, axis=-1, keepdims=True) + EPS) * g


def _silu(x):
    return x * jax.nn.sigmoid(x)


def _dot(a, b):
    return jnp.dot(a, b, preferred_element_type=F32)


def _const_spec(shape):
    nd = len(shape)
    return pl.BlockSpec(shape, lambda *_: (0,) * nd)


def _rotate(x, cos, sin_signed, half):
    n = x.shape[1]
    first_half = (lax.broadcasted_iota(jnp.int32, x.shape, 1) // half) % 2 == 0
    partner = jnp.where(first_half, pltpu.roll(x, n - half, 1), pltpu.roll(x, half, 1))
    reps = n // cos.shape[1]
    return x * jnp.tile(cos, (1, reps)) + partner * jnp.tile(sin_signed, (1, reps))


def _mla_common(x_ref, gn_ref, wa_ref, gq_ref, gkv_ref, ck_ref, sk_ref, h_ref, ckv_ref, kpe_ref,
                q_lora, kv_lora, rope, kpe_tokens_on_lanes=False):
    h = _rms(x_ref[...], gn_ref[...]).astype(BF16)
    h_ref[...] = h
    z = _dot(h, wa_ref[...])
    cq_n = _rms(z[:, :q_lora], gq_ref[...]).astype(BF16)
    ckv = _rms(z[:, q_lora:q_lora + kv_lora], gkv_ref[...])
    ckv_ref[...] = ckv
    o = q_lora + kv_lora
    kpe = _rotate(z[:, o:o + LANES], ck_ref[...], sk_ref[...], rope // 2)
    if kpe_tokens_on_lanes:
        kpe_ref[0] = kpe.T[:rope, :]
    else:
        kpe_ref[...] = kpe[:, :rope]
    return cq_n, ckv, kpe


def _mla_prep_prompt_kernel(x_ref, gn_ref, wa_ref, gq_ref, wq_ref, gkv_ref, wkv_ref, cq_ref, sq_ref,
                            ck_ref, sk_ref, h_ref, q_ref, k_ref, v_ref, ckv_ref, kpe_ref, *,
                            scale, q_lora, kv_lora, rope, nope):
    cq_n, ckv, kpe = _mla_common(x_ref, gn_ref, wa_ref, gq_ref, gkv_ref, ck_ref, sk_ref, h_ref,
                                 ckv_ref, kpe_ref, q_lora, kv_lora, rope, kpe_tokens_on_lanes=True)
    q2 = _dot(cq_n, wq_ref[...])
    w = q2.shape[1] // 2
    reps = w // LANES
    q_ref[...] = ((q2[:, :w] * jnp.tile(cq_ref[...], (1, reps))
                   + q2[:, w:] * jnp.tile(sq_ref[...], (1, reps))) * scale).astype(BF16)
    kv = _dot(ckv.astype(BF16), wkv_ref[...])
    w = kv.shape[1] // 2
    heads = w // LANES
    k_ref[...] = (kv[:, :w] + jnp.tile(pltpu.roll(kpe, nope, 1), (1, heads))).astype(BF16)
    lane = lax.broadcasted_iota(jnp.int32, (1, w), 1)
    sum_lane = jnp.where((lane // LANES) % 2 == 0, _sum_lane(0), _sum_lane(1))
    v_ref[...] = jnp.where(lane % LANES == sum_lane, 1.0, kv[:, w:]).astype(BF16)


def _mla_prep_sample_kernel(x_ref, gn_ref, wa_ref, gq_ref, wq_ref, gkv_ref, wabs_ref, ck_ref, sk_ref,
                            h_ref, qlat_ref, qrope_ref, ckv_ref, kpe_ref, *,
                            scale, q_lora, kv_lora, rope, nope_w):
    cq_n, _, _ = _mla_common(x_ref, gn_ref, wa_ref, gq_ref, gkv_ref, ck_ref, sk_ref, h_ref,
                             ckv_ref, kpe_ref, q_lora, kv_lora, rope)
    q2 = _dot(cq_n, wq_ref[...])
    q_nope = (q2[:, :nope_w] * scale).astype(BF16)
    qlat_ref[...] = _dot(q_nope, wabs_ref[...])
    qrope_ref[...] = _rotate(q2[:, nope_w:], ck_ref[...], sk_ref[...], rope // 2) * scale


def _sum_lane(head):
    return 0 if head % 2 else LANES - 1


def _flash_kernel(it_ref, jt_ref, q_ref, k_ref, v_ref, o_ref, m_scr, acc_scr, *, heads):
    step = pl.program_id(1)
    i = it_ref[step]
    j = jt_ref[step]
    tq = q_ref.shape[1]

    @pl.when(j == 0)
    def _init():
        m_scr[...] = jnp.full(m_scr.shape, -jnp.inf, F32)
        acc_scr[...] = jnp.zeros(acc_scr.shape, F32)

    def update(masked):
        if masked:
            row = lax.broadcasted_iota(jnp.int32, (tq, tq), 0)
            col = lax.broadcasted_iota(jnp.int32, (tq, tq), 1)
            keep = row >= col
        all_scores = [lax.dot_general(q_ref[0, :, h * LANES:(h + 1) * LANES], k_ref[0, :, h * LANES:(h + 1) * LANES],
                                      NT, preferred_element_type=F32) for h in range(heads)]
        for h, s in enumerate(all_scores):
            sl = slice(h * LANES, (h + 1) * LANES)
            if masked:
                s = jnp.where(keep, s, -jnp.inf)
            m_prev = m_scr[h]
            m_new = jnp.maximum(m_prev, jnp.max(s, axis=1, keepdims=True))
            p = jnp.exp2(s - jnp.tile(m_new, (1, tq // LANES)))
            acc_scr[h] = jnp.exp2(m_prev - m_new) * acc_scr[h] + _dot(p.astype(BF16), v_ref[0, :, sl])
            m_scr[h] = m_new

    @pl.when(j < i)
    def _off_diagonal():
        update(False)

    @pl.when(j == i)
    def _diagonal():
        update(True)
        low_half = lax.broadcasted_iota(jnp.int32, (tq, LANES), 1) < LANES // 2
        for p in range(heads // 2):
            even, odd = acc_scr[2 * p], acc_scr[2 * p + 1]
            l_even = even[:, _sum_lane(0):_sum_lane(0) + 1]
            l_odd = odd[:, _sum_lane(1):_sum_lane(1) + 1]
            o_ref[0, :, p * LANES:(p + 1) * LANES] = jnp.where(low_half, even / l_even,
                                                               odd / l_odd).astype(o_ref.dtype)


def _paged_kernel(pt_ref, qlat_ref, qrope_ref, ckv_ref, kpe_ref, lat_hbm, rope_hbm, o_ref,
                  *scratch, heads, group, page, n_chunks, block_keys, kv_lora, rope):
    r = pl.program_id(0)
    n_req = pl.num_programs(0)
    t = qlat_ref.shape[0]
    lat_bufs = scratch[:PAGED_SLOTS]
    rope_bufs = scratch[PAGED_SLOTS:2 * PAGED_SLOTS]
    sem = scratch[2 * PAGED_SLOTS]

    def page_copies(req, chunk, slot, g):
        pg = pt_ref[req, chunk * group + g]
        rows = pl.ds(g * page, page)
        return (pltpu.make_async_copy(lat_hbm.at[pg], lat_bufs[slot].at[rows], sem.at[0, slot]),
                pltpu.make_async_copy(rope_hbm.at[pg], rope_bufs[slot].at[g], sem.at[1, slot]))

    def start_chunk(req, chunk, slot):
        for g in range(group):
            for cp in page_copies(req, chunk, slot, g):
                cp.start()

    def wait_chunk(req, chunk, slot):
        for g in range(group):
            for cp in page_copies(req, chunk, slot, g):
                cp.wait()

    @pl.when(r == 0)
    def _prime():
        for c in range(PAGED_AHEAD):
            start_chunk(0, c, c)

    q_lat = jnp.concatenate([qlat_ref[:, h * kv_lora:(h + 1) * kv_lora] for h in range(heads)], axis=0)
    q_rope = jnp.concatenate([qrope_ref[:, h * LANES:(h + 1) * LANES] for h in range(heads)],
                             axis=0)[:, :rope]

    def scores(lat, rp, rp_dims):
        return (lax.dot_general(q_lat.astype(lat.dtype), lat, NT, preferred_element_type=F32)
                + lax.dot_general(q_rope.astype(rp.dtype), rp, rp_dims, preferred_element_type=F32))

    def partial_softmax(s, lat):
        m = jnp.max(s, axis=1, keepdims=True)
        p = jnp.exp(s - m)
        return m, jnp.sum(p, axis=1, keepdims=True), _dot(p.astype(lat.dtype), lat)

    def combine(carry, parts):
        m_prev, l_prev, acc = carry
        m_new = m_prev
        for m, _, _ in parts:
            m_new = jnp.maximum(m_new, m)
        alpha = jnp.exp(m_prev - m_new)
        l_new, acc = alpha * l_prev, alpha * acc
        for m, l, a in parts:
            w = jnp.exp(m - m_new)
            l_new, acc = l_new + w * l, acc + w * a
        return m_new, l_new, acc

    def chunk_step(c, slot, carry):
        wait_chunk(r, c, slot)
        wrap = jnp.where(c + PAGED_AHEAD >= n_chunks, 1, 0)
        start_chunk(jnp.minimum(r + wrap, n_req - 1), c + PAGED_AHEAD - wrap * n_chunks,
                    (slot + PAGED_AHEAD) % PAGED_SLOTS)
        pages = block_keys // page
        lat_block = lambda blk: lat_bufs[slot][blk * block_keys:(blk + 1) * block_keys].astype(BF16)
        all_scores = []
        for blk in range(group // pages):
            rope_t = jnp.concatenate([rope_bufs[slot][blk * pages + g] for g in range(pages)], axis=1)
            all_scores.append(scores(lat_block(blk), rope_t.astype(BF16), NN))
        return combine(carry, [partial_softmax(s, lat_block(blk)) for blk, s in enumerate(all_scores)])

    def body(it, carry):
        for slot in range(PAGED_SLOTS):
            carry = chunk_step(PAGED_SLOTS * it + slot, slot, carry)
        return carry

    rows = heads * t
    init = (jnp.full((rows, 1), -jnp.inf, F32), jnp.zeros((rows, 1), F32), jnp.zeros((rows, kv_lora), F32))
    carry = lax.fori_loop(0, n_chunks // PAGED_SLOTS, body, init)

    @pl.when(r == n_req - 1)
    def _drain():
        for c in range(PAGED_AHEAD):
            wait_chunk(r, c, c)

    lat_new = ckv_ref[...]
    s_new = scores(lat_new, kpe_ref[...], NT)
    tok = lax.broadcasted_iota(jnp.int32, (rows, t), 0) % t
    key = lax.broadcasted_iota(jnp.int32, (rows, t), 1)
    s_new = jnp.where(key <= tok, s_new, -jnp.inf)
    _, l_fin, acc = combine(carry, [partial_softmax(s_new, lat_new)])
    o = acc / l_fin
    for h in range(heads):
        o_ref[:, h * kv_lora:(h + 1) * kv_lora] = o[h * t:(h + 1) * t].astype(o_ref.dtype)


def _hgrn_project(h_ref, wb_ref, lb_ref, g_scr, k_scr, q_scr, v_scr, fdim, vdim):
    z = _dot(h_ref[...], wb_ref[...])
    l0 = lb_ref[0:1, :]
    l1 = lb_ref[1:2, :]
    mx = jnp.maximum(l0, l1)
    e0 = jnp.exp(l0 - mx)
    e1 = jnp.exp(l1 - mx)
    lb = e1 / (e0 + e1)
    f = lb + (1.0 - lb) * jax.nn.sigmoid(z[:, :fdim])
    g_scr[...] = jnp.log(f)
    k_scr[...] = 1.0 - f
    q_scr[...] = _silu(z[:, fdim:2 * fdim])
    v_scr[...] = z[:, 2 * fdim:2 * fdim + vdim]
    return z[:, 2 * fdim + vdim:]


def _cumsum_rows(x):
    row = lax.broadcasted_iota(jnp.int32, x.shape, 0)
    shift = 1
    while shift < x.shape[0]:
        x = x + jnp.where(row >= shift, pltpu.roll(x, shift, 0), 0.0)
        shift *= 2
    return x


def _hgrn_chunk(qq, kk, g, v, st_ref, o_ref, rows, *, sub):
    c = qq.shape[0]
    n_pairs = st_ref.shape[0]
    hv = st_ref.shape[1] // 2
    hk = st_ref.shape[2] // 2
    mm = BF16 if sub % BF16_ROWS == 0 else F32
    bcum = _cumsum_rows(g)
    last = bcum[c - 1:c, :]
    q_in = (qq * jnp.exp(bcum)).astype(mm)
    k_dec = (kk * jnp.exp(last - bcum)).astype(mm)
    v_mm = v.astype(mm)
    n_sub = c // sub
    row_blk = lax.broadcasted_iota(jnp.int32, (c, hk), 0) // sub
    causal = lax.broadcasted_iota(jnp.int32, (c, c), 0) >= lax.broadcasted_iota(jnp.int32, (c, c), 1)
    lane_v = lax.broadcasted_iota(jnp.int32, (1, 2 * hv), 1) // hv
    bd = (lax.broadcasted_iota(jnp.int32, (2 * hv, 2 * hk), 0) // hv
          == lax.broadcasted_iota(jnp.int32, (2 * hv, 2 * hk), 1) // hk)
    def head_scores(hl):
        b_h, q_h = bcum[:, hl], qq[:, hl]
        refs = [b_h[j * sub + sub // 2:j * sub + sub // 2 + 1] for j in range(n_sub)]
        ref_rows = jnp.concatenate([jnp.broadcast_to(rj, (sub, hk)) for rj in refs], axis=0)
        k_all = kk[:, hl] * jnp.exp(ref_rows - b_h)
        k_cat = jnp.concatenate([jnp.where(row_blk == j, k_all, 0.0).astype(mm) for j in range(n_sub)], axis=1)
        q_cat = jnp.concatenate(
            [jnp.concatenate([jnp.zeros((j * sub, hk), F32)] * (j > 0)
                             + [q_h[j * sub:] * jnp.exp(b_h[j * sub:] - refs[j])], axis=0).astype(mm)
             for j in range(n_sub)], axis=1)
        att = lax.dot_general(q_cat, k_cat, NT, preferred_element_type=F32)
        return jnp.where(causal, att, 0.0).astype(mm)

    atts = [head_scores(slice(h * hk, (h + 1) * hk)) for h in range(2 * n_pairs)]
    for p in range(n_pairs):
        pk = slice(p * 2 * hk, (p + 1) * 2 * hk)
        pv = slice(p * 2 * hv, (p + 1) * 2 * hv)
        vp = v_mm[:, pv]
        st = st_ref[p]
        o = lax.dot_general(q_in[:, pk], st.astype(mm), NT, preferred_element_type=F32)
        for e in range(2):
            o = o + _dot(atts[2 * p + e], jnp.where(lane_v == e, vp, jnp.zeros_like(vp)))
        o_ref[rows, pv] = o
        upd = lax.dot_general(vp, k_dec[:, pk], TN, preferred_element_type=F32)
        st_ref[p] = st * jnp.exp(last[:, pk]) + jnp.where(bd, upd, 0.0)


def _hgrn_finish(o, zb, gho_ref, avg_ref, out_ref):
    ms = _dot((o * o).astype(BF16), avg_ref[...])
    out_ref[...] = (o * lax.rsqrt(ms + EPS) * gho_ref[...] * _silu(zb)).astype(out_ref.dtype)


def _compact_state(st):
    hv = st.shape[0] // 2
    hk = st.shape[1] // 2
    return jnp.concatenate([st[:hv, :hk], st[hv:, hk:]], axis=0)


def _expand_state(sc):
    hv = sc.shape[0] // 2
    z = jnp.zeros((hv, sc.shape[1]), F32)
    return jnp.concatenate([jnp.concatenate([sc[:hv], z], axis=1),
                            jnp.concatenate([z, sc[hv:]], axis=1)], axis=0)


def _hgrn_prompt_kernel(h_ref, wb_ref, lb_ref, gho_ref, avg_ref, out_ref, sfin_ref,
                        st_scr, g_scr, k_scr, q_scr, v_scr, o_scr, *, chunk, sub, fdim, vdim):
    tb = pl.program_id(1)

    @pl.when(tb == 0)
    def _init():
        st_scr[...] = jnp.zeros(st_scr.shape, F32)

    zb = _hgrn_project(h_ref, wb_ref, lb_ref, g_scr, k_scr, q_scr, v_scr, fdim, vdim)

    def body(ci, carry):
        rows = pl.ds(pl.multiple_of(ci * chunk, chunk), chunk)
        _hgrn_chunk(q_scr[rows, :], k_scr[rows, :], g_scr[rows, :], v_scr[rows, :], st_scr, o_scr, rows,
                    sub=sub)
        return carry

    lax.fori_loop(0, h_ref.shape[0] // chunk, body, 0, unroll=True)
    _hgrn_finish(o_scr[...], zb, gho_ref, avg_ref, out_ref)

    @pl.when(tb == pl.num_programs(1) - 1)
    def _final():
        for p in range(st_scr.shape[0]):
            sfin_ref[0, p] = _compact_state(st_scr[p])


def _hgrn_sample_kernel(h_ref, wb_ref, lb_ref, gho_ref, avg_ref, s0_ref, out_ref, sfin_ref,
                        st_scr, g_scr, k_scr, q_scr, v_scr, o_scr, *, t, fdim, vdim):
    zb = _hgrn_project(h_ref, wb_ref, lb_ref, g_scr, k_scr, q_scr, v_scr, fdim, vdim)

    for ri in range(s0_ref.shape[0]):
        st_ref = st_scr.at[ri]
        for p in range(st_ref.shape[0]):
            st_ref[p] = _expand_state(s0_ref[ri, p])
        rows = pl.ds(ri * t, t)
        _hgrn_chunk(q_scr[rows, :], k_scr[rows, :], g_scr[rows, :], v_scr[rows, :], st_ref, o_scr, rows,
                    sub=t)
        for p in range(st_ref.shape[0]):
            sfin_ref[ri, p] = _compact_state(st_ref[p])
    _hgrn_finish(o_scr[...], zb, gho_ref, avg_ref, out_ref)


def _xattn_heads(q, zc, head_k, head_v, heads, hdim):
    outs = []
    all_scores = [lax.dot_general(q[:, h * hdim:(h + 1) * hdim], head_k(h), NT, preferred_element_type=F32)
                  for h in range(heads)]
    for h, s in enumerate(all_scores):
        mv = head_v(h)
        s = s - jnp.max(s, axis=1, keepdims=True)
        p = jnp.exp(s)
        p = (p / jnp.sum(p, axis=1, keepdims=True)).astype(mv.dtype)
        outs.append(_dot(p, mv))
    return jnp.concatenate(outs, axis=1) * _silu(zc)


def _xattn_prompt_kernel(h_ref, wc_ref, mk_ref, mv_ref, out_ref, *, heads, hdim, scale):
    z = _dot(h_ref[0], wc_ref[...])
    w = heads * hdim
    q = (z[:, :w] * scale).astype(BF16)
    mk = mk_ref[0].astype(BF16)
    mv = mv_ref[0].astype(BF16)
    out_ref[0] = _xattn_heads(q, z[:, w:], lambda h: mk[:, h * hdim:(h + 1) * hdim],
                              lambda h: mv[:, h * hdim:(h + 1) * hdim], heads, hdim).astype(out_ref.dtype)


def _xattn_sample_kernel(h_ref, wc_ref, mk_ref, mv_ref, out_ref, *, heads, hdim, scale, t):
    z = _dot(h_ref[...], wc_ref[...])
    w = heads * hdim
    q = z[:, :w] * scale
    zc = z[:, w:]
    n_rows = mk_ref.shape[1]
    row_head = lax.broadcasted_iota(jnp.int32, (heads * t, n_rows), 0) // t
    col_head = lax.broadcasted_iota(jnp.int32, (heads * t, n_rows), 1) % heads
    all_scores = []
    for ri in range(mk_ref.shape[0]):
        qs = jnp.concatenate([q[ri * t:(ri + 1) * t, h * hdim:(h + 1) * hdim] for h in range(heads)], axis=0)
        all_scores.append(lax.dot_general(qs.astype(BF16), mk_ref[ri].astype(BF16), NT,
                                          preferred_element_type=F32))
    for ri, s in enumerate(all_scores):
        rows = slice(ri * t, (ri + 1) * t)
        s = jnp.where(row_head == col_head, s, -jnp.inf)
        p = jnp.exp(s - jnp.max(s, axis=1, keepdims=True))
        p = (p / jnp.sum(p, axis=1, keepdims=True)).astype(BF16)
        o = _dot(p, mv_ref[ri].astype(BF16))
        o = jnp.concatenate([o[h * t:(h + 1) * t] for h in range(heads)], axis=1)
        out_ref[rows, :] = (o * _silu(zc[rows])).astype(out_ref.dtype)


def _memkv_kernel(m_ref, g_ref, w_ref, k_ref, v_ref):
    kv = _dot(_rms(m_ref[...], g_ref[...]).astype(BF16), w_ref[...])
    w = kv.shape[1] // 2
    k_ref[...] = kv[:, :w]
    v_ref[...] = kv[:, w:]


def _merge_kernel(x_ref, h_ref, oa_ref, ob_ref, oc_ref, wuv_ref, wm_ref, wbr_ref, wout_ref, gf_ref, y_ref, *,
                  width, from_latent):
    h = h_ref[...]
    d = x_ref.shape[1]
    oa = oa_ref[...]
    if from_latent:
        oa = _dot(oa.astype(BF16), wuv_ref[...])
    else:
        oa = oa.astype(F32)
    za = _dot(h, wm_ref[:, :width])
    branches = ((oa * _silu(za)).astype(BF16), ob_ref[...].astype(BF16), oc_ref[...].astype(BF16))
    merged = jnp.zeros((x_ref.shape[0], d), F32)
    for n, o in enumerate(branches):
        gate = jax.nn.sigmoid(_dot(h, wm_ref[:, width + n * d:width + (n + 1) * d]))
        merged = merged + gate * _dot(o, wbr_ref[n])
    out = x_ref[...] + _dot(merged.astype(BF16), wout_ref[...])
    y_ref[...] = _rms(out, gf_ref[...])


def _rope_tables(pos, rope, lead):
    half = rope // 2
    inv = jnp.exp(-math.log(ROPE_BASE) * jnp.arange(half, dtype=F32) / half)
    ang = pos.astype(F32)[:, None] * inv[None, :]
    n = pos.shape[0]
    pad = jnp.zeros((n, LANES - lead - rope), F32)
    cos = jnp.concatenate([jnp.ones((n, lead), F32), jnp.cos(ang), jnp.cos(ang), pad], axis=1)
    sin_signed = jnp.concatenate([jnp.zeros((n, lead), F32), -jnp.sin(ang), jnp.sin(ang), pad], axis=1)
    return cos, sin_signed


def _pad_cols(w, width):
    return jnp.pad(w, [(0, 0)] * (w.ndim - 1) + [(0, width - w.shape[-1])])


def _block_diag(blocks):
    rows = sum(b.shape[0] for b in blocks)
    cols = sum(b.shape[1] for b in blocks)
    out = jnp.zeros((rows, cols), blocks[0].dtype)
    r = c = 0
    for b in blocks:
        out = lax.dynamic_update_slice(out, b, (r, c))
        r += b.shape[0]
        c += b.shape[1]
    return out


def _tile_rows(n, pref):
    t = min(n, pref)
    assert n % t == 0
    return t


def kernel(x_prompt, x_sample, mem_prompt, cache_kv_latent, cache_k_rope, page_table, state_hgrn, cache_mem_k, cache_mem_v, g_norm, w_in, g_q_lora, w_uq, g_kv_lora, w_uk, w_uv, lb_logits, g_hgrn_out, g_mem_norm, w_mem_kv, w_branch, w_out, g_final):
    depth = w_in.shape[0]
    assert depth == 1, "one layer per step"
    b, t, d = x_prompt.shape
    nb, ts, _ = x_sample.shape
    n_mem = mem_prompt.shape[1]
    n_phys, page, kv_lora = cache_kv_latent.shape[1:]
    rope = cache_k_rope.shape[-1]
    n_pages = page_table.shape[1]
    q_lora = g_q_lora.shape[-1]
    a_heads, a_qk = w_uq.shape[2:]
    a_nope = w_uk.shape[-1]
    a_vdim = w_uv.shape[-1]
    b_heads, b_expand, b_vdim = state_hgrn.shape[2:]
    fdim = b_heads * b_expand
    c_heads, c_hdim = cache_mem_k.shape[3:]
    width = w_branch.shape[2]
    assert a_qk == a_nope + rope and a_heads * a_vdim == width and b_heads * b_vdim == width
    assert c_heads * c_hdim == width and a_qk <= LANES and 2 * a_vdim == LANES and 2 * b_vdim == LANES
    assert b_expand == LANES and a_heads % 2 == 0 and b_heads % 2 == 0

    splits = (q_lora, kv_lora, rope, width, fdim, fdim, width, width, width, width, N_BRANCH * d)
    offs = np.concatenate([[0], np.cumsum(splits)])
    assert offs[-1] == w_in.shape[-1]
    w_in_t = jnp.swapaxes(w_in, 1, 2).reshape(w_in.shape[-1], d)

    def in_proj(*pieces, pad_to=None):
        rows = [w_in_t[offs[i]:offs[i + 1]] for i in pieces]
        if pad_to is not None:
            rows.append(jnp.zeros((pad_to - sum(r.shape[0] for r in rows), d), F32))
        return jnp.concatenate(rows, axis=0).astype(BF16).T

    a_scale = a_qk ** -0.5
    n_p = b * t
    n_s = nb * ts
    xp = x_prompt.reshape(n_p, d)
    xs = x_sample.reshape(n_s, d)
    row = lambda g: g.reshape(1, -1).astype(F32)

    w_a = in_proj(0, 1, 2, pad_to=q_lora + kv_lora + LANES)
    uq = w_uq.reshape(q_lora, a_heads, a_qk)
    uq_swapped = jnp.concatenate([jnp.zeros_like(uq[..., :a_nope]), uq[..., a_nope + rope // 2:],
                                  uq[..., a_nope:a_nope + rope // 2]], axis=-1)
    wq_prompt = jnp.concatenate([_pad_cols(uq, LANES).reshape(q_lora, -1),
                                 _pad_cols(uq_swapped, LANES).reshape(q_lora, -1)], axis=1).astype(BF16)
    wq_sample = jnp.concatenate([uq[..., :a_nope].reshape(q_lora, -1),
                                 _pad_cols(uq[..., a_nope:], LANES).reshape(q_lora, -1)], axis=1).astype(BF16)
    uk = w_uk.reshape(kv_lora, a_heads, a_nope)
    uv = w_uv.reshape(kv_lora, a_heads, a_vdim)
    uv_pad = jnp.stack([jnp.pad(uv[:, h], ((0, 0), ((h % 2) * a_vdim, LANES - a_vdim - (h % 2) * a_vdim)))
                        for h in range(a_heads)], axis=1)
    w_kv = jnp.concatenate([_pad_cols(uk, LANES).reshape(kv_lora, -1), uv_pad.reshape(kv_lora, -1)],
                           axis=1).astype(BF16)
    w_abs = _block_diag([uk[:, h].T for h in range(a_heads)]).astype(BF16)
    w_uv_bd = _block_diag([uv[:, h] for h in range(a_heads)]).astype(BF16)
    w_b = in_proj(4, 5, 6, 7)
    w_c = in_proj(8, 9)
    w_m = in_proj(3, 10)
    w_br = w_branch[0].astype(BF16)
    w_o = w_out[0].astype(BF16)
    w_mem = w_mem_kv[0].astype(BF16)
    gho = jnp.tile(g_hgrn_out[0], b_heads).reshape(1, width).astype(F32)
    head_of = jnp.arange(width) // b_vdim
    avg = ((head_of[:, None] == head_of[None, :]).astype(F32) / b_vdim).astype(BF16)
    lb2 = lb_logits[:2].astype(F32)

    n_past = n_pages * page
    cq_p, sq_p = _rope_tables(jnp.arange(t), rope, a_nope)
    ck_p, sk_p = _rope_tables(jnp.arange(t), rope, 0)
    tm_s = _tile_rows(n_s, 256)
    assert tm_s % ts == 0
    ck_s, sk_s = _rope_tables(n_past + (jnp.arange(tm_s) % ts), rope, 0)

    tm = _tile_rows(t, 1024)
    n_t = t // tm
    mla_common = dict(q_lora=q_lora, kv_lora=kv_lora, rope=rope)
    rowspec = lambda w: pl.BlockSpec((tm, w), lambda i: (i, 0))
    tabspec = pl.BlockSpec((tm, LANES), lambda i: (i % n_t, 0))
    hq = a_heads * LANES
    h_p, q_p, k_p, v_p, ckv_p, kpe_p = pl.pallas_call(
        functools.partial(_mla_prep_prompt_kernel, scale=a_scale * math.log2(math.e), nope=a_nope,
                          **mla_common),
        grid=(n_p // tm,),
        in_specs=[rowspec(d), _const_spec((1, d)), _const_spec(w_a.shape), _const_spec((1, q_lora)),
                  _const_spec(wq_prompt.shape), _const_spec((1, kv_lora)), _const_spec(w_kv.shape),
                  tabspec, tabspec, tabspec, tabspec],
        out_specs=[rowspec(d), rowspec(hq), rowspec(hq), rowspec(hq), rowspec(kv_lora),
                   pl.BlockSpec((1, rope, tm), lambda i: (i // n_t, 0, i % n_t))],
        out_shape=[jax.ShapeDtypeStruct((n_p, d), BF16), jax.ShapeDtypeStruct((n_p, hq), BF16),
                   jax.ShapeDtypeStruct((n_p, hq), BF16), jax.ShapeDtypeStruct((n_p, hq), BF16),
                   jax.ShapeDtypeStruct((n_p, kv_lora), F32), jax.ShapeDtypeStruct((b, rope, t), F32)],
        compiler_params=_params(("parallel",)),
    )(xp, row(g_norm), w_a, row(g_q_lora), wq_prompt, row(g_kv_lora), w_kv, cq_p, sq_p, ck_p, sk_p)

    srow = lambda w: pl.BlockSpec((tm_s, w), lambda i: (i, 0))
    hl = a_heads * kv_lora
    h_s, qlat_s, qrope_s, ckv_s, kpe_s = pl.pallas_call(
        functools.partial(_mla_prep_sample_kernel, scale=a_scale, nope_w=a_heads * a_nope, **mla_common),
        grid=(n_s // tm_s,),
        in_specs=[srow(d), _const_spec((1, d)), _const_spec(w_a.shape), _const_spec((1, q_lora)),
                  _const_spec(wq_sample.shape), _const_spec((1, kv_lora)), _const_spec(w_abs.shape),
                  _const_spec((tm_s, LANES)), _const_spec((tm_s, LANES))],
        out_specs=[srow(d), srow(hl), srow(hq), srow(kv_lora), srow(rope)],
        out_shape=[jax.ShapeDtypeStruct((n_s, d), BF16), jax.ShapeDtypeStruct((n_s, hl), F32),
                   jax.ShapeDtypeStruct((n_s, hq), F32), jax.ShapeDtypeStruct((n_s, kv_lora), F32),
                   jax.ShapeDtypeStruct((n_s, rope), F32)],
        compiler_params=_params(("parallel",)),
    )(xs, row(g_norm), w_a, row(g_q_lora), wq_sample, row(g_kv_lora), w_abs, ck_s, sk_s)

    tq = _tile_rows(t, 512)
    assert tq % LANES == 0
    nq = t // tq
    pairs = [(i, j) for i in range(nq) for j in range(i + 1)]
    i_tab = jnp.asarray([p[0] for p in pairs], jnp.int32)
    j_tab = jnp.asarray([p[1] for p in pairs], jnp.int32)
    qkv = lambda a: a.reshape(b, t, hq)
    oa_p = pl.pallas_call(
        functools.partial(_flash_kernel, heads=a_heads),
        grid_spec=pltpu.PrefetchScalarGridSpec(
            num_scalar_prefetch=2, grid=(b, len(pairs)),
            in_specs=[pl.BlockSpec((1, tq, hq), lambda bi, s, it, jt: (bi, it[s], 0)),
                      pl.BlockSpec((1, tq, hq), lambda bi, s, it, jt: (bi, jt[s], 0)),
                      pl.BlockSpec((1, tq, hq), lambda bi, s, it, jt: (bi, jt[s], 0))],
            out_specs=pl.BlockSpec((1, tq, width), lambda bi, s, it, jt: (bi, it[s], 0)),
            scratch_shapes=[pltpu.VMEM((a_heads, tq, LANES), F32)] * 2),
        out_shape=jax.ShapeDtypeStruct((b, t, width), BF16),
        compiler_params=_params(("parallel", "arbitrary")),
    )(i_tab, j_tab, qkv(q_p), qkv(k_p), qkv(v_p)).reshape(n_p, width)

    assert n_pages % PAGED_SLOTS == 0
    group = math.gcd(n_pages // PAGED_SLOTS, 32)
    n_chunks = n_pages // group
    block_keys = math.gcd(group, 8) * page
    req = lambda w: pl.BlockSpec((ts, w), lambda r, pt: (r, 0))
    olat_s = pl.pallas_call(
        functools.partial(_paged_kernel, heads=a_heads, group=group, page=page, n_chunks=n_chunks,
                          block_keys=block_keys,
                          kv_lora=kv_lora, rope=rope),
        grid_spec=pltpu.PrefetchScalarGridSpec(
            num_scalar_prefetch=1, grid=(nb,),
            in_specs=[req(hl), req(hq), req(kv_lora), req(rope),
                      pl.BlockSpec(memory_space=pl.ANY), pl.BlockSpec(memory_space=pl.ANY)],
            out_specs=req(hl),
            scratch_shapes=[pltpu.VMEM((group * page, kv_lora), F32)] * PAGED_SLOTS
            + [pltpu.VMEM((group, rope, page), F32)] * PAGED_SLOTS
            + [pltpu.SemaphoreType.DMA((2, PAGED_SLOTS))]),
        out_shape=jax.ShapeDtypeStruct((n_s, hl), F32),
        compiler_params=_params(("arbitrary",)),
    )(page_table, qlat_s, qrope_s, ckv_s, kpe_s, cache_kv_latent.reshape(n_phys, page, kv_lora),
      jnp.swapaxes(cache_k_rope, 2, 3).reshape(n_phys, rope, page))

    n_pairs = b_heads // 2
    tb = _tile_rows(t, 256)
    chunk = _tile_rows(tb, 64)
    sub = _tile_rows(chunk, 16)
    wb_cols = w_b.shape[1]
    hgrn_scratch = lambda rows, *lead: [pltpu.VMEM(lead + (n_pairs, LANES, 2 * b_expand), F32),
                                 pltpu.VMEM((rows, fdim), F32), pltpu.VMEM((rows, fdim), F32),
                                 pltpu.VMEM((rows, fdim), F32), pltpu.VMEM((rows, width), F32),
                                 pltpu.VMEM((rows, width), F32)]
    ob_p, st_p = pl.pallas_call(
        functools.partial(_hgrn_prompt_kernel, chunk=chunk, sub=sub, fdim=fdim, vdim=width),
        grid=(b, t // tb),
        in_specs=[pl.BlockSpec((tb, d), lambda bi, ti: (bi * (t // tb) + ti, 0)),
                  _const_spec((d, wb_cols)), _const_spec((2, fdim)), _const_spec((1, width)),
                  _const_spec((width, width))],
        out_specs=[pl.BlockSpec((tb, width), lambda bi, ti: (bi * (t // tb) + ti, 0)),
                   pl.BlockSpec((1, n_pairs, LANES, b_expand), lambda bi, ti: (bi, 0, 0, 0))],
        out_shape=[jax.ShapeDtypeStruct((n_p, width), BF16),
                   jax.ShapeDtypeStruct((b, n_pairs, LANES, b_expand), F32)],
        scratch_shapes=hgrn_scratch(tb),
        compiler_params=_params(("parallel", "arbitrary")),
    )(h_p, w_b, lb2, gho, avg)

    def to_pairs(s):
        n = s.shape[0]
        return s.reshape(n, n_pairs, 2, b_expand, b_vdim).transpose(0, 1, 2, 4, 3).reshape(
            n, n_pairs, LANES, b_expand)

    def from_pairs(s):
        n = s.shape[0]
        return s.reshape(n, n_pairs, 2, b_vdim, b_expand).transpose(0, 1, 2, 4, 3).reshape(
            n, b_heads, b_expand, b_vdim)

    rq = _tile_rows(nb, 8)
    ob_s, st_s = pl.pallas_call(
        functools.partial(_hgrn_sample_kernel, t=ts, fdim=fdim, vdim=width),
        grid=(nb // rq,),
        in_specs=[pl.BlockSpec((rq * ts, d), lambda i: (i, 0)),
                  _const_spec((d, wb_cols)), _const_spec((2, fdim)), _const_spec((1, width)),
                  _const_spec((width, width)),
                  pl.BlockSpec((rq, n_pairs, LANES, b_expand), lambda i: (i, 0, 0, 0))],
        out_specs=[pl.BlockSpec((rq * ts, width), lambda i: (i, 0)),
                   pl.BlockSpec((rq, n_pairs, LANES, b_expand), lambda i: (i, 0, 0, 0))],
        out_shape=[jax.ShapeDtypeStruct((n_s, width), BF16),
                   jax.ShapeDtypeStruct((nb, n_pairs, LANES, b_expand), F32)],
        scratch_shapes=hgrn_scratch(rq * ts, rq),
        compiler_params=_params(("parallel",)),
    )(h_s, w_b, lb2, gho, avg, to_pairs(state_hgrn.reshape(nb, b_heads, b_expand, b_vdim)))

    n_m = b * n_mem
    tmm = _tile_rows(n_m, 512)
    mk_p, mv_p = pl.pallas_call(
        _memkv_kernel,
        grid=(n_m // tmm,),
        in_specs=[pl.BlockSpec((tmm, d), lambda i: (i, 0)), _const_spec((1, d)), _const_spec(w_mem.shape)],
        out_specs=[pl.BlockSpec((tmm, width), lambda i: (i, 0)), pl.BlockSpec((tmm, width), lambda i: (i, 0))],
        out_shape=[jax.ShapeDtypeStruct((n_m, width), F32), jax.ShapeDtypeStruct((n_m, width), F32)],
        compiler_params=_params(("parallel",)),
    )(mem_prompt.reshape(n_m, d), row(g_mem_norm), w_mem)

    c_scale = c_hdim ** -0.5
    tx = _tile_rows(t, 1024)
    memspec = pl.BlockSpec((1, n_mem, width), lambda bi, ti: (bi, 0, 0))
    oc_p = pl.pallas_call(
        functools.partial(_xattn_prompt_kernel, heads=c_heads, hdim=c_hdim, scale=c_scale),
        grid=(b, t // tx),
        in_specs=[pl.BlockSpec((1, tx, d), lambda bi, ti: (bi, ti, 0)), _const_spec(w_c.shape), memspec, memspec],
        out_specs=pl.BlockSpec((1, tx, width), lambda bi, ti: (bi, ti, 0)),
        out_shape=jax.ShapeDtypeStruct((b, t, width), BF16),
        compiler_params=_params(("parallel", "parallel")),
    )(h_p.reshape(b, t, d), w_c, mk_p.reshape(b, n_mem, width), mv_p.reshape(b, n_mem, width)).reshape(n_p, width)

    rx = _tile_rows(nb, 8)
    smem = pl.BlockSpec((rx, n_mem * c_heads, c_hdim), lambda i: (i, 0, 0))
    oc_s = pl.pallas_call(
        functools.partial(_xattn_sample_kernel, heads=c_heads, hdim=c_hdim, scale=c_scale, t=ts),
        grid=(nb // rx,),
        in_specs=[pl.BlockSpec((rx * ts, d), lambda i: (i, 0)), _const_spec(w_c.shape), smem, smem],
        out_specs=pl.BlockSpec((rx * ts, width), lambda i: (i, 0)),
        out_shape=jax.ShapeDtypeStruct((n_s, width), F32),
        compiler_params=_params(("parallel",)),
    )(h_s, w_c, cache_mem_k.reshape(nb, n_mem * c_heads, c_hdim), cache_mem_v.reshape(nb, n_mem * c_heads, c_hdim))

    def merge(x2, h2, oa, ob, oc, from_latent):
        n = x2.shape[0]
        tmg = _tile_rows(n, 512)
        rs = lambda w: pl.BlockSpec((tmg, w), lambda i: (i, 0))
        return pl.pallas_call(
            functools.partial(_merge_kernel, width=width, from_latent=from_latent),
            grid=(n // tmg,),
            in_specs=[rs(d), rs(d), rs(oa.shape[1]), rs(width), rs(width), _const_spec(w_uv_bd.shape),
                      _const_spec(w_m.shape), _const_spec(w_br.shape), _const_spec(w_o.shape),
                      _const_spec((1, d))],
            out_specs=rs(d),
            out_shape=jax.ShapeDtypeStruct((n, d), F32),
            compiler_params=_params(("parallel",)),
        )(x2, h2, oa, ob, oc, w_uv_bd, w_m, w_br, w_o, row(g_final))

    y_p = merge(xp, h_p, oa_p, ob_p, oc_p, False).reshape(b, t, d)
    y_s = merge(xs, h_s, olat_s, ob_s, oc_s, True).reshape(nb, ts, d)

    return (y_p, y_s,
            ckv_p.reshape(1, b, t, kv_lora), jnp.swapaxes(kpe_p, 1, 2).reshape(1, b, t, rope),
            from_pairs(st_p)[None],
            mk_p.reshape(1, b, n_mem, c_heads, c_hdim), mv_p.reshape(1, b, n_mem, c_heads, c_hdim),
            ckv_s.reshape(1, nb, ts, kv_lora), kpe_s.reshape(1, nb, ts, rope),
            from_pairs(st_s)[None])
```

```python
import functools
import math

import numpy as np
import jax
import jax.numpy as jnp
from jax import lax
from jax.experimental import pallas as pl
from jax.experimental.pallas import tpu as pltpu

F32 = jnp.float32
BF16 = jnp.bfloat16
EPS = 1e-6
ROPE_BASE = 10000.0
N_BRANCH = 3
LANES = 128
BF16_ROWS = 16
PAGED_SLOTS = 4
PAGED_AHEAD = 3
VMEM_LIMIT = 56 * 1024 * 1024
NT = (((1,), (1,)), ((), ()))
TN = (((0,), (0,)), ((), ()))
NN = (((1,), (0,)), ((), ()))


def _params(sem):
    return pltpu.CompilerParams(dimension_semantics=sem, vmem_limit_bytes=VMEM_LIMIT)


def _rms(x, g):
    return x * lax.rsqrt(jnp.mean(x * x, axis=-1, keepdims=True) + EPS) * g


def _silu(x):
    return x * jax.nn.sigmoid(x)


def _dot(a, b):
    return jnp.dot(a, b, preferred_element_type=F32)


def _const_spec(shape):
    nd = len(shape)
    return pl.BlockSpec(shape, lambda *_: (0,) * nd)


def _rotate(x, cos, sin_signed, half):
    n = x.shape[1]
    first_half = (lax.broadcasted_iota(jnp.int32, x.shape, 1) // half) % 2 == 0
    partner = jnp.where(first_half, pltpu.roll(x, n - half, 1), pltpu.roll(x, half, 1))
    reps = n // cos.shape[1]
    return x * jnp.tile(cos, (1, reps)) + partner * jnp.tile(sin_signed, (1, reps))


def _mla_common(x_ref, gn_ref, wa_ref, gq_ref, gkv_ref, ck_ref, sk_ref, h_ref, ckv_ref, kpe_ref,
                q_lora, kv_lora, rope, kpe_tokens_on_lanes=False):
    h = _rms(x_ref[...], gn_ref[...]).astype(BF16)
    h_ref[...] = h
    z = _dot(h, wa_ref[...])
    cq_n = _rms(z[:, :q_lora], gq_ref[...]).astype(BF16)
    ckv = _rms(z[:, q_lora:q_lora + kv_lora], gkv_ref[...])
    ckv_ref[...] = ckv
    o = q_lora + kv_lora
    kpe = _rotate(z[:, o:o + LANES], ck_ref[...], sk_ref[...], rope // 2)
    if kpe_tokens_on_lanes:
        kpe_ref[0] = kpe.T[:rope, :]
    else:
        kpe_ref[...] = kpe[:, :rope]
    return cq_n, ckv, kpe


def _mla_prep_prompt_kernel(x_ref, gn_ref, wa_ref, gq_ref, wq_ref, gkv_ref, wkv_ref, cq_ref, sq_ref,
                            ck_ref, sk_ref, h_ref, q_ref, k_ref, v_ref, ckv_ref, kpe_ref, *,
                            scale, q_lora, kv_lora, rope, nope):
    cq_n, ckv, kpe = _mla_common(x_ref, gn_ref, wa_ref, gq_ref, gkv_ref, ck_ref, sk_ref, h_ref,
                                 ckv_ref, kpe_ref, q_lora, kv_lora, rope, kpe_tokens_on_lanes=True)
    q2 = _dot(cq_n, wq_ref[...])
    w = q2.shape[1] // 2
    reps = w // LANES
    q_ref[...] = ((q2[:, :w] * jnp.tile(cq_ref[...], (1, reps))
                   + q2[:, w:] * jnp.tile(sq_ref[...], (1, reps))) * scale).astype(BF16)
    kv = _dot(ckv.astype(BF16), wkv_ref[...])
    w = kv.shape[1] // 2
    heads = w // LANES
    k_ref[...] = (kv[:, :w] + jnp.tile(pltpu.roll(kpe, nope, 1), (1, heads))).astype(BF16)
    lane = lax.broadcasted_iota(jnp.int32, (1, w), 1)
    sum_lane = jnp.where((lane // LANES) % 2 == 0, _sum_lane(0), _sum_lane(1))
    v_ref[...] = jnp.where(lane % LANES == sum_lane, 1.0, kv[:, w:]).astype(BF16)


def _mla_prep_sample_kernel(x_ref, gn_ref, wa_ref, gq_ref, wq_ref, gkv_ref, wabs_ref, ck_ref, sk_ref,
                            h_ref, qlat_ref, qrope_ref, ckv_ref, kpe_ref, *,
                            scale, q_lora, kv_lora, rope, nope_w):
    cq_n, _, _ = _mla_common(x_ref, gn_ref, wa_ref, gq_ref, gkv_ref, ck_ref, sk_ref, h_ref,
                             ckv_ref, kpe_ref, q_lora, kv_lora, rope)
    q2 = _dot(cq_n, wq_ref[...])
    q_nope = (q2[:, :nope_w] * scale).astype(BF16)
    qlat_ref[...] = _dot(q_nope, wabs_ref[...])
    qrope_ref[...] = _rotate(q2[:, nope_w:], ck_ref[...], sk_ref[...], rope // 2) * scale


def _sum_lane(head):
    return 0 if head % 2 else LANES - 1


def _flash_kernel(it_ref, jt_ref, q_ref, k_ref, v_ref, o_ref, m_scr, acc_scr, *, heads):
    step = pl.program_id(1)
    i = it_ref[step]
    j = jt_ref[step]
    tq = q_ref.shape[1]

    @pl.when(j == 0)
    def _init():
        m_scr[...] = jnp.full(m_scr.shape, -jnp.inf, F32)
        acc_scr[...] = jnp.zeros(acc_scr.shape, F32)

    def update(masked):
        if masked:
            row = lax.broadcasted_iota(jnp.int32, (tq, tq), 0)
            col = lax.broadcasted_iota(jnp.int32, (tq, tq), 1)
            keep = row >= col
        def scores(h):
            sl = slice(h * LANES, (h + 1) * LANES)
            return lax.dot_general(q_ref[0, :, sl], k_ref[0, :, sl], NT, preferred_element_type=F32)

        ahead = 2
        pending = [scores(h) for h in range(ahead)]
        for h in range(heads):
            s = pending.pop(0)
            if h + ahead < heads:
                pending.append(scores(h + ahead))
            sl = slice(h * LANES, (h + 1) * LANES)
            if masked:
                s = jnp.where(keep, s, -jnp.inf)
            m_prev = m_scr[h]
            m_new = jnp.maximum(m_prev, jnp.max(s, axis=1, keepdims=True))
            p = jnp.exp2(s - jnp.tile(m_new, (1, tq // LANES)))
            acc_scr[h] = jnp.exp2(m_prev - m_new) * acc_scr[h] + _dot(p.astype(BF16), v_ref[0, :, sl])
            m_scr[h] = m_new

    @pl.when(j < i)
    def _off_diagonal():
        update(False)

    @pl.when(j == i)
    def _diagonal():
        update(True)
        low_half = lax.broadcasted_iota(jnp.int32, (tq, LANES), 1) < LANES // 2
        for p in range(heads // 2):
            even, odd = acc_scr[2 * p], acc_scr[2 * p + 1]
            l_even = even[:, _sum_lane(0):_sum_lane(0) + 1]
            l_odd = odd[:, _sum_lane(1):_sum_lane(1) + 1]
            o_ref[0, :, p * LANES:(p + 1) * LANES] = jnp.where(low_half, even / l_even,
                                                               odd / l_odd).astype(o_ref.dtype)


def _paged_kernel(pt_ref, qlat_ref, qrope_ref, ckv_ref, kpe_ref, lat_hbm, rope_hbm, o_ref,
                  *scratch, heads, group, page, n_chunks, block_keys, kv_lora, rope):
    r = pl.program_id(0)
    n_req = pl.num_programs(0)
    t = qlat_ref.shape[0]
    lat_bufs = scratch[:PAGED_SLOTS]
    rope_bufs = scratch[PAGED_SLOTS:2 * PAGED_SLOTS]
    sem = scratch[2 * PAGED_SLOTS]

    def page_copies(req, chunk, slot, g):
        pg = pt_ref[req, chunk * group + g]
        rows = pl.ds(g * page, page)
        return (pltpu.make_async_copy(lat_hbm.at[pg], lat_bufs[slot].at[rows], sem.at[0, slot]),
                pltpu.make_async_copy(rope_hbm.at[pg], rope_bufs[slot].at[g], sem.at[1, slot]))

    def start_chunk(req, chunk, slot):
        for g in range(group):
            for cp in page_copies(req, chunk, slot, g):
                cp.start()

    def wait_chunk(req, chunk, slot):
        for g in range(group):
            for cp in page_copies(req, chunk, slot, g):
                cp.wait()

    @pl.when(r == 0)
    def _prime():
        for c in range(PAGED_AHEAD):
            start_chunk(0, c, c)

    q_lat = jnp.concatenate([qlat_ref[:, h * kv_lora:(h + 1) * kv_lora] for h in range(heads)], axis=0)
    q_rope = jnp.concatenate([qrope_ref[:, h * LANES:(h + 1) * LANES] for h in range(heads)],
                             axis=0)[:, :rope]

    def scores(lat, rp, rp_dims):
        return (lax.dot_general(q_lat.astype(lat.dtype), lat, NT, preferred_element_type=F32)
                + lax.dot_general(q_rope.astype(rp.dtype), rp, rp_dims, preferred_element_type=F32))

    def partial_softmax(s, lat):
        m = jnp.max(s, axis=1, keepdims=True)
        p = jnp.exp(s - m)
        return m, jnp.sum(p, axis=1, keepdims=True), _dot(p.astype(lat.dtype), lat)

    def combine(carry, parts):
        m_prev, l_prev, acc = carry
        m_new = m_prev
        for m, _, _ in parts:
            m_new = jnp.maximum(m_new, m)
        alpha = jnp.exp(m_prev - m_new)
        l_new, acc = alpha * l_prev, alpha * acc
        for m, l, a in parts:
            w = jnp.exp(m - m_new)
            l_new, acc = l_new + w * l, acc + w * a
        return m_new, l_new, acc

    def chunk_step(c, slot, carry):
        wait_chunk(r, c, slot)
        wrap = jnp.where(c + PAGED_AHEAD >= n_chunks, 1, 0)
        start_chunk(jnp.minimum(r + wrap, n_req - 1), c + PAGED_AHEAD - wrap * n_chunks,
                    (slot + PAGED_AHEAD) % PAGED_SLOTS)
        pages = block_keys // page
        lat_block = lambda blk: lat_bufs[slot][blk * block_keys:(blk + 1) * block_keys].astype(BF16)
        all_scores = []
        for blk in range(group // pages):
            rope_t = jnp.concatenate([rope_bufs[slot][blk * pages + g] for g in range(pages)], axis=1)
            all_scores.append(scores(lat_block(blk), rope_t.astype(BF16), NN))
        return combine(carry, [partial_softmax(s, lat_block(blk)) for blk, s in enumerate(all_scores)])

    def body(it, carry):
        for slot in range(PAGED_SLOTS):
            carry = chunk_step(PAGED_SLOTS * it + slot, slot, carry)
        return carry

    rows = heads * t
    init = (jnp.full((rows, 1), -jnp.inf, F32), jnp.zeros((rows, 1), F32), jnp.zeros((rows, kv_lora), F32))
    carry = lax.fori_loop(0, n_chunks // PAGED_SLOTS, body, init)

    @pl.when(r == n_req - 1)
    def _drain():
        for c in range(PAGED_AHEAD):
            wait_chunk(r, c, c)

    lat_new = ckv_ref[...]
    s_new = scores(lat_new, kpe_ref[...], NT)
    tok = lax.broadcasted_iota(jnp.int32, (rows, t), 0) % t
    key = lax.broadcasted_iota(jnp.int32, (rows, t), 1)
    s_new = jnp.where(key <= tok, s_new, -jnp.inf)
    _, l_fin, acc = combine(carry, [partial_softmax(s_new, lat_new)])
    o = acc / l_fin
    for h in range(heads):
        o_ref[:, h * kv_lora:(h + 1) * kv_lora] = o[h * t:(h + 1) * t].astype(o_ref.dtype)


def _hgrn_project(h_ref, wb_ref, lb_ref, g_scr, k_scr, q_scr, v_scr, fdim, vdim):
    z = _dot(h_ref[...], wb_ref[...])
    l0 = lb_ref[0:1, :]
    l1 = lb_ref[1:2, :]
    mx = jnp.maximum(l0, l1)
    e0 = jnp.exp(l0 - mx)
    e1 = jnp.exp(l1 - mx)
    lb = e1 / (e0 + e1)
    f = lb + (1.0 - lb) * jax.nn.sigmoid(z[:, :fdim])
    g_scr[...] = jnp.log(f)
    k_scr[...] = 1.0 - f
    q_scr[...] = _silu(z[:, fdim:2 * fdim])
    v_scr[...] = z[:, 2 * fdim:2 * fdim + vdim]
    return z[:, 2 * fdim + vdim:]


def _cumsum_rows(x):
    row = lax.broadcasted_iota(jnp.int32, x.shape, 0)
    shift = 1
    while shift < x.shape[0]:
        x = x + jnp.where(row >= shift, pltpu.roll(x, shift, 0), 0.0)
        shift *= 2
    return x


def _hgrn_chunk(qq, kk, g, v, st_ref, o_ref, rows, *, sub):
    c = qq.shape[0]
    n_pairs = st_ref.shape[0]
    hv = st_ref.shape[1] // 2
    hk = st_ref.shape[2] // 2
    mm = BF16 if sub % BF16_ROWS == 0 else F32
    bcum = _cumsum_rows(g)
    last = bcum[c - 1:c, :]
    q_in = (qq * jnp.exp(bcum)).astype(mm)
    k_dec = (kk * jnp.exp(last - bcum)).astype(mm)
    v_mm = v.astype(mm)
    n_sub = c // sub
    row_blk = lax.broadcasted_iota(jnp.int32, (c, hk), 0) // sub
    causal = lax.broadcasted_iota(jnp.int32, (c, c), 0) >= lax.broadcasted_iota(jnp.int32, (c, c), 1)
    lane_v = lax.broadcasted_iota(jnp.int32, (1, 2 * hv), 1) // hv
    bd = (lax.broadcasted_iota(jnp.int32, (2 * hv, 2 * hk), 0) // hv
          == lax.broadcasted_iota(jnp.int32, (2 * hv, 2 * hk), 1) // hk)
    def head_scores(hl):
        b_h, q_h = bcum[:, hl], qq[:, hl]
        refs = [b_h[j * sub + sub // 2:j * sub + sub // 2 + 1] for j in range(n_sub)]
        ref_rows = jnp.concatenate([jnp.broadcast_to(rj, (sub, hk)) for rj in refs], axis=0)
        k_all = kk[:, hl] * jnp.exp(ref_rows - b_h)
        k_cat = jnp.concatenate([jnp.where(row_blk == j, k_all, 0.0).astype(mm) for j in range(n_sub)], axis=1)
        q_cat = jnp.concatenate(
            [jnp.concatenate([jnp.zeros((j * sub, hk), F32)] * (j > 0)
                             + [q_h[j * sub:] * jnp.exp(b_h[j * sub:] - refs[j])], axis=0).astype(mm)
             for j in range(n_sub)], axis=1)
        att = lax.dot_general(q_cat, k_cat, NT, preferred_element_type=F32)
        return jnp.where(causal, att, 0.0).astype(mm)

    atts = [head_scores(slice(h * hk, (h + 1) * hk)) for h in range(2 * n_pairs)]
    for p in range(n_pairs):
        pk = slice(p * 2 * hk, (p + 1) * 2 * hk)
        pv = slice(p * 2 * hv, (p + 1) * 2 * hv)
        vp = v_mm[:, pv]
        st = st_ref[p]
        o = lax.dot_general(q_in[:, pk], st.astype(mm), NT, preferred_element_type=F32)
        for e in range(2):
            o = o + _dot(atts[2 * p + e], jnp.where(lane_v == e, vp, jnp.zeros_like(vp)))
        o_ref[rows, pv] = o
        upd = lax.dot_general(vp, k_dec[:, pk], TN, preferred_element_type=F32)
        st_ref[p] = st * jnp.exp(last[:, pk]) + jnp.where(bd, upd, 0.0)


def _hgrn_finish(o, zb, gho_ref, avg_ref, out_ref):
    ms = _dot((o * o).astype(BF16), avg_ref[...])
    out_ref[...] = (o * lax.rsqrt(ms + EPS) * gho_ref[...] * _silu(zb)).astype(out_ref.dtype)


def _compact_state(st):
    hv = st.shape[0] // 2
    hk = st.shape[1] // 2
    return jnp.concatenate([st[:hv, :hk], st[hv:, hk:]], axis=0)


def _expand_state(sc):
    hv = sc.shape[0] // 2
    z = jnp.zeros((hv, sc.shape[1]), F32)
    return jnp.concatenate([jnp.concatenate([sc[:hv], z], axis=1),
                            jnp.concatenate([z, sc[hv:]], axis=1)], axis=0)


def _hgrn_prompt_kernel(h_ref, wb_ref, lb_ref, gho_ref, avg_ref, out_ref, sfin_ref,
                        st_scr, g_scr, k_scr, q_scr, v_scr, o_scr, *, chunk, sub, fdim, vdim):
    tb = pl.program_id(1)

    @pl.when(tb == 0)
    def _init():
        st_scr[...] = jnp.zeros(st_scr.shape, F32)

    zb = _hgrn_project(h_ref, wb_ref, lb_ref, g_scr, k_scr, q_scr, v_scr, fdim, vdim)

    def body(ci, carry):
        rows = pl.ds(pl.multiple_of(ci * chunk, chunk), chunk)
        _hgrn_chunk(q_scr[rows, :], k_scr[rows, :], g_scr[rows, :], v_scr[rows, :], st_scr, o_scr, rows,
                    sub=sub)
        return carry

    lax.fori_loop(0, h_ref.shape[0] // chunk, body, 0, unroll=True)
    _hgrn_finish(o_scr[...], zb, gho_ref, avg_ref, out_ref)

    @pl.when(tb == pl.num_programs(1) - 1)
    def _final():
        for p in range(st_scr.shape[0]):
            sfin_ref[0, p] = _compact_state(st_scr[p])


def _hgrn_sample_kernel(h_ref, wb_ref, lb_ref, gho_ref, avg_ref, s0_ref, out_ref, sfin_ref,
                        st_scr, g_scr, k_scr, q_scr, v_scr, o_scr, *, t, fdim, vdim):
    zb = _hgrn_project(h_ref, wb_ref, lb_ref, g_scr, k_scr, q_scr, v_scr, fdim, vdim)

    for ri in range(s0_ref.shape[0]):
        st_ref = st_scr.at[ri]
        for p in range(st_ref.shape[0]):
            st_ref[p] = _expand_state(s0_ref[ri, p])
        rows = pl.ds(ri * t, t)
        _hgrn_chunk(q_scr[rows, :], k_scr[rows, :], g_scr[rows, :], v_scr[rows, :], st_ref, o_scr, rows,
                    sub=t)
        for p in range(st_ref.shape[0]):
            sfin_ref[ri, p] = _compact_state(st_ref[p])
    _hgrn_finish(o_scr[...], zb, gho_ref, avg_ref, out_ref)


def _xattn_heads(q, zc, head_k, head_v, heads, hdim):
    outs = []
    all_scores = [lax.dot_general(q[:, h * hdim:(h + 1) * hdim], head_k(h), NT, preferred_element_type=F32)
                  for h in range(heads)]
    for h, s in enumerate(all_scores):
        mv = head_v(h)
        s = s - jnp.max(s, axis=1, keepdims=True)
        p = jnp.exp(s)
        p = (p / jnp.sum(p, axis=1, keepdims=True)).astype(mv.dtype)
        outs.append(_dot(p, mv))
    return jnp.concatenate(outs, axis=1) * _silu(zc)


def _xattn_prompt_kernel(h_ref, wc_ref, mk_ref, mv_ref, out_ref, *, heads, hdim, scale):
    z = _dot(h_ref[0], wc_ref[...])
    w = heads * hdim
    q = (z[:, :w] * scale).astype(BF16)
    mk = mk_ref[0].astype(BF16)
    mv = mv_ref[0].astype(BF16)
    out_ref[0] = _xattn_heads(q, z[:, w:], lambda h: mk[:, h * hdim:(h + 1) * hdim],
                              lambda h: mv[:, h * hdim:(h + 1) * hdim], heads, hdim).astype(out_ref.dtype)


def _xattn_sample_kernel(h_ref, wc_ref, mk_ref, mv_ref, out_ref, *, heads, hdim, scale, t):
    z = _dot(h_ref[...], wc_ref[...])
    w = heads * hdim
    q = z[:, :w] * scale
    zc = z[:, w:]
    n_rows = mk_ref.shape[1]
    row_head = lax.broadcasted_iota(jnp.int32, (heads * t, n_rows), 0) // t
    col_head = lax.broadcasted_iota(jnp.int32, (heads * t, n_rows), 1) % heads
    all_scores = []
    for ri in range(mk_ref.shape[0]):
        qs = jnp.concatenate([q[ri * t:(ri + 1) * t, h * hdim:(h + 1) * hdim] for h in range(heads)], axis=0)
        all_scores.append(lax.dot_general(qs.astype(BF16), mk_ref[ri].astype(BF16), NT,
                                          preferred_element_type=F32))
    for ri, s in enumerate(all_scores):
        rows = slice(ri * t, (ri + 1) * t)
        s = jnp.where(row_head == col_head, s, -jnp.inf)
        p = jnp.exp(s - jnp.max(s, axis=1, keepdims=True))
        p = (p / jnp.sum(p, axis=1, keepdims=True)).astype(BF16)
        o = _dot(p, mv_ref[ri].astype(BF16))
        o = jnp.concatenate([o[h * t:(h + 1) * t] for h in range(heads)], axis=1)
        out_ref[rows, :] = (o * _silu(zc[rows])).astype(out_ref.dtype)


def _memkv_kernel(m_ref, g_ref, w_ref, k_ref, v_ref):
    kv = _dot(_rms(m_ref[...], g_ref[...]).astype(BF16), w_ref[...])
    w = kv.shape[1] // 2
    k_ref[...] = kv[:, :w]
    v_ref[...] = kv[:, w:]


def _merge_kernel(x_ref, h_ref, oa_ref, ob_ref, oc_ref, wuv_ref, wm_ref, wbr_ref, wout_ref, gf_ref, y_ref, *,
                  width, from_latent):
    h = h_ref[...]
    d = x_ref.shape[1]
    oa = oa_ref[...]
    if from_latent:
        oa = _dot(oa.astype(BF16), wuv_ref[...])
    else:
        oa = oa.astype(F32)
    za = _dot(h, wm_ref[:, :width])
    branches = ((oa * _silu(za)).astype(BF16), ob_ref[...].astype(BF16), oc_ref[...].astype(BF16))
    merged = jnp.zeros((x_ref.shape[0], d), F32)
    for n, o in enumerate(branches):
        gate = jax.nn.sigmoid(_dot(h, wm_ref[:, width + n * d:width + (n + 1) * d]))
        merged = merged + gate * _dot(o, wbr_ref[n])
    out = x_ref[...] + _dot(merged.astype(BF16), wout_ref[...])
    y_ref[...] = _rms(out, gf_ref[...])


def _rope_tables(pos, rope, lead):
    half = rope // 2
    inv = jnp.exp(-math.log(ROPE_BASE) * jnp.arange(half, dtype=F32) / half)
    ang = pos.astype(F32)[:, None] * inv[None, :]
    n = pos.shape[0]
    pad = jnp.zeros((n, LANES - lead - rope), F32)
    cos = jnp.concatenate([jnp.ones((n, lead), F32), jnp.cos(ang), jnp.cos(ang), pad], axis=1)
    sin_signed = jnp.concatenate([jnp.zeros((n, lead), F32), -jnp.sin(ang), jnp.sin(ang), pad], axis=1)
    return cos, sin_signed


def _pad_cols(w, width):
    return jnp.pad(w, [(0, 0)] * (w.ndim - 1) + [(0, width - w.shape[-1])])


def _block_diag(blocks):
    rows = sum(b.shape[0] for b in blocks)
    cols = sum(b.shape[1] for b in blocks)
    out = jnp.zeros((rows, cols), blocks[0].dtype)
    r = c = 0
    for b in blocks:
        out = lax.dynamic_update_slice(out, b, (r, c))
        r += b.shape[0]
        c += b.shape[1]
    return out


def _tile_rows(n, pref):
    t = min(n, pref)
    assert n % t == 0
    return t


def kernel(x_prompt, x_sample, mem_prompt, cache_kv_latent, cache_k_rope, page_table, state_hgrn, cache_mem_k, cache_mem_v, g_norm, w_in, g_q_lora, w_uq, g_kv_lora, w_uk, w_uv, lb_logits, g_hgrn_out, g_mem_norm, w_mem_kv, w_branch, w_out, g_final):
    depth = w_in.shape[0]
    assert depth == 1, "one layer per step"
    b, t, d = x_prompt.shape
    nb, ts, _ = x_sample.shape
    n_mem = mem_prompt.shape[1]
    n_phys, page, kv_lora = cache_kv_latent.shape[1:]
    rope = cache_k_rope.shape[-1]
    n_pages = page_table.shape[1]
    q_lora = g_q_lora.shape[-1]
    a_heads, a_qk = w_uq.shape[2:]
    a_nope = w_uk.shape[-1]
    a_vdim = w_uv.shape[-1]
    b_heads, b_expand, b_vdim = state_hgrn.shape[2:]
    fdim = b_heads * b_expand
    c_heads, c_hdim = cache_mem_k.shape[3:]
    width = w_branch.shape[2]
    assert a_qk == a_nope + rope and a_heads * a_vdim == width and b_heads * b_vdim == width
    assert c_heads * c_hdim == width and a_qk <= LANES and 2 * a_vdim == LANES and 2 * b_vdim == LANES
    assert b_expand == LANES and a_heads % 2 == 0 and b_heads % 2 == 0

    splits = (q_lora, kv_lora, rope, width, fdim, fdim, width, width, width, width, N_BRANCH * d)
    offs = np.concatenate([[0], np.cumsum(splits)])
    assert offs[-1] == w_in.shape[-1]
    w_in_t = jnp.swapaxes(w_in, 1, 2).reshape(w_in.shape[-1], d)

    def in_proj(*pieces, pad_to=None):
        rows = [w_in_t[offs[i]:offs[i + 1]] for i in pieces]
        if pad_to is not None:
            rows.append(jnp.zeros((pad_to - sum(r.shape[0] for r in rows), d), F32))
        return jnp.concatenate(rows, axis=0).astype(BF16).T

    a_scale = a_qk ** -0.5
    n_p = b * t
    n_s = nb * ts
    xp = x_prompt.reshape(n_p, d)
    xs = x_sample.reshape(n_s, d)
    row = lambda g: g.reshape(1, -1).astype(F32)

    w_a = in_proj(0, 1, 2, pad_to=q_lora + kv_lora + LANES)
    uq = w_uq.reshape(q_lora, a_heads, a_qk)
    uq_swapped = jnp.concatenate([jnp.zeros_like(uq[..., :a_nope]), uq[..., a_nope + rope // 2:],
                                  uq[..., a_nope:a_nope + rope // 2]], axis=-1)
    wq_prompt = jnp.concatenate([_pad_cols(uq, LANES).reshape(q_lora, -1),
                                 _pad_cols(uq_swapped, LANES).reshape(q_lora, -1)], axis=1).astype(BF16)
    wq_sample = jnp.concatenate([uq[..., :a_nope].reshape(q_lora, -1),
                                 _pad_cols(uq[..., a_nope:], LANES).reshape(q_lora, -1)], axis=1).astype(BF16)
    uk = w_uk.reshape(kv_lora, a_heads, a_nope)
    uv = w_uv.reshape(kv_lora, a_heads, a_vdim)
    uv_pad = jnp.stack([jnp.pad(uv[:, h], ((0, 0), ((h % 2) * a_vdim, LANES - a_vdim - (h % 2) * a_vdim)))
                        for h in range(a_heads)], axis=1)
    w_kv = jnp.concatenate([_pad_cols(uk, LANES).reshape(kv_lora, -1), uv_pad.reshape(kv_lora, -1)],
                           axis=1).astype(BF16)
    w_abs = _block_diag([uk[:, h].T for h in range(a_heads)]).astype(BF16)
    w_uv_bd = _block_diag([uv[:, h] for h in range(a_heads)]).astype(BF16)
    w_b = in_proj(4, 5, 6, 7)
    w_c = in_proj(8, 9)
    w_m = in_proj(3, 10)
    w_br = w_branch[0].astype(BF16)
    w_o = w_out[0].astype(BF16)
    w_mem = w_mem_kv[0].astype(BF16)
    gho = jnp.tile(g_hgrn_out[0], b_heads).reshape(1, width).astype(F32)
    head_of = jnp.arange(width) // b_vdim
    avg = ((head_of[:, None] == head_of[None, :]).astype(F32) / b_vdim).astype(BF16)
    lb2 = lb_logits[:2].astype(F32)

    n_past = n_pages * page
    cq_p, sq_p = _rope_tables(jnp.arange(t), rope, a_nope)
    ck_p, sk_p = _rope_tables(jnp.arange(t), rope, 0)
    tm_s = _tile_rows(n_s, 256)
    assert tm_s % ts == 0
    ck_s, sk_s = _rope_tables(n_past + (jnp.arange(tm_s) % ts), rope, 0)

    tm = _tile_rows(t, 1024)
    n_t = t // tm
    mla_common = dict(q_lora=q_lora, kv_lora=kv_lora, rope=rope)
    rowspec = lambda w: pl.BlockSpec((tm, w), lambda i: (i, 0))
    tabspec = pl.BlockSpec((tm, LANES), lambda i: (i % n_t, 0))
    hq = a_heads * LANES
    h_p, q_p, k_p, v_p, ckv_p, kpe_p = pl.pallas_call(
        functools.partial(_mla_prep_prompt_kernel, scale=a_scale * math.log2(math.e), nope=a_nope,
                          **mla_common),
        grid=(n_p // tm,),
        in_specs=[rowspec(d), _const_spec((1, d)), _const_spec(w_a.shape), _const_spec((1, q_lora)),
                  _const_spec(wq_prompt.shape), _const_spec((1, kv_lora)), _const_spec(w_kv.shape),
                  tabspec, tabspec, tabspec, tabspec],
        out_specs=[rowspec(d), rowspec(hq), rowspec(hq), rowspec(hq), rowspec(kv_lora),
                   pl.BlockSpec((1, rope, tm), lambda i: (i // n_t, 0, i % n_t))],
        out_shape=[jax.ShapeDtypeStruct((n_p, d), BF16), jax.ShapeDtypeStruct((n_p, hq), BF16),
                   jax.ShapeDtypeStruct((n_p, hq), BF16), jax.ShapeDtypeStruct((n_p, hq), BF16),
                   jax.ShapeDtypeStruct((n_p, kv_lora), F32), jax.ShapeDtypeStruct((b, rope, t), F32)],
        compiler_params=_params(("parallel",)),
    )(xp, row(g_norm), w_a, row(g_q_lora), wq_prompt, row(g_kv_lora), w_kv, cq_p, sq_p, ck_p, sk_p)

    srow = lambda w: pl.BlockSpec((tm_s, w), lambda i: (i, 0))
    hl = a_heads * kv_lora
    h_s, qlat_s, qrope_s, ckv_s, kpe_s = pl.pallas_call(
        functools.partial(_mla_prep_sample_kernel, scale=a_scale, nope_w=a_heads * a_nope, **mla_common),
        grid=(n_s // tm_s,),
        in_specs=[srow(d), _const_spec((1, d)), _const_spec(w_a.shape), _const_spec((1, q_lora)),
                  _const_spec(wq_sample.shape), _const_spec((1, kv_lora)), _const_spec(w_abs.shape),
                  _const_spec((tm_s, LANES)), _const_spec((tm_s, LANES))],
        out_specs=[srow(d), srow(hl), srow(hq), srow(kv_lora), srow(rope)],
        out_shape=[jax.ShapeDtypeStruct((n_s, d), BF16), jax.ShapeDtypeStruct((n_s, hl), F32),
                   jax.ShapeDtypeStruct((n_s, hq), F32), jax.ShapeDtypeStruct((n_s, kv_lora), F32),
                   jax.ShapeDtypeStruct((n_s, rope), F32)],
        compiler_params=_params(("parallel",)),
    )(xs, row(g_norm), w_a, row(g_q_lora), wq_sample, row(g_kv_lora), w_abs, ck_s, sk_s)

    tq = _tile_rows(t, 512)
    assert tq % LANES == 0
    nq = t // tq
    pairs = [(i, j) for i in range(nq) for j in range(i + 1)]
    i_tab = jnp.asarray([p[0] for p in pairs], jnp.int32)
    j_tab = jnp.asarray([p[1] for p in pairs], jnp.int32)
    qkv = lambda a: a.reshape(b, t, hq)
    oa_p = pl.pallas_call(
        functools.partial(_flash_kernel, heads=a_heads),
        grid_spec=pltpu.PrefetchScalarGridSpec(
            num_scalar_prefetch=2, grid=(b, len(pairs)),
            in_specs=[pl.BlockSpec((1, tq, hq), lambda bi, s, it, jt: (bi, it[s], 0)),
                      pl.BlockSpec((1, tq, hq), lambda bi, s, it, jt: (bi, jt[s], 0)),
                      pl.BlockSpec((1, tq, hq), lambda bi, s, it, jt: (bi, jt[s], 0))],
            out_specs=pl.BlockSpec((1, tq, width), lambda bi, s, it, jt: (bi, it[s], 0)),
            scratch_shapes=[pltpu.VMEM((a_heads, tq, LANES), F32)] * 2),
        out_shape=jax.ShapeDtypeStruct((b, t, width), BF16),
        compiler_params=_params(("parallel", "arbitrary")),
    )(i_tab, j_tab, qkv(q_p), qkv(k_p), qkv(v_p)).reshape(n_p, width)

    assert n_pages % PAGED_SLOTS == 0
    group = math.gcd(n_pages // PAGED_SLOTS, 32)
    n_chunks = n_pages // group
    block_keys = math.gcd(group, 8) * page
    req = lambda w: pl.BlockSpec((ts, w), lambda r, pt: (r, 0))
    olat_s = pl.pallas_call(
        functools.partial(_paged_kernel, heads=a_heads, group=group, page=page, n_chunks=n_chunks,
                          block_keys=block_keys,
                          kv_lora=kv_lora, rope=rope),
        grid_spec=pltpu.PrefetchScalarGridSpec(
            num_scalar_prefetch=1, grid=(nb,),
            in_specs=[req(hl), req(hq), req(kv_lora), req(rope),
                      pl.BlockSpec(memory_space=pl.ANY), pl.BlockSpec(memory_space=pl.ANY)],
            out_specs=req(hl),
            scratch_shapes=[pltpu.VMEM((group * page, kv_lora), F32)] * PAGED_SLOTS
            + [pltpu.VMEM((group, rope, page), F32)] * PAGED_SLOTS
            + [pltpu.SemaphoreType.DMA((2, PAGED_SLOTS))]),
        out_shape=jax.ShapeDtypeStruct((n_s, hl), F32),
        compiler_params=_params(("arbitrary",)),
    )(page_table, qlat_s, qrope_s, ckv_s, kpe_s, cache_kv_latent.reshape(n_phys, page, kv_lora),
      jnp.swapaxes(cache_k_rope, 2, 3).reshape(n_phys, rope, page))

    n_pairs = b_heads // 2
    tb = _tile_rows(t, 512)
    chunk = _tile_rows(tb, 64)
    sub = _tile_rows(chunk, 16)
    wb_cols = w_b.shape[1]
    hgrn_scratch = lambda rows, *lead: [pltpu.VMEM(lead + (n_pairs, LANES, 2 * b_expand), F32),
                                 pltpu.VMEM((rows, fdim), F32), pltpu.VMEM((rows, fdim), F32),
                                 pltpu.VMEM((rows, fdim), F32), pltpu.VMEM((rows, width), F32),
                                 pltpu.VMEM((rows, width), F32)]
    ob_p, st_p = pl.pallas_call(
        functools.partial(_hgrn_prompt_kernel, chunk=chunk, sub=sub, fdim=fdim, vdim=width),
        grid=(b, t // tb),
        in_specs=[pl.BlockSpec((tb, d), lambda bi, ti: (bi * (t // tb) + ti, 0)),
                  _const_spec((d, wb_cols)), _const_spec((2, fdim)), _const_spec((1, width)),
                  _const_spec((width, width))],
        out_specs=[pl.BlockSpec((tb, width), lambda bi, ti: (bi * (t // tb) + ti, 0)),
                   pl.BlockSpec((1, n_pairs, LANES, b_expand), lambda bi, ti: (bi, 0, 0, 0))],
        out_shape=[jax.ShapeDtypeStruct((n_p, width), BF16),
                   jax.ShapeDtypeStruct((b, n_pairs, LANES, b_expand), F32)],
        scratch_shapes=hgrn_scratch(tb),
        compiler_params=_params(("parallel", "arbitrary")),
    )(h_p, w_b, lb2, gho, avg)

    def to_pairs(s):
        n = s.shape[0]
        return s.reshape(n, n_pairs, 2, b_expand, b_vdim).transpose(0, 1, 2, 4, 3).reshape(
            n, n_pairs, LANES, b_expand)

    def from_pairs(s):
        n = s.shape[0]
        return s.reshape(n, n_pairs, 2, b_vdim, b_expand).transpose(0, 1, 2, 4, 3).reshape(
            n, b_heads, b_expand, b_vdim)

    rq = _tile_rows(nb, 8)
    ob_s, st_s = pl.pallas_call(
        functools.partial(_hgrn_sample_kernel, t=ts, fdim=fdim, vdim=width),
        grid=(nb // rq,),
        in_specs=[pl.BlockSpec((rq * ts, d), lambda i: (i, 0)),
                  _const_spec((d, wb_cols)), _const_spec((2, fdim)), _const_spec((1, width)),
                  _const_spec((width, width)),
                  pl.BlockSpec((rq, n_pairs, LANES, b_expand), lambda i: (i, 0, 0, 0))],
        out_specs=[pl.BlockSpec((rq * ts, width), lambda i: (i, 0)),
                   pl.BlockSpec((rq, n_pairs, LANES, b_expand), lambda i: (i, 0, 0, 0))],
        out_shape=[jax.ShapeDtypeStruct((n_s, width), BF16),
                   jax.ShapeDtypeStruct((nb, n_pairs, LANES, b_expand), F32)],
        scratch_shapes=hgrn_scratch(rq * ts, rq),
        compiler_params=_params(("parallel",)),
    )(h_s, w_b, lb2, gho, avg, to_pairs(state_hgrn.reshape(nb, b_heads, b_expand, b_vdim)))

    n_m = b * n_mem
    tmm = _tile_rows(n_m, 512)
    mk_p, mv_p = pl.pallas_call(
        _memkv_kernel,
        grid=(n_m // tmm,),
        in_specs=[pl.BlockSpec((tmm, d), lambda i: (i, 0)), _const_spec((1, d)), _const_spec(w_mem.shape)],
        out_specs=[pl.BlockSpec((tmm, width), lambda i: (i, 0)), pl.BlockSpec((tmm, width), lambda i: (i, 0))],
        out_shape=[jax.ShapeDtypeStruct((n_m, width), F32), jax.ShapeDtypeStruct((n_m, width), F32)],
        compiler_params=_params(("parallel",)),
    )(mem_prompt.reshape(n_m, d), row(g_mem_norm), w_mem)

    c_scale = c_hdim ** -0.5
    tx = _tile_rows(t, 1024)
    memspec = pl.BlockSpec((1, n_mem, width), lambda bi, ti: (bi, 0, 0))
    oc_p = pl.pallas_call(
        functools.partial(_xattn_prompt_kernel, heads=c_heads, hdim=c_hdim, scale=c_scale),
        grid=(b, t // tx),
        in_specs=[pl.BlockSpec((1, tx, d), lambda bi, ti: (bi, ti, 0)), _const_spec(w_c.shape), memspec, memspec],
        out_specs=pl.BlockSpec((1, tx, width), lambda bi, ti: (bi, ti, 0)),
        out_shape=jax.ShapeDtypeStruct((b, t, width), BF16),
        compiler_params=_params(("parallel", "parallel")),
    )(h_p.reshape(b, t, d), w_c, mk_p.reshape(b, n_mem, width), mv_p.reshape(b, n_mem, width)).reshape(n_p, width)

    rx = _tile_rows(nb, 8)
    smem = pl.BlockSpec((rx, n_mem * c_heads, c_hdim), lambda i: (i, 0, 0))
    oc_s = pl.pallas_call(
        functools.partial(_xattn_sample_kernel, heads=c_heads, hdim=c_hdim, scale=c_scale, t=ts),
        grid=(nb // rx,),
        in_specs=[pl.BlockSpec((rx * ts, d), lambda i: (i, 0)), _const_spec(w_c.shape), smem, smem],
        out_specs=pl.BlockSpec((rx * ts, width), lambda i: (i, 0)),
        out_shape=jax.ShapeDtypeStruct((n_s, width), F32),
        compiler_params=_params(("parallel",)),
    )(h_s, w_c, cache_mem_k.reshape(nb, n_mem * c_heads, c_hdim), cache_mem_v.reshape(nb, n_mem * c_heads, c_hdim))

    def merge(x2, h2, oa, ob, oc, from_latent):
        n = x2.shape[0]
        tmg = _tile_rows(n, 512)
        rs = lambda w: pl.BlockSpec((tmg, w), lambda i: (i, 0))
        return pl.pallas_call(
            functools.partial(_merge_kernel, width=width, from_latent=from_latent),
            grid=(n // tmg,),
            in_specs=[rs(d), rs(d), rs(oa.shape[1]), rs(width), rs(width), _const_spec(w_uv_bd.shape),
                      _const_spec(w_m.shape), _const_spec(w_br.shape), _const_spec(w_o.shape),
                      _const_spec((1, d))],
            out_specs=rs(d),
            out_shape=jax.ShapeDtypeStruct((n, d), F32),
            compiler_params=_params(("parallel",)),
        )(x2, h2, oa, ob, oc, w_uv_bd, w_m, w_br, w_o, row(g_final))

    y_p = merge(xp, h_p, oa_p, ob_p, oc_p, False).reshape(b, t, d)
    y_s = merge(xs, h_s, olat_s, ob_s, oc_s, True).reshape(nb, ts, d)

    return (y_p, y_s,
            ckv_p.reshape(1, b, t, kv_lora), jnp.swapaxes(kpe_p, 1, 2).reshape(1, b, t, rope),
            from_pairs(st_p)[None],
            mk_p.reshape(1, b, n_mem, c_heads, c_hdim), mv_p.reshape(1, b, n_mem, c_heads, c_hdim),
            ckv_s.reshape(1, nb, ts, kv_lora), kpe_s.reshape(1, nb, ts, rope),
            from_pairs(st_s)[None])
```

```python
import functools
import math

import numpy as np
import jax
import jax.numpy as jnp
from jax import lax
from jax.experimental import pallas as pl
from jax.experimental.pallas import tpu as pltpu

F32 = jnp.float32
BF16 = jnp.bfloat16
EPS = 1e-6
ROPE_BASE = 10000.0
N_BRANCH = 3
LANES = 128
BF16_ROWS = 16
PAGED_SLOTS = 4
PAGED_AHEAD = 3
VMEM_LIMIT = 56 * 1024 * 1024
NT = (((1,), (1,)), ((), ()))
TN = (((0,), (0,)), ((), ()))
NN = (((1,), (0,)), ((), ()))


def _params(sem):
    return pltpu.CompilerParams(dimension_semantics=sem, vmem_limit_bytes=VMEM_LIMIT)


def _rms(x, g):
    return x * lax.rsqrt(jnp.mean(x * x, axis=-1, keepdims=True) + EPS) * g


def _silu(x):
    return x * jax.nn.sigmoid(x)


def _dot(a, b):
    return jnp.dot(a, b, preferred_element_type=F32)


def _const_spec(shape):
    nd = len(shape)
    return pl.BlockSpec(shape, lambda *_: (0,) * nd)


def _rotate(x, cos, sin_signed, half):
    n = x.shape[1]
    first_half = (lax.broadcasted_iota(jnp.int32, x.shape, 1) // half) % 2 == 0
    partner = jnp.where(first_half, pltpu.roll(x, n - half, 1), pltpu.roll(x, half, 1))
    reps = n // cos.shape[1]
    return x * jnp.tile(cos, (1, reps)) + partner * jnp.tile(sin_signed, (1, reps))


def _mla_common(x_ref, gn_ref, wa_ref, gq_ref, gkv_ref, ck_ref, sk_ref, h_ref, ckv_ref, kpe_ref,
                q_lora, kv_lora, rope, kpe_tokens_on_lanes=False):
    h = _rms(x_ref[...], gn_ref[...]).astype(BF16)
    h_ref[...] = h
    z = _dot(h, wa_ref[...])
    cq_n = _rms(z[:, :q_lora], gq_ref[...]).astype(BF16)
    ckv = _rms(z[:, q_lora:q_lora + kv_lora], gkv_ref[...])
    ckv_ref[...] = ckv
    o = q_lora + kv_lora
    kpe = _rotate(z[:, o:o + LANES], ck_ref[...], sk_ref[...], rope // 2)
    if kpe_tokens_on_lanes:
        kpe_ref[0] = kpe.T[:rope, :]
    else:
        kpe_ref[...] = kpe[:, :rope]
    return cq_n, ckv, kpe


def _mla_prep_prompt_kernel(x_ref, gn_ref, wa_ref, gq_ref, wq_ref, gkv_ref, wkv_ref, cq_ref, sq_ref,
                            ck_ref, sk_ref, h_ref, q_ref, k_ref, v_ref, ckv_ref, kpe_ref, *,
                            scale, q_lora, kv_lora, rope, nope):
    cq_n, ckv, kpe = _mla_common(x_ref, gn_ref, wa_ref, gq_ref, gkv_ref, ck_ref, sk_ref, h_ref,
                                 ckv_ref, kpe_ref, q_lora, kv_lora, rope, kpe_tokens_on_lanes=True)
    q2 = _dot(cq_n, wq_ref[...])
    w = q2.shape[1] // 2
    reps = w // LANES
    q_ref[...] = ((q2[:, :w] * jnp.tile(cq_ref[...], (1, reps))
                   + q2[:, w:] * jnp.tile(sq_ref[...], (1, reps))) * scale).astype(BF16)
    kv = _dot(ckv.astype(BF16), wkv_ref[...])
    w = kv.shape[1] // 2
    heads = w // LANES
    k_ref[...] = (kv[:, :w] + jnp.tile(pltpu.roll(kpe, nope, 1), (1, heads))).astype(BF16)
    lane = lax.broadcasted_iota(jnp.int32, (1, w), 1)
    sum_lane = jnp.where((lane // LANES) % 2 == 0, _sum_lane(0), _sum_lane(1))
    v_ref[...] = jnp.where(lane % LANES == sum_lane, 1.0, kv[:, w:]).astype(BF16)


def _mla_prep_sample_kernel(x_ref, gn_ref, wa_ref, gq_ref, wq_ref, gkv_ref, wabs_ref, ck_ref, sk_ref,
                            h_ref, qlat_ref, qrope_ref, ckv_ref, kpe_ref, *,
                            scale, q_lora, kv_lora, rope, nope_w):
    cq_n, _, _ = _mla_common(x_ref, gn_ref, wa_ref, gq_ref, gkv_ref, ck_ref, sk_ref, h_ref,
                             ckv_ref, kpe_ref, q_lora, kv_lora, rope)
    q2 = _dot(cq_n, wq_ref[...])
    q_nope = (q2[:, :nope_w] * scale).astype(BF16)
    qlat_ref[...] = _dot(q_nope, wabs_ref[...])
    qrope_ref[...] = _rotate(q2[:, nope_w:], ck_ref[...], sk_ref[...], rope // 2) * scale


def _sum_lane(head):
    return 0 if head % 2 else LANES - 1


def _flash_kernel(it_ref, jt_ref, q_ref, k_ref, v_ref, o_ref, m_scr, acc_scr, *, heads):
    step = pl.program_id(1)
    i = it_ref[step]
    j = jt_ref[step]
    tq = q_ref.shape[1]

    @pl.when(j == 0)
    def _init():
        m_scr[...] = jnp.full(m_scr.shape, -jnp.inf, F32)
        acc_scr[...] = jnp.zeros(acc_scr.shape, F32)

    def update(masked):
        if masked:
            row = lax.broadcasted_iota(jnp.int32, (tq, tq), 0)
            col = lax.broadcasted_iota(jnp.int32, (tq, tq), 1)
            keep = row >= col
        def scores(h):
            sl = slice(h * LANES, (h + 1) * LANES)
            return lax.dot_general(q_ref[0, :, sl], k_ref[0, :, sl], NT, preferred_element_type=F32)

        ahead = 2
        pending = [scores(h) for h in range(ahead)]
        for h in range(heads):
            s = pending.pop(0)
            if h + ahead < heads:
                pending.append(scores(h + ahead))
            sl = slice(h * LANES, (h + 1) * LANES)
            if masked:
                s = jnp.where(keep, s, -jnp.inf)
            m_prev = m_scr[h]
            m_new = jnp.maximum(m_prev, jnp.max(s, axis=1, keepdims=True))
            p = jnp.exp2(s - jnp.tile(m_new, (1, tq // LANES)))
            acc_scr[h] = jnp.exp2(m_prev - m_new) * acc_scr[h] + _dot(p.astype(BF16), v_ref[0, :, sl])
            m_scr[h] = m_new

    @pl.when(j < i)
    def _off_diagonal():
        update(False)

    @pl.when(j == i)
    def _diagonal():
        update(True)
        low_half = lax.broadcasted_iota(jnp.int32, (tq, LANES), 1) < LANES // 2
        for p in range(heads // 2):
            even, odd = acc_scr[2 * p], acc_scr[2 * p + 1]
            l_even = even[:, _sum_lane(0):_sum_lane(0) + 1]
            l_odd = odd[:, _sum_lane(1):_sum_lane(1) + 1]
            o_ref[0, :, p * LANES:(p + 1) * LANES] = jnp.where(low_half, even / l_even,
                                                               odd / l_odd).astype(o_ref.dtype)


def _paged_kernel(pt_ref, qlat_ref, qrope_ref, ckv_ref, kpe_ref, lat_hbm, rope_hbm, o_ref,
                  *scratch, heads, group, page, n_chunks, block_keys, kv_lora, rope):
    r = pl.program_id(0)
    n_req = pl.num_programs(0)
    t = qlat_ref.shape[0]
    lat_bufs = scratch[:PAGED_SLOTS]
    rope_bufs = scratch[PAGED_SLOTS:2 * PAGED_SLOTS]
    sem = scratch[2 * PAGED_SLOTS]

    def page_copies(req, chunk, slot, g):
        pg = pt_ref[req, chunk * group + g]
        rows = pl.ds(g * page, page)
        return (pltpu.make_async_copy(lat_hbm.at[pg], lat_bufs[slot].at[rows], sem.at[0, slot]),
                pltpu.make_async_copy(rope_hbm.at[pg], rope_bufs[slot].at[g], sem.at[1, slot]))

    def start_chunk(req, chunk, slot):
        for g in range(group):
            for kind, cp in enumerate(page_copies(req, chunk, slot, g)):
                cp.start(priority=(g + kind) % 2)

    def wait_chunk(req, chunk, slot):
        for g in range(group):
            for cp in page_copies(req, chunk, slot, g):
                cp.wait()

    @pl.when(r == 0)
    def _prime():
        for c in range(PAGED_AHEAD):
            start_chunk(0, c, c)

    q_lat = jnp.concatenate([qlat_ref[:, h * kv_lora:(h + 1) * kv_lora] for h in range(heads)], axis=0)
    q_rope = jnp.concatenate([qrope_ref[:, h * LANES:(h + 1) * LANES] for h in range(heads)],
                             axis=0)[:, :rope]

    def scores(lat, rp, rp_dims):
        return (lax.dot_general(q_lat.astype(lat.dtype), lat, NT, preferred_element_type=F32)
                + lax.dot_general(q_rope.astype(rp.dtype), rp, rp_dims, preferred_element_type=F32))

    def partial_softmax(s, lat):
        m = jnp.max(s, axis=1, keepdims=True)
        p = jnp.exp(s - m)
        return m, jnp.sum(p, axis=1, keepdims=True), _dot(p.astype(lat.dtype), lat)

    def combine(carry, parts):
        m_prev, l_prev, acc = carry
        m_new = m_prev
        for m, _, _ in parts:
            m_new = jnp.maximum(m_new, m)
        alpha = jnp.exp(m_prev - m_new)
        l_new, acc = alpha * l_prev, alpha * acc
        for m, l, a in parts:
            w = jnp.exp(m - m_new)
            l_new, acc = l_new + w * l, acc + w * a
        return m_new, l_new, acc

    def chunk_step(c, slot, carry):
        wait_chunk(r, c, slot)
        wrap = jnp.where(c + PAGED_AHEAD >= n_chunks, 1, 0)
        start_chunk(jnp.minimum(r + wrap, n_req - 1), c + PAGED_AHEAD - wrap * n_chunks,
                    (slot + PAGED_AHEAD) % PAGED_SLOTS)
        pages = block_keys // page
        lat_block = lambda blk: lat_bufs[slot][blk * block_keys:(blk + 1) * block_keys].astype(BF16)
        all_scores = []
        for blk in range(group // pages):
            rope_t = jnp.concatenate([rope_bufs[slot][blk * pages + g] for g in range(pages)], axis=1)
            all_scores.append(scores(lat_block(blk), rope_t.astype(BF16), NN))
        return combine(carry, [partial_softmax(s, lat_block(blk)) for blk, s in enumerate(all_scores)])

    def body(it, carry):
        for slot in range(PAGED_SLOTS):
            carry = chunk_step(PAGED_SLOTS * it + slot, slot, carry)
        return carry

    rows = heads * t
    init = (jnp.full((rows, 1), -jnp.inf, F32), jnp.zeros((rows, 1), F32), jnp.zeros((rows, kv_lora), F32))
    carry = lax.fori_loop(0, n_chunks // PAGED_SLOTS, body, init)

    @pl.when(r == n_req - 1)
    def _drain():
        for c in range(PAGED_AHEAD):
            wait_chunk(r, c, c)

    lat_new = ckv_ref[...]
    s_new = scores(lat_new, kpe_ref[...], NT)
    tok = lax.broadcasted_iota(jnp.int32, (rows, t), 0) % t
    key = lax.broadcasted_iota(jnp.int32, (rows, t), 1)
    s_new = jnp.where(key <= tok, s_new, -jnp.inf)
    _, l_fin, acc = combine(carry, [partial_softmax(s_new, lat_new)])
    o = acc / l_fin
    for h in range(heads):
        o_ref[:, h * kv_lora:(h + 1) * kv_lora] = o[h * t:(h + 1) * t].astype(o_ref.dtype)


def _hgrn_project(h_ref, wb_ref, lb_ref, g_scr, k_scr, q_scr, v_scr, fdim, vdim):
    z = _dot(h_ref[...], wb_ref[...])
    l0 = lb_ref[0:1, :]
    l1 = lb_ref[1:2, :]
    mx = jnp.maximum(l0, l1)
    e0 = jnp.exp(l0 - mx)
    e1 = jnp.exp(l1 - mx)
    lb = e1 / (e0 + e1)
    f = lb + (1.0 - lb) * jax.nn.sigmoid(z[:, :fdim])
    g_scr[...] = jnp.log(f)
    k_scr[...] = 1.0 - f
    q_scr[...] = _silu(z[:, fdim:2 * fdim])
    v_scr[...] = z[:, 2 * fdim:2 * fdim + vdim]
    return z[:, 2 * fdim + vdim:]


def _cumsum_rows(x):
    row = lax.broadcasted_iota(jnp.int32, x.shape, 0)
    shift = 1
    while shift < x.shape[0]:
        x = x + jnp.where(row >= shift, pltpu.roll(x, shift, 0), 0.0)
        shift *= 2
    return x


def _hgrn_chunk(qq, kk, g, v, st_ref, o_ref, rows, *, sub):
    c = qq.shape[0]
    n_pairs = st_ref.shape[0]
    hv = st_ref.shape[1] // 2
    hk = st_ref.shape[2] // 2
    mm = BF16 if sub % BF16_ROWS == 0 else F32
    bcum = _cumsum_rows(g)
    last = bcum[c - 1:c, :]
    q_in = (qq * jnp.exp(bcum)).astype(mm)
    k_dec = (kk * jnp.exp(last - bcum)).astype(mm)
    v_mm = v.astype(mm)
    n_sub = c // sub
    row_blk = lax.broadcasted_iota(jnp.int32, (c, hk), 0) // sub
    causal = lax.broadcasted_iota(jnp.int32, (c, c), 0) >= lax.broadcasted_iota(jnp.int32, (c, c), 1)
    lane_v = lax.broadcasted_iota(jnp.int32, (1, 2 * hv), 1) // hv
    bd = (lax.broadcasted_iota(jnp.int32, (2 * hv, 2 * hk), 0) // hv
          == lax.broadcasted_iota(jnp.int32, (2 * hv, 2 * hk), 1) // hk)
    def head_scores(hl):
        b_h, q_h = bcum[:, hl], qq[:, hl]
        refs = [b_h[j * sub + sub // 2:j * sub + sub // 2 + 1] for j in range(n_sub)]
        ref_rows = jnp.concatenate([jnp.broadcast_to(rj, (sub, hk)) for rj in refs], axis=0)
        k_all = kk[:, hl] * jnp.exp(ref_rows - b_h)
        k_cat = jnp.concatenate([jnp.where(row_blk == j, k_all, 0.0).astype(mm) for j in range(n_sub)], axis=1)
        q_cat = jnp.concatenate(
            [jnp.concatenate([jnp.zeros((j * sub, hk), F32)] * (j > 0)
                             + [q_h[j * sub:] * jnp.exp(b_h[j * sub:] - refs[j])], axis=0).astype(mm)
             for j in range(n_sub)], axis=1)
        att = lax.dot_general(q_cat, k_cat, NT, preferred_element_type=F32)
        return jnp.where(causal, att, 0.0).astype(mm)

    atts = [head_scores(slice(h * hk, (h + 1) * hk)) for h in range(2 * n_pairs)]
    for p in range(n_pairs):
        pk = slice(p * 2 * hk, (p + 1) * 2 * hk)
        pv = slice(p * 2 * hv, (p + 1) * 2 * hv)
        vp = v_mm[:, pv]
        st = st_ref[p]
        o = lax.dot_general(q_in[:, pk], st.astype(mm), NT, preferred_element_type=F32)
        for e in range(2):
            o = o + _dot(atts[2 * p + e], jnp.where(lane_v == e, vp, jnp.zeros_like(vp)))
        o_ref[rows, pv] = o
        upd = lax.dot_general(vp, k_dec[:, pk], TN, preferred_element_type=F32)
        st_ref[p] = st * jnp.exp(last[:, pk]) + jnp.where(bd, upd, 0.0)


def _hgrn_finish(o, zb, gho_ref, avg_ref, out_ref):
    ms = _dot((o * o).astype(BF16), avg_ref[...])
    out_ref[...] = (o * lax.rsqrt(ms + EPS) * gho_ref[...] * _silu(zb)).astype(out_ref.dtype)


def _compact_state(st):
    hv = st.shape[0] // 2
    hk = st.shape[1] // 2
    return jnp.concatenate([st[:hv, :hk], st[hv:, hk:]], axis=0)


def _expand_state(sc):
    hv = sc.shape[0] // 2
    z = jnp.zeros((hv, sc.shape[1]), F32)
    return jnp.concatenate([jnp.concatenate([sc[:hv], z], axis=1),
                            jnp.concatenate([z, sc[hv:]], axis=1)], axis=0)


def _hgrn_prompt_kernel(h_ref, wb_ref, lb_ref, gho_ref, avg_ref, out_ref, sfin_ref,
                        st_scr, g_scr, k_scr, q_scr, v_scr, o_scr, *, chunk, sub, fdim, vdim):
    tb = pl.program_id(1)

    @pl.when(tb == 0)
    def _init():
        st_scr[...] = jnp.zeros(st_scr.shape, F32)

    zb = _hgrn_project(h_ref, wb_ref, lb_ref, g_scr, k_scr, q_scr, v_scr, fdim, vdim)

    def body(ci, carry):
        rows = pl.ds(pl.multiple_of(ci * chunk, chunk), chunk)
        _hgrn_chunk(q_scr[rows, :], k_scr[rows, :], g_scr[rows, :], v_scr[rows, :], st_scr, o_scr, rows,
                    sub=sub)
        return carry

    lax.fori_loop(0, h_ref.shape[0] // chunk, body, 0, unroll=True)
    _hgrn_finish(o_scr[...], zb, gho_ref, avg_ref, out_ref)

    @pl.when(tb == pl.num_programs(1) - 1)
    def _final():
        for p in range(st_scr.shape[0]):
            sfin_ref[0, p] = _compact_state(st_scr[p])


def _hgrn_sample_kernel(h_ref, wb_ref, lb_ref, gho_ref, avg_ref, s0_ref, out_ref, sfin_ref,
                        st_scr, g_scr, k_scr, q_scr, v_scr, o_scr, *, t, fdim, vdim):
    zb = _hgrn_project(h_ref, wb_ref, lb_ref, g_scr, k_scr, q_scr, v_scr, fdim, vdim)

    for ri in range(s0_ref.shape[0]):
        st_ref = st_scr.at[ri]
        for p in range(st_ref.shape[0]):
            st_ref[p] = _expand_state(s0_ref[ri, p])
        rows = pl.ds(ri * t, t)
        _hgrn_chunk(q_scr[rows, :], k_scr[rows, :], g_scr[rows, :], v_scr[rows, :], st_ref, o_scr, rows,
                    sub=t)
        for p in range(st_ref.shape[0]):
            sfin_ref[ri, p] = _compact_state(st_ref[p])
    _hgrn_finish(o_scr[...], zb, gho_ref, avg_ref, out_ref)


def _xattn_heads(q, zc, head_k, head_v, heads, hdim):
    outs = []
    all_scores = [lax.dot_general(q[:, h * hdim:(h + 1) * hdim], head_k(h), NT, preferred_element_type=F32)
                  for h in range(heads)]
    for h, s in enumerate(all_scores):
        mv = head_v(h)
        s = s - jnp.max(s, axis=1, keepdims=True)
        p = jnp.exp(s)
        p = (p / jnp.sum(p, axis=1, keepdims=True)).astype(mv.dtype)
        outs.append(_dot(p, mv))
    return jnp.concatenate(outs, axis=1) * _silu(zc)


def _xattn_prompt_kernel(h_ref, wc_ref, mk_ref, mv_ref, out_ref, *, heads, hdim, scale):
    z = _dot(h_ref[0], wc_ref[...])
    w = heads * hdim
    q = (z[:, :w] * scale).astype(BF16)
    mk = mk_ref[0].astype(BF16)
    mv = mv_ref[0].astype(BF16)
    out_ref[0] = _xattn_heads(q, z[:, w:], lambda h: mk[:, h * hdim:(h + 1) * hdim],
                              lambda h: mv[:, h * hdim:(h + 1) * hdim], heads, hdim).astype(out_ref.dtype)


def _xattn_sample_kernel(h_ref, wc_ref, mk_ref, mv_ref, out_ref, *, heads, hdim, scale, t):
    z = _dot(h_ref[...], wc_ref[...])
    w = heads * hdim
    q = z[:, :w] * scale
    zc = z[:, w:]
    n_rows = mk_ref.shape[1]
    row_head = lax.broadcasted_iota(jnp.int32, (heads * t, n_rows), 0) // t
    col_head = lax.broadcasted_iota(jnp.int32, (heads * t, n_rows), 1) % heads
    all_scores = []
    for ri in range(mk_ref.shape[0]):
        qs = jnp.concatenate([q[ri * t:(ri + 1) * t, h * hdim:(h + 1) * hdim] for h in range(heads)], axis=0)
        all_scores.append(lax.dot_general(qs.astype(BF16), mk_ref[ri].astype(BF16), NT,
                                          preferred_element_type=F32))
    for ri, s in enumerate(all_scores):
        rows = slice(ri * t, (ri + 1) * t)
        s = jnp.where(row_head == col_head, s, -jnp.inf)
        p = jnp.exp(s - jnp.max(s, axis=1, keepdims=True))
        p = (p / jnp.sum(p, axis=1, keepdims=True)).astype(BF16)
        o = _dot(p, mv_ref[ri].astype(BF16))
        o = jnp.concatenate([o[h * t:(h + 1) * t] for h in range(heads)], axis=1)
        out_ref[rows, :] = (o * _silu(zc[rows])).astype(out_ref.dtype)


def _memkv_kernel(m_ref, g_ref, w_ref, k_ref, v_ref):
    kv = _dot(_rms(m_ref[...], g_ref[...]).astype(BF16), w_ref[...])
    w = kv.shape[1] // 2
    k_ref[...] = kv[:, :w]
    v_ref[...] = kv[:, w:]


def _merge_kernel(x_ref, h_ref, oa_ref, ob_ref, oc_ref, wuv_ref, wm_ref, wbr_ref, wout_ref, gf_ref, y_ref, *,
                  width, from_latent):
    h = h_ref[...]
    d = x_ref.shape[1]
    oa = oa_ref[...]
    if from_latent:
        oa = _dot(oa.astype(BF16), wuv_ref[...])
    else:
        oa = oa.astype(F32)
    za = _dot(h, wm_ref[:, :width])
    branches = ((oa * _silu(za)).astype(BF16), ob_ref[...].astype(BF16), oc_ref[...].astype(BF16))
    merged = jnp.zeros((x_ref.shape[0], d), F32)
    for n, o in enumerate(branches):
        gate = jax.nn.sigmoid(_dot(h, wm_ref[:, width + n * d:width + (n + 1) * d]))
        merged = merged + gate * _dot(o, wbr_ref[n])
    out = x_ref[...] + _dot(merged.astype(BF16), wout_ref[...])
    y_ref[...] = _rms(out, gf_ref[...])


def _rope_tables(pos, rope, lead):
    half = rope // 2
    inv = jnp.exp(-math.log(ROPE_BASE) * jnp.arange(half, dtype=F32) / half)
    ang = pos.astype(F32)[:, None] * inv[None, :]
    n = pos.shape[0]
    pad = jnp.zeros((n, LANES - lead - rope), F32)
    cos = jnp.concatenate([jnp.ones((n, lead), F32), jnp.cos(ang), jnp.cos(ang), pad], axis=1)
    sin_signed = jnp.concatenate([jnp.zeros((n, lead), F32), -jnp.sin(ang), jnp.sin(ang), pad], axis=1)
    return cos, sin_signed


def _pad_cols(w, width):
    return jnp.pad(w, [(0, 0)] * (w.ndim - 1) + [(0, width - w.shape[-1])])


def _block_diag(blocks):
    rows = sum(b.shape[0] for b in blocks)
    cols = sum(b.shape[1] for b in blocks)
    out = jnp.zeros((rows, cols), blocks[0].dtype)
    r = c = 0
    for b in blocks:
        out = lax.dynamic_update_slice(out, b, (r, c))
        r += b.shape[0]
        c += b.shape[1]
    return out


def _tile_rows(n, pref):
    t = min(n, pref)
    assert n % t == 0
    return t


def kernel(x_prompt, x_sample, mem_prompt, cache_kv_latent, cache_k_rope, page_table, state_hgrn, cache_mem_k, cache_mem_v, g_norm, w_in, g_q_lora, w_uq, g_kv_lora, w_uk, w_uv, lb_logits, g_hgrn_out, g_mem_norm, w_mem_kv, w_branch, w_out, g_final):
    depth = w_in.shape[0]
    assert depth == 1, "one layer per step"
    b, t, d = x_prompt.shape
    nb, ts, _ = x_sample.shape
    n_mem = mem_prompt.shape[1]
    n_phys, page, kv_lora = cache_kv_latent.shape[1:]
    rope = cache_k_rope.shape[-1]
    n_pages = page_table.shape[1]
    q_lora = g_q_lora.shape[-1]
    a_heads, a_qk = w_uq.shape[2:]
    a_nope = w_uk.shape[-1]
    a_vdim = w_uv.shape[-1]
    b_heads, b_expand, b_vdim = state_hgrn.shape[2:]
    fdim = b_heads * b_expand
    c_heads, c_hdim = cache_mem_k.shape[3:]
    width = w_branch.shape[2]
    assert a_qk == a_nope + rope and a_heads * a_vdim == width and b_heads * b_vdim == width
    assert c_heads * c_hdim == width and a_qk <= LANES and 2 * a_vdim == LANES and 2 * b_vdim == LANES
    assert b_expand == LANES and a_heads % 2 == 0 and b_heads % 2 == 0

    splits = (q_lora, kv_lora, rope, width, fdim, fdim, width, width, width, width, N_BRANCH * d)
    offs = np.concatenate([[0], np.cumsum(splits)])
    assert offs[-1] == w_in.shape[-1]
    w_in_t = jnp.swapaxes(w_in, 1, 2).reshape(w_in.shape[-1], d)

    def in_proj(*pieces, pad_to=None):
        rows = [w_in_t[offs[i]:offs[i + 1]] for i in pieces]
        if pad_to is not None:
            rows.append(jnp.zeros((pad_to - sum(r.shape[0] for r in rows), d), F32))
        return jnp.concatenate(rows, axis=0).astype(BF16).T

    a_scale = a_qk ** -0.5
    n_p = b * t
    n_s = nb * ts
    xp = x_prompt.reshape(n_p, d)
    xs = x_sample.reshape(n_s, d)
    row = lambda g: g.reshape(1, -1).astype(F32)

    w_a = in_proj(0, 1, 2, pad_to=q_lora + kv_lora + LANES)
    uq = w_uq.reshape(q_lora, a_heads, a_qk)
    uq_swapped = jnp.concatenate([jnp.zeros_like(uq[..., :a_nope]), uq[..., a_nope + rope // 2:],
                                  uq[..., a_nope:a_nope + rope // 2]], axis=-1)
    wq_prompt = jnp.concatenate([_pad_cols(uq, LANES).reshape(q_lora, -1),
                                 _pad_cols(uq_swapped, LANES).reshape(q_lora, -1)], axis=1).astype(BF16)
    wq_sample = jnp.concatenate([uq[..., :a_nope].reshape(q_lora, -1),
                                 _pad_cols(uq[..., a_nope:], LANES).reshape(q_lora, -1)], axis=1).astype(BF16)
    uk = w_uk.reshape(kv_lora, a_heads, a_nope)
    uv = w_uv.reshape(kv_lora, a_heads, a_vdim)
    uv_pad = jnp.stack([jnp.pad(uv[:, h], ((0, 0), ((h % 2) * a_vdim, LANES - a_vdim - (h % 2) * a_vdim)))
                        for h in range(a_heads)], axis=1)
    w_kv = jnp.concatenate([_pad_cols(uk, LANES).reshape(kv_lora, -1), uv_pad.reshape(kv_lora, -1)],
                           axis=1).astype(BF16)
    w_abs = _block_diag([uk[:, h].T for h in range(a_heads)]).astype(BF16)
    w_uv_bd = _block_diag([uv[:, h] for h in range(a_heads)]).astype(BF16)
    w_b = in_proj(4, 5, 6, 7)
    w_c = in_proj(8, 9)
    w_m = in_proj(3, 10)
    w_br = w_branch[0].astype(BF16)
    w_o = w_out[0].astype(BF16)
    w_mem = w_mem_kv[0].astype(BF16)
    gho = jnp.tile(g_hgrn_out[0], b_heads).reshape(1, width).astype(F32)
    head_of = jnp.arange(width) // b_vdim
    avg = ((head_of[:, None] == head_of[None, :]).astype(F32) / b_vdim).astype(BF16)
    lb2 = lb_logits[:2].astype(F32)

    n_past = n_pages * page
    cq_p, sq_p = _rope_tables(jnp.arange(t), rope, a_nope)
    ck_p, sk_p = _rope_tables(jnp.arange(t), rope, 0)
    tm_s = _tile_rows(n_s, 256)
    assert tm_s % ts == 0
    ck_s, sk_s = _rope_tables(n_past + (jnp.arange(tm_s) % ts), rope, 0)

    tm = _tile_rows(t, 1024)
    n_t = t // tm
    mla_common = dict(q_lora=q_lora, kv_lora=kv_lora, rope=rope)
    rowspec = lambda w: pl.BlockSpec((tm, w), lambda i: (i, 0))
    tabspec = pl.BlockSpec((tm, LANES), lambda i: (i % n_t, 0))
    hq = a_heads * LANES
    h_p, q_p, k_p, v_p, ckv_p, kpe_p = pl.pallas_call(
        functools.partial(_mla_prep_prompt_kernel, scale=a_scale * math.log2(math.e), nope=a_nope,
                          **mla_common),
        grid=(n_p // tm,),
        in_specs=[rowspec(d), _const_spec((1, d)), _const_spec(w_a.shape), _const_spec((1, q_lora)),
                  _const_spec(wq_prompt.shape), _const_spec((1, kv_lora)), _const_spec(w_kv.shape),
                  tabspec, tabspec, tabspec, tabspec],
        out_specs=[rowspec(d), rowspec(hq), rowspec(hq), rowspec(hq), rowspec(kv_lora),
                   pl.BlockSpec((1, rope, tm), lambda i: (i // n_t, 0, i % n_t))],
        out_shape=[jax.ShapeDtypeStruct((n_p, d), BF16), jax.ShapeDtypeStruct((n_p, hq), BF16),
                   jax.ShapeDtypeStruct((n_p, hq), BF16), jax.ShapeDtypeStruct((n_p, hq), BF16),
                   jax.ShapeDtypeStruct((n_p, kv_lora), F32), jax.ShapeDtypeStruct((b, rope, t), F32)],
        compiler_params=_params(("parallel",)),
    )(xp, row(g_norm), w_a, row(g_q_lora), wq_prompt, row(g_kv_lora), w_kv, cq_p, sq_p, ck_p, sk_p)

    srow = lambda w: pl.BlockSpec((tm_s, w), lambda i: (i, 0))
    hl = a_heads * kv_lora
    h_s, qlat_s, qrope_s, ckv_s, kpe_s = pl.pallas_call(
        functools.partial(_mla_prep_sample_kernel, scale=a_scale, nope_w=a_heads * a_nope, **mla_common),
        grid=(n_s // tm_s,),
        in_specs=[srow(d), _const_spec((1, d)), _const_spec(w_a.shape), _const_spec((1, q_lora)),
                  _const_spec(wq_sample.shape), _const_spec((1, kv_lora)), _const_spec(w_abs.shape),
                  _const_spec((tm_s, LANES)), _const_spec((tm_s, LANES))],
        out_specs=[srow(d), srow(hl), srow(hq), srow(kv_lora), srow(rope)],
        out_shape=[jax.ShapeDtypeStruct((n_s, d), BF16), jax.ShapeDtypeStruct((n_s, hl), F32),
                   jax.ShapeDtypeStruct((n_s, hq), F32), jax.ShapeDtypeStruct((n_s, kv_lora), F32),
                   jax.ShapeDtypeStruct((n_s, rope), F32)],
        compiler_params=_params(("parallel",)),
    )(xs, row(g_norm), w_a, row(g_q_lora), wq_sample, row(g_kv_lora), w_abs, ck_s, sk_s)

    tq = _tile_rows(t, 512)
    assert tq % LANES == 0
    nq = t // tq
    pairs = [(i, j) for i in range(nq) for j in range(i + 1)]
    i_tab = jnp.asarray([p[0] for p in pairs], jnp.int32)
    j_tab = jnp.asarray([p[1] for p in pairs], jnp.int32)
    qkv = lambda a: a.reshape(b, t, hq)
    oa_p = pl.pallas_call(
        functools.partial(_flash_kernel, heads=a_heads),
        grid_spec=pltpu.PrefetchScalarGridSpec(
            num_scalar_prefetch=2, grid=(b, len(pairs)),
            in_specs=[pl.BlockSpec((1, tq, hq), lambda bi, s, it, jt: (bi, it[s], 0)),
                      pl.BlockSpec((1, tq, hq), lambda bi, s, it, jt: (bi, jt[s], 0)),
                      pl.BlockSpec((1, tq, hq), lambda bi, s, it, jt: (bi, jt[s], 0))],
            out_specs=pl.BlockSpec((1, tq, width), lambda bi, s, it, jt: (bi, it[s], 0)),
            scratch_shapes=[pltpu.VMEM((a_heads, tq, LANES), F32)] * 2),
        out_shape=jax.ShapeDtypeStruct((b, t, width), BF16),
        compiler_params=_params(("parallel", "arbitrary")),
    )(i_tab, j_tab, qkv(q_p), qkv(k_p), qkv(v_p)).reshape(n_p, width)

    assert n_pages % PAGED_SLOTS == 0
    group = math.gcd(n_pages // PAGED_SLOTS, 32)
    n_chunks = n_pages // group
    block_keys = math.gcd(group, 8) * page
    req = lambda w: pl.BlockSpec((ts, w), lambda r, pt: (r, 0))
    olat_s = pl.pallas_call(
        functools.partial(_paged_kernel, heads=a_heads, group=group, page=page, n_chunks=n_chunks,
                          block_keys=block_keys,
                          kv_lora=kv_lora, rope=rope),
        grid_spec=pltpu.PrefetchScalarGridSpec(
            num_scalar_prefetch=1, grid=(nb,),
            in_specs=[req(hl), req(hq), req(kv_lora), req(rope),
                      pl.BlockSpec(memory_space=pl.ANY), pl.BlockSpec(memory_space=pl.ANY)],
            out_specs=req(hl),
            scratch_shapes=[pltpu.VMEM((group * page, kv_lora), F32)] * PAGED_SLOTS
            + [pltpu.VMEM((group, rope, page), F32)] * PAGED_SLOTS
            + [pltpu.SemaphoreType.DMA((2, PAGED_SLOTS))]),
        out_shape=jax.ShapeDtypeStruct((n_s, hl), F32),
        compiler_params=_params(("arbitrary",)),
    )(page_table, qlat_s, qrope_s, ckv_s, kpe_s, cache_kv_latent.reshape(n_phys, page, kv_lora),
      jnp.swapaxes(cache_k_rope, 2, 3).reshape(n_phys, rope, page))

    n_pairs = b_heads // 2
    tb = _tile_rows(t, 512)
    chunk = _tile_rows(tb, 64)
    sub = _tile_rows(chunk, 16)
    wb_cols = w_b.shape[1]
    hgrn_scratch = lambda rows, *lead: [pltpu.VMEM(lead + (n_pairs, LANES, 2 * b_expand), F32),
                                 pltpu.VMEM((rows, fdim), F32), pltpu.VMEM((rows, fdim), F32),
                                 pltpu.VMEM((rows, fdim), F32), pltpu.VMEM((rows, width), F32),
                                 pltpu.VMEM((rows, width), F32)]
    ob_p, st_p = pl.pallas_call(
        functools.partial(_hgrn_prompt_kernel, chunk=chunk, sub=sub, fdim=fdim, vdim=width),
        grid=(b, t // tb),
        in_specs=[pl.BlockSpec((tb, d), lambda bi, ti: (bi * (t // tb) + ti, 0)),
                  _const_spec((d, wb_cols)), _const_spec((2, fdim)), _const_spec((1, width)),
                  _const_spec((width, width))],
        out_specs=[pl.BlockSpec((tb, width), lambda bi, ti: (bi * (t // tb) + ti, 0)),
                   pl.BlockSpec((1, n_pairs, LANES, b_expand), lambda bi, ti: (bi, 0, 0, 0))],
        out_shape=[jax.ShapeDtypeStruct((n_p, width), BF16),
                   jax.ShapeDtypeStruct((b, n_pairs, LANES, b_expand), F32)],
        scratch_shapes=hgrn_scratch(tb),
        compiler_params=_params(("parallel", "arbitrary")),
    )(h_p, w_b, lb2, gho, avg)

    def to_pairs(s):
        n = s.shape[0]
        return s.reshape(n, n_pairs, 2, b_expand, b_vdim).transpose(0, 1, 2, 4, 3).reshape(
            n, n_pairs, LANES, b_expand)

    def from_pairs(s):
        n = s.shape[0]
        return s.reshape(n, n_pairs, 2, b_vdim, b_expand).transpose(0, 1, 2, 4, 3).reshape(
            n, b_heads, b_expand, b_vdim)

    rq = _tile_rows(nb, 8)
    ob_s, st_s = pl.pallas_call(
        functools.partial(_hgrn_sample_kernel, t=ts, fdim=fdim, vdim=width),
        grid=(nb // rq,),
        in_specs=[pl.BlockSpec((rq * ts, d), lambda i: (i, 0)),
                  _const_spec((d, wb_cols)), _const_spec((2, fdim)), _const_spec((1, width)),
                  _const_spec((width, width)),
                  pl.BlockSpec((rq, n_pairs, LANES, b_expand), lambda i: (i, 0, 0, 0))],
        out_specs=[pl.BlockSpec((rq * ts, width), lambda i: (i, 0)),
                   pl.BlockSpec((rq, n_pairs, LANES, b_expand), lambda i: (i, 0, 0, 0))],
        out_shape=[jax.ShapeDtypeStruct((n_s, width), BF16),
                   jax.ShapeDtypeStruct((nb, n_pairs, LANES, b_expand), F32)],
        scratch_shapes=hgrn_scratch(rq * ts, rq),
        compiler_params=_params(("parallel",)),
    )(h_s, w_b, lb2, gho, avg, to_pairs(state_hgrn.reshape(nb, b_heads, b_expand, b_vdim)))

    n_m = b * n_mem
    tmm = _tile_rows(n_m, 512)
    mk_p, mv_p = pl.pallas_call(
        _memkv_kernel,
        grid=(n_m // tmm,),
        in_specs=[pl.BlockSpec((tmm, d), lambda i: (i, 0)), _const_spec((1, d)), _const_spec(w_mem.shape)],
        out_specs=[pl.BlockSpec((tmm, width), lambda i: (i, 0)), pl.BlockSpec((tmm, width), lambda i: (i, 0))],
        out_shape=[jax.ShapeDtypeStruct((n_m, width), F32), jax.ShapeDtypeStruct((n_m, width), F32)],
        compiler_params=_params(("parallel",)),
    )(mem_prompt.reshape(n_m, d), row(g_mem_norm), w_mem)

    c_scale = c_hdim ** -0.5
    tx = _tile_rows(t, 1024)
    memspec = pl.BlockSpec((1, n_mem, width), lambda bi, ti: (bi, 0, 0))
    oc_p = pl.pallas_call(
        functools.partial(_xattn_prompt_kernel, heads=c_heads, hdim=c_hdim, scale=c_scale),
        grid=(b, t // tx),
        in_specs=[pl.BlockSpec((1, tx, d), lambda bi, ti: (bi, ti, 0)), _const_spec(w_c.shape), memspec, memspec],
        out_specs=pl.BlockSpec((1, tx, width), lambda bi, ti: (bi, ti, 0)),
        out_shape=jax.ShapeDtypeStruct((b, t, width), BF16),
        compiler_params=_params(("parallel", "parallel")),
    )(h_p.reshape(b, t, d), w_c, mk_p.reshape(b, n_mem, width), mv_p.reshape(b, n_mem, width)).reshape(n_p, width)

    rx = _tile_rows(nb, 8)
    smem = pl.BlockSpec((rx, n_mem * c_heads, c_hdim), lambda i: (i, 0, 0))
    oc_s = pl.pallas_call(
        functools.partial(_xattn_sample_kernel, heads=c_heads, hdim=c_hdim, scale=c_scale, t=ts),
        grid=(nb // rx,),
        in_specs=[pl.BlockSpec((rx * ts, d), lambda i: (i, 0)), _const_spec(w_c.shape), smem, smem],
        out_specs=pl.BlockSpec((rx * ts, width), lambda i: (i, 0)),
        out_shape=jax.ShapeDtypeStruct((n_s, width), F32),
        compiler_params=_params(("parallel",)),
    )(h_s, w_c, cache_mem_k.reshape(nb, n_mem * c_heads, c_hdim), cache_mem_v.reshape(nb, n_mem * c_heads, c_hdim))

    def merge(x2, h2, oa, ob, oc, from_latent):
        n = x2.shape[0]
        tmg = _tile_rows(n, 512)
        rs = lambda w: pl.BlockSpec((tmg, w), lambda i: (i, 0))
        return pl.pallas_call(
            functools.partial(_merge_kernel, width=width, from_latent=from_latent),
            grid=(n // tmg,),
            in_specs=[rs(d), rs(d), rs(oa.shape[1]), rs(width), rs(width), _const_spec(w_uv_bd.shape),
                      _const_spec(w_m.shape), _const_spec(w_br.shape), _const_spec(w_o.shape),
                      _const_spec((1, d))],
            out_specs=rs(d),
            out_shape=jax.ShapeDtypeStruct((n, d), F32),
            compiler_params=_params(("parallel",)),
        )(x2, h2, oa, ob, oc, w_uv_bd, w_m, w_br, w_o, row(g_final))

    y_p = merge(xp, h_p, oa_p, ob_p, oc_p, False).reshape(b, t, d)
    y_s = merge(xs, h_s, olat_s, ob_s, oc_s, True).reshape(nb, ts, d)

    return (y_p, y_s,
            ckv_p.reshape(1, b, t, kv_lora), jnp.swapaxes(kpe_p, 1, 2).reshape(1, b, t, rope),
            from_pairs(st_p)[None],
            mk_p.reshape(1, b, n_mem, c_heads, c_hdim), mv_p.reshape(1, b, n_mem, c_heads, c_hdim),
            ckv_s.reshape(1, nb, ts, kv_lora), kpe_s.reshape(1, nb, ts, rope),
            from_pairs(st_s)[None])
```
